```python
import math
import jax
import jax.numpy as jnp
from jax import lax
import numpy as np

D_MODEL = 1024
BATCH = 2
SEQ = 16384
DEPTH = 2

GRID_W = 64
CTX_LEN = 256
ADALN_CHUNKS = 6
RMS_EPS = 1e-6

ATT_HEADS = 8
ATT_KV_HEADS = 2
ATT_GROUP = ATT_HEADS // ATT_KV_HEADS
HEAD_DIM = 64
ATT_WIDTH = ATT_HEADS * HEAD_DIM
ATT_KV_WIDTH = ATT_KV_HEADS * HEAD_DIM
Q_BLOCK = 128
ROPE_THETA = 10000.0

HG_HEADS = 4
HG_DK = 64
HG_DV = 64
HG_WIDTH = HG_HEADS * HG_DK
HG_VWIDTH = HG_HEADS * HG_DV
HG_CHUNK = 128

SSM_WIDTH = 256
SSM_GROUP = 16
SSM_GROUPS = SSM_WIDTH // SSM_GROUP
SSM_STATE = 64

FFN_HIDDEN = ((8 * D_MODEL + 3 * 256 - 1) // (3 * 256)) * 256

IN_SIZES = (ATT_WIDTH, ATT_KV_WIDTH, ATT_KV_WIDTH, HG_WIDTH, HG_WIDTH, HG_WIDTH, HG_VWIDTH, HG_VWIDTH, SSM_WIDTH, D_MODEL, D_MODEL, D_MODEL)
IN_OFFSETS = tuple(int(o) for o in np.cumsum(IN_SIZES)[:-1])
N_IN = sum(IN_SIZES)

kernel_name = 'hybrid_gqa_hgrn2_s5_dit_block'


def rms_norm(x, g):
    xf = x.astype(jnp.float32)
    y = xf * lax.rsqrt(jnp.mean(xf * xf, axis=-1, keepdims=True) + RMS_EPS)
    return (y * g.astype(jnp.float32)).astype(x.dtype)


def modulate(x, g, shift, scale):
    return rms_norm(x, g) * (1.0 + scale) + shift


def split_heads(a, n_heads):
    return a.reshape(a.shape[0], a.shape[1], n_heads, a.shape[-1] // n_heads)


def axial_rope_tables(n_tokens):
    rows = n_tokens // GRID_W
    row_ids = jnp.repeat(jnp.arange(rows, dtype=jnp.float32), GRID_W)
    col_ids = jnp.broadcast_to(jnp.arange(GRID_W, dtype=jnp.float32), (rows, GRID_W)).reshape(-1)
    axis_dim = HEAD_DIM // 2
    inv_freq = ROPE_THETA ** (-jnp.arange(0, axis_dim, 2, dtype=jnp.float32) / axis_dim)
    ang_r = row_ids[:, None] * inv_freq
    ang_c = col_ids[:, None] * inv_freq
    return (jnp.cos(ang_r), jnp.sin(ang_r), jnp.cos(ang_c), jnp.sin(ang_c))


def rotate_pairs(x, cos, sin):
    x1, x2 = jnp.split(x, 2, axis=-1)
    cos, sin = cos[None, :, None, :], sin[None, :, None, :]
    return jnp.concatenate([x1 * cos - x2 * sin, x2 * cos + x1 * sin], axis=-1)


def apply_axial_rope(x, rope):
    cos_r, sin_r, cos_c, sin_c = rope
    x_row, x_col = jnp.split(x.astype(jnp.float32), 2, axis=-1)
    out = jnp.concatenate([rotate_pairs(x_row, cos_r, sin_r), rotate_pairs(x_col, cos_c, sin_c)], axis=-1)
    return out.astype(x.dtype)


def gqa_attend(q, k, v):
    s = jnp.einsum('bqhgd,bkhd->bhgqk', q, k).astype(jnp.float32) * (HEAD_DIM ** -0.5)
    p = jax.nn.softmax(s, axis=-1).astype(v.dtype)
    return jnp.einsum('bhgqk,bkhd->bqhgd', p, v)


def attention_branch(z_ctx, z_lat, q_g, k_g, rope, with_ctx):
    (qc, kc, vc), (ql, kl, vl) = z_ctx, z_lat
    b_, t = ql.shape[0], ql.shape[1]
    t_ctx = qc.shape[1]
    k_ctx = rms_norm(split_heads(kc, ATT_KV_HEADS), k_g)
    v_ctx = split_heads(vc, ATT_KV_HEADS)
    k_lat = apply_axial_rope(rms_norm(split_heads(kl, ATT_KV_HEADS), k_g), rope)
    q_lat = apply_axial_rope(rms_norm(split_heads(ql, ATT_HEADS), q_g), rope)
    k_all = jnp.concatenate([k_ctx, k_lat], axis=1)
    v_all = jnp.concatenate([v_ctx, split_heads(vl, ATT_KV_HEADS)], axis=1)
    q_blocks = q_lat.reshape(b_, t // Q_BLOCK, Q_BLOCK, ATT_KV_HEADS, ATT_GROUP, HEAD_DIM).swapaxes(0, 1)
    o_blocks = lax.map(lambda qb: gqa_attend(qb, k_all, v_all), q_blocks)
    y_lat = o_blocks.swapaxes(0, 1).reshape(b_, t, ATT_WIDTH)
    y_ctx = None
    if with_ctx:
        q_ctx = rms_norm(split_heads(qc, ATT_HEADS), q_g).reshape(b_, t_ctx, ATT_KV_HEADS, ATT_GROUP, HEAD_DIM)
        y_ctx = gqa_attend(q_ctx, k_ctx, v_ctx).reshape(b_, t_ctx, ATT_WIDTH)
    return y_ctx, y_lat


def gla_chunked(q, k, v, log_f, s0):
    b_, t, h, _ = q.shape
    n = t // HG_CHUNK

    def chunks(a):
        return a.reshape(b_, n, HG_CHUNK, h, a.shape[-1]).transpose(1, 0, 3, 2, 4)

    lower = jnp.tril(jnp.ones((HG_CHUNK, HG_CHUNK), dtype=bool))[:, :, None]

    def step(state, inp):
        qc, kc, vc, gc = inp
        cum = jnp.cumsum(gc, axis=2)
        o_inter = jnp.einsum('bhld,bhde->bhle', qc * jnp.exp(cum), state)
        diff = cum[:, :, :, None, :] - cum[:, :, None, :, :]
        decay = jnp.where(lower, jnp.exp(jnp.where(lower, diff, 0.0)), 0.0)
        scores = jnp.einsum('bhtd,bhsd,bhtsd->bhts', qc, kc, decay)
        o = o_inter + jnp.einsum('bhts,bhse->bhte', scores, vc)
        last = cum[:, :, -1:, :]
        state = jnp.exp(last[:, :, 0, :, None]) * state + jnp.einsum('bhsd,bhse->bhde', kc * jnp.exp(last - cum), vc)
        return state, o

    state, o = lax.scan(step, s0, (chunks(q), chunks(k), chunks(v), chunks(log_f)))
    o = o.transpose(1, 0, 3, 2, 4).reshape(b_, t, h, o.shape[-1])
    return o, state


def hgrn_forget(f_pre, lb):
    f_pre = f_pre.astype(jnp.float32)
    f = lb + (1.0 - lb) * jax.nn.sigmoid(f_pre)
    key_in = (1.0 - lb) * jax.nn.sigmoid(-f_pre)
    return split_heads(jnp.log(f), HG_HEADS), split_heads(key_in, HG_HEADS)


def hgrn2_branch(z_ctx, z_lat, lb, norm_g, with_ctx):
    out_dtype = z_lat[0].dtype

    def prep(z):
        q, f_fwd, f_bwd, i, g = z
        q = split_heads(jax.nn.silu(q.astype(jnp.float32)), HG_HEADS)
        v = split_heads(i.astype(jnp.float32), HG_HEADS)
        return q, v, (hgrn_forget(f_fwd, lb[0]), hgrn_forget(f_bwd, lb[1])), g

    qc, vc, fc, gc = prep(z_ctx)
    ql, vl, fl, gl = prep(z_lat)
    o_ctx, o_lat = [], []
    for d, reverse in ((0, False), (1, True)):
        flip = (lambda a: jnp.flip(a, axis=1)) if reverse else (lambda a: a)
        (lf_c, k_c), (lf_l, k_l) = fc[d], fl[d]
        s0 = jnp.zeros((qc.shape[0], HG_HEADS, HG_DK, HG_DV), jnp.float32)
        oc, s_ctx = gla_chunked(flip(qc), flip(k_c), flip(vc), flip(lf_c), s0)
        ol, _ = gla_chunked(flip(ql), flip(k_l), flip(vl), flip(lf_l), s_ctx)
        o_ctx.append(flip(oc))
        o_lat.append(flip(ol))

    def readout(o, g):
        o = rms_norm(o, norm_g).reshape(o.shape[0], o.shape[1], HG_VWIDTH)
        return (o * jax.nn.silu(g.astype(jnp.float32))).astype(out_dtype)

    y_ctx = readout(o_ctx[0] + o_ctx[1], gc) if with_ctx else None
    return y_ctx, readout(o_lat[0] + o_lat[1], gl)


def s5_discretise(lam_re, lam_im, log_dt, b_re, b_im):
    lam = lax.complex(lam_re.astype(jnp.float32), lam_im.astype(jnp.float32))
    dt = jnp.exp(log_dt.astype(jnp.float32))[:, None]
    lam_bar = jnp.exp(lam * dt)
    b = lax.complex(b_re.astype(jnp.float32), b_im.astype(jnp.float32))
    b_bar = ((lam_bar - 1.0) / lam)[..., None] * b
    return lam_bar, b_bar


def linear_recurrence_op(e1, e2):
    a1, b1 = e1
    a2, b2 = e2
    return a1 * a2, a2 * b1 + b2


def s5_scan(u, lam_bar, b_bar, x0, reverse):
    if reverse:
        u = jnp.flip(u, axis=1)
    bu = jnp.einsum('btgc,gpc->tbgp', u.astype(jnp.complex64), b_bar)
    bu = bu.at[0].add(lam_bar * x0)
    a = jnp.broadcast_to(lam_bar, bu.shape)
    _, xs = lax.associative_scan(linear_recurrence_op, (a, bu), axis=0)
    xs = jnp.moveaxis(xs, 0, 1)
    return jnp.flip(xs, axis=1) if reverse else xs


def s5_readout(xs, c_mat):
    return jnp.einsum('btgp,gcp->btgc', xs, c_mat).real


def s5_branch(u_ctx, u_lat, p, with_ctx):
    out_dtype = u_lat.dtype

    def groups(u):
        return u.astype(jnp.float32).reshape(u.shape[0], u.shape[1], SSM_GROUPS, SSM_GROUP)

    uc, ul = groups(u_ctx), groups(u_lat)
    d_skip = p['ssm_d'].astype(jnp.float32).reshape(SSM_GROUPS, SSM_GROUP)
    y_lat = d_skip * ul
    y_ctx = d_skip * uc if with_ctx else None
    for d, reverse in ((0, False), (1, True)):
        lam_bar, b_bar = s5_discretise(p['ssm_lam_re'][d], p['ssm_lam_im'][d], p['ssm_log_dt'][d], p['ssm_b_re'][d], p['ssm_b_im'][d])
        c_mat = lax.complex(p['ssm_c_re'][d].astype(jnp.float32), p['ssm_c_im'][d].astype(jnp.float32))
        x0 = jnp.zeros((uc.shape[0], SSM_GROUPS, SSM_STATE), jnp.complex64)
        xs_ctx = s5_scan(uc, lam_bar, b_bar, x0, reverse)
        x_init = xs_ctx[:, 0] if reverse else xs_ctx[:, -1]
        xs_lat = s5_scan(ul, lam_bar, b_bar, x_init, reverse)
        y_lat = y_lat + s5_readout(xs_lat, c_mat)
        if with_ctx:
            y_ctx = y_ctx + s5_readout(xs_ctx, c_mat)

    def glu(y):
        z = jax.nn.gelu(y.reshape(y.shape[0], y.shape[1], SSM_WIDTH))
        return (z * jax.nn.sigmoid(z @ p['w_glu'] + p['b_glu'])).astype(out_dtype)

    return (glu(y_ctx) if with_ctx else None), glu(y_lat)


def mixer_sublayer(h_ctx, h_lat, p, lb, rope, with_ctx):
    z_ctx = jnp.split(h_ctx @ p['w_in'], IN_OFFSETS, axis=-1)
    z_lat = jnp.split(h_lat @ p['w_in'], IN_OFFSETS, axis=-1)
    a_ctx, a_lat = attention_branch(z_ctx[0:3], z_lat[0:3], p['q_norm_g'], p['k_norm_g'], rope, with_ctx)
    r_ctx, r_lat = hgrn2_branch(z_ctx[3:8], z_lat[3:8], lb, p['hgrn_norm_g'], with_ctx)
    s_ctx, s_lat = s5_branch(z_ctx[8], z_lat[8], p, with_ctx)

    def merge(y_att, y_rec, y_ssm, gates):
        g_att, g_rec, g_ssm = gates
        m = (jax.nn.sigmoid(g_att) * (y_att @ p['w_br_attn'])
             + jax.nn.sigmoid(g_rec) * (y_rec @ p['w_br_hgrn'])
             + jax.nn.sigmoid(g_ssm) * (y_ssm @ p['w_br_ssm']))
        return m @ p['w_out']

    y_lat = merge(a_lat, r_lat, s_lat, z_lat[9:12])
    y_ctx = merge(a_ctx, r_ctx, s_ctx, z_ctx[9:12]) if with_ctx else None
    return y_ctx, y_lat


def swiglu_ffn(h, w_up, w_down):
    a, b = jnp.split(h @ w_up, 2, axis=-1)
    return (jax.nn.silu(a) * b) @ w_down


def setup_inputs(seed: int = 0) -> dict:
    key = jax.random.key(seed)
    ks = jax.random.split(key, 32)
    f32 = jnp.float32
    L, G, P, C = DEPTH, SSM_GROUPS, SSM_STATE, SSM_GROUP

    def normal(k, shape, scale):
        return jax.random.normal(k, shape, f32) * scale

    def gain(k, shape):
        return 1.0 + normal(k, shape, 0.02)

    lam_im_base = jnp.pi * jnp.arange(P, dtype=f32)
    return {
        'x': normal(ks[0], (BATCH, SEQ, D_MODEL), 1.0),
        'c': normal(ks[1], (BATCH, D_MODEL), 1.0),
        'ctx': normal(ks[2], (BATCH, CTX_LEN, D_MODEL), 1.0),
        'c_ctx': normal(ks[3], (D_MODEL,), 1.0),
        'w_mod': normal(ks[4], (L, D_MODEL, ADALN_CHUNKS * D_MODEL), 0.5 * D_MODEL ** -0.5),
        'b_mod': normal(ks[5], (L, ADALN_CHUNKS * D_MODEL), 0.01),
        'norm1_g': gain(ks[6], (L, D_MODEL)),
        'norm2_g': gain(ks[7], (L, D_MODEL)),
        'w_in': normal(ks[8], (L, D_MODEL, N_IN), D_MODEL ** -0.5),
        'q_norm_g': gain(ks[9], (L, HEAD_DIM)),
        'k_norm_g': gain(ks[10], (L, HEAD_DIM)),
        'hgrn_lb': normal(ks[11], (L, 2, HG_WIDTH), 1.0),
        'hgrn_norm_g': gain(ks[12], (L, HG_DV)),
        'ssm_lam_re': -0.5 + normal(ks[13], (L, 2, G, P), 0.01),
        'ssm_lam_im': lam_im_base + normal(ks[14], (L, 2, G, P), 0.01),
        'ssm_log_dt': jax.random.uniform(ks[15], (L, 2, G), f32, math.log(1e-3), math.log(1e-1)),
        'ssm_b_re': normal(ks[16], (L, 2, G, P, C), (2 * C) ** -0.5),
        'ssm_b_im': normal(ks[17], (L, 2, G, P, C), (2 * C) ** -0.5),
        'ssm_c_re': normal(ks[18], (L, 2, G, C, P), P ** -0.5),
        'ssm_c_im': normal(ks[19], (L, 2, G, C, P), P ** -0.5),
        'ssm_d': normal(ks[20], (L, SSM_WIDTH), 1.0),
        'w_glu': normal(ks[21], (L, SSM_WIDTH, SSM_WIDTH), SSM_WIDTH ** -0.5),
        'b_glu': normal(ks[22], (L, SSM_WIDTH), 0.01),
        'w_br_attn': normal(ks[23], (L, ATT_WIDTH, D_MODEL), ATT_WIDTH ** -0.5),
        'w_br_hgrn': normal(ks[24], (L, HG_VWIDTH, D_MODEL), HG_VWIDTH ** -0.5),
        'w_br_ssm': normal(ks[25], (L, SSM_WIDTH, D_MODEL), SSM_WIDTH ** -0.5),
        'w_out': normal(ks[26], (L, D_MODEL, D_MODEL), D_MODEL ** -0.5),
        'w_ffn_up': normal(ks[27], (L, D_MODEL, 2 * FFN_HIDDEN), D_MODEL ** -0.5),
        'w_ffn_down': normal(ks[28], (L, FFN_HIDDEN, D_MODEL), FFN_HIDDEN ** -0.5),
    }


def reference(x, c, ctx, c_ctx, w_mod, b_mod, norm1_g, norm2_g, w_in, q_norm_g, k_norm_g, hgrn_lb, hgrn_norm_g,
              ssm_lam_re, ssm_lam_im, ssm_log_dt, ssm_b_re, ssm_b_im, ssm_c_re, ssm_c_im, ssm_d, w_glu, b_glu,
              w_br_attn, w_br_hgrn, w_br_ssm, w_out, w_ffn_up, w_ffn_down):
    rope = axial_rope_tables(x.shape[1])
    lb_soft = jax.nn.softmax(hgrn_lb.astype(jnp.float32), axis=0)
    lower_bounds = jnp.cumsum(lb_soft, axis=0) - lb_soft[0]
    cond_lat = jax.nn.silu(c)
    cond_ctx = jax.nn.silu(c_ctx)
    x_lat, x_ctx = x, ctx
    for l in range(DEPTH):
        with_ctx = l < DEPTH - 1
        sh1, sc1, g1, sh2, sc2, g2 = jnp.split((cond_lat @ w_mod[l] + b_mod[l])[:, None, :], ADALN_CHUNKS, axis=-1)
        csh1, csc1, cg1, csh2, csc2, cg2 = jnp.split(cond_ctx @ w_mod[l] + b_mod[l], ADALN_CHUNKS, axis=-1)
        p = {
            'w_in': w_in[l], 'q_norm_g': q_norm_g[l], 'k_norm_g': k_norm_g[l], 'hgrn_norm_g': hgrn_norm_g[l],
            'ssm_lam_re': ssm_lam_re[l], 'ssm_lam_im': ssm_lam_im[l], 'ssm_log_dt': ssm_log_dt[l],
            'ssm_b_re': ssm_b_re[l], 'ssm_b_im': ssm_b_im[l], 'ssm_c_re': ssm_c_re[l], 'ssm_c_im': ssm_c_im[l],
            'ssm_d': ssm_d[l], 'w_glu': w_glu[l], 'b_glu': b_glu[l],
            'w_br_attn': w_br_attn[l], 'w_br_hgrn': w_br_hgrn[l], 'w_br_ssm': w_br_ssm[l], 'w_out': w_out[l],
        }
        h_lat = modulate(x_lat, norm1_g[l], sh1, sc1)
        h_ctx = modulate(x_ctx, norm1_g[l], csh1, csc1)
        y_ctx, y_lat = mixer_sublayer(h_ctx, h_lat, p, lower_bounds[l], rope, with_ctx)
        x_lat = x_lat + g1 * y_lat
        x_lat = x_lat + g2 * swiglu_ffn(modulate(x_lat, norm2_g[l], sh2, sc2), w_ffn_up[l], w_ffn_down[l])
        if with_ctx:
            x_ctx = x_ctx + cg1 * y_ctx
            x_ctx = x_ctx + cg2 * swiglu_ffn(modulate(x_ctx, norm2_g[l], csh2, csc2), w_ffn_up[l], w_ffn_down[l])
    return x_lat
```

```python
import functools
import math

import jax
import jax.numpy as jnp
from jax import lax
from jax.experimental import pallas as pl
from jax.experimental.pallas import tpu as pltpu

F32 = jnp.float32
BF16 = jnp.bfloat16

D_MODEL = 1024
GRID_W = 64
RMS_EPS = 1e-6
ADALN_CHUNKS = 6
ATT_HEADS = 8
ATT_KV_HEADS = 2
ATT_GROUP = ATT_HEADS // ATT_KV_HEADS
HEAD_DIM = 64
ATT_WIDTH = ATT_HEADS * HEAD_DIM
ATT_KV_WIDTH = ATT_KV_HEADS * HEAD_DIM
ROPE_THETA = 10000.0
HG_HEADS = 4
HG_DK = 64
HG_WIDTH = HG_HEADS * HG_DK
SSM_WIDTH = 256
SSM_GROUP = 16
SSM_GROUPS = SSM_WIDTH // SSM_GROUP
SSM_STATE = 64
FFN_HIDDEN = 2816
N_IN = 5376
O_Q, O_K, O_V, O_HQ, O_HF, O_HI, O_HG, O_U, O_GATE = 0, 512, 640, 768, 1024, 1536, 1792, 2048, 2304

V7X_LANES = 128
V7X_VMEM_BYTES = 64 * 1024 * 1024
MIB = 1024 * 1024

HG_TILE = 128
HG_SUB = 16
S5_CHUNK = 128


def _dot(a, b):
    return jnp.dot(a, b, preferred_element_type=F32)


def _dot_nt(a, b):
    return lax.dot_general(a, b, (((1,), (1,)), ((), ())), preferred_element_type=F32)


def _dot_tn(a, b):
    return lax.dot_general(a, b, (((0,), (0,)), ((), ())), preferred_element_type=F32)


def _split(x):
    hi = x.astype(BF16)
    lo = (x - hi.astype(F32)).astype(BF16)
    return hi, lo


def _dot_sel(sel_bf16, x):
    hi, lo = _split(x)
    return _dot(sel_bf16, hi) + _dot(sel_bf16, lo)


def _dot3(a, b):
    ah, al = _split(a)
    bh, bl = _split(b)
    return _dot(ah, bh) + (_dot(ah, bl) + _dot(al, bh))


def _sigmoid(x):
    return jax.nn.sigmoid(x)


def _cparams(n_axes, vmem_mib):
    return pltpu.CompilerParams(
        dimension_semantics=("arbitrary",) * n_axes,
        vmem_limit_bytes=min(vmem_mib * MIB, V7X_VMEM_BYTES - 4 * MIB),
    )


def _const_spec(shape):
    nd = len(shape)
    return pl.BlockSpec(shape, lambda *_: (0,) * nd, pipeline_mode=pl.Buffered(1))


def _mod_kernel(c_ref, w_ref, b_ref, o_ref):
    c = c_ref[...]
    s = c * _sigmoid(c)
    o_ref[0] = _dot3(s, w_ref[0]) + b_ref[0]


def _modulation(cond8, w_mod, b_mod):
    n_layers, d, n = w_mod.shape
    nb = 1536
    return pl.pallas_call(
        _mod_kernel,
        grid=(n_layers, n // nb),
        in_specs=[
            pl.BlockSpec((8, d), lambda l, j: (0, 0)),
            pl.BlockSpec((1, d, nb), lambda l, j: (l, 0, j)),
            pl.BlockSpec((1, 1, nb), lambda l, j: (l, 0, j)),
        ],
        out_specs=pl.BlockSpec((1, 8, nb), lambda l, j: (l, 0, j)),
        out_shape=jax.ShapeDtypeStruct((n_layers, 8, n), F32),
        compiler_params=_cparams(2, 40),
        name="adaln_modulation",
    )(cond8, w_mod, b_mod.reshape(n_layers, 1, n))


def _rope128(x, c, s1, s2):
    return x * c + pltpu.roll(x, V7X_LANES - 16, 1) * s1 + pltpu.roll(x, 16, 1) * s2


def _inproj_kernel(x_ref, sh_ref, a_ref, w_ref, qg_ref, kg_ref, c_ref, s1_ref, s2_ref, gmq_ref, gmk_ref,
                   q_ref, k_ref, v_ref, hq_ref, hf_ref, hi_ref, hg_ref, ut_ref, gate_ref):
    x = x_ref[0]
    ms = jnp.mean(x * x, axis=-1, keepdims=True)
    h = (x * lax.rsqrt(ms + RMS_EPS)) * a_ref[0] + sh_ref[0]
    hb = h.astype(BF16)

    def proj(lo, hi):
        return _dot(hb, w_ref[:, lo:hi])

    c, s1, s2 = c_ref[...], s1_ref[...], s2_ref[...]

    zq = proj(O_Q, O_K)
    msq = _dot((zq * zq).astype(BF16), gmq_ref[...])
    qn = zq * lax.rsqrt(msq + RMS_EPS) * qg_ref[...]
    for j in range(ATT_WIDTH // V7X_LANES):
        sl = slice(j * V7X_LANES, (j + 1) * V7X_LANES)
        q_ref[0, :, sl] = _rope128(qn[:, sl], c, s1, s2).astype(BF16)

    zk = proj(O_K, O_V)
    msk = _dot((zk * zk).astype(BF16), gmk_ref[...])
    kn = zk * lax.rsqrt(msk + RMS_EPS) * kg_ref[...]
    k_ref[0] = _rope128(kn, c, s1, s2).astype(BF16)

    v_ref[0] = proj(O_V, O_HQ).astype(BF16)
    hq_ref[0] = proj(O_HQ, O_HF).astype(BF16)
    hf_ref[0] = proj(O_HF, O_HI)
    hi_ref[0] = proj(O_HI, O_HG).astype(BF16)
    hg_ref[0] = proj(O_HG, O_U).astype(BF16)
    ut_ref[0] = proj(O_U, O_GATE).T
    for j in range(3):
        lo = O_GATE + j * D_MODEL
        gate_ref[0, :, j * D_MODEL:(j + 1) * D_MODEL] = _sigmoid(proj(lo, lo + D_MODEL)).astype(BF16)


def _inproj(x, sh, a, w_in, qg, kg, rope, gmq, gmk):
    b, t, d = x.shape
    tm = min(512, t)
    c, s1, s2 = rope
    row = lambda bi, i: (bi, i, 0)
    vec = lambda bi, i: (bi, 0, 0)
    tab = lambda bi, i: (i, 0)
    outs = [
        (ATT_WIDTH, BF16), (ATT_KV_WIDTH, BF16), (ATT_KV_WIDTH, BF16),
        (HG_WIDTH, BF16), (2 * HG_WIDTH, F32), (HG_WIDTH, BF16), (HG_WIDTH, BF16),
    ]
    out_shape = [jax.ShapeDtypeStruct((b, t, w), dt) for w, dt in outs]
    out_specs = [pl.BlockSpec((1, tm, w), row) for w, _ in outs]
    out_shape.append(jax.ShapeDtypeStruct((b, SSM_WIDTH, t), F32))
    out_specs.append(pl.BlockSpec((1, SSM_WIDTH, tm), lambda bi, i: (bi, 0, i)))
    out_shape.append(jax.ShapeDtypeStruct((b, t, 3 * D_MODEL), BF16))
    out_specs.append(pl.BlockSpec((1, tm, 3 * D_MODEL), row))
    return pl.pallas_call(
        _inproj_kernel,
        grid=(b, t // tm),
        in_specs=[
            pl.BlockSpec((1, tm, d), row),
            pl.BlockSpec((1, 1, d), vec),
            pl.BlockSpec((1, 1, d), vec),
            _const_spec((d, N_IN)),
            _const_spec((1, ATT_WIDTH)),
            _const_spec((1, ATT_KV_WIDTH)),
            pl.BlockSpec((tm, V7X_LANES), tab),
            pl.BlockSpec((tm, V7X_LANES), tab),
            pl.BlockSpec((tm, V7X_LANES), tab),
            _const_spec((ATT_WIDTH, ATT_WIDTH)),
            _const_spec((ATT_KV_WIDTH, ATT_KV_WIDTH)),
        ],
        out_specs=out_specs,
        out_shape=out_shape,
        compiler_params=_cparams(2, 56),
        name="in_projection",
    )(x, sh, a, w_in, qg, kg, c, s1, s2, gmq, gmk)


def _attn_kernel(q_ref, kt_ref, v_ref, o_ref, m_ref, acc_ref, *, tk, nkb):
    tq = q_ref.shape[1]
    q = q_ref[0].astype(F32)
    qs = jnp.concatenate([q[:, HEAD_DIM * g:HEAD_DIM * (g + 1)] for g in range(ATT_GROUP)], axis=0).astype(BF16)
    m_ref[...] = jnp.full(m_ref.shape, -jnp.inf, F32)
    acc_ref[...] = jnp.zeros(acc_ref.shape, F32)
    ncol = tk // V7X_LANES

    def body(kb, carry):
        off = pl.multiple_of(kb * tk, tk)
        s = _dot(qs, kt_ref[0, 0, :, pl.ds(off, tk)])
        cols = [s[:, j * V7X_LANES:(j + 1) * V7X_LANES] for j in range(ncol)]
        mx = cols[0]
        for cj in cols[1:]:
            mx = jnp.maximum(mx, cj)
        m_prev = m_ref[...]
        m_new = jnp.maximum(m_prev, jnp.max(mx, axis=1, keepdims=True))
        alpha = jnp.exp(m_prev - m_new)
        p = jnp.concatenate([jnp.exp(cj - m_new).astype(BF16) for cj in cols], axis=1)
        acc_ref[...] = alpha * acc_ref[...] + _dot(p, v_ref[0, 0, pl.ds(off, tk), :])
        m_ref[...] = m_new
        return carry

    lax.fori_loop(0, nkb, body, 0)
    acc = acc_ref[...]
    out = acc / acc[:, HEAD_DIM:HEAD_DIM + 1]
    o_ref[0] = jnp.concatenate([out[g * tq:(g + 1) * tq, :HEAD_DIM] for g in range(ATT_GROUP)], axis=1).astype(BF16)


def _attention(q, kt, v_ext):
    b, t, _ = q.shape
    tkeys = kt.shape[-1]
    tq = min(256, t)
    tk = next(c for c in (640, 512, 256, 128) if tkeys % c == 0)
    kern = functools.partial(_attn_kernel, tk=tk, nkb=tkeys // tk)
    gw = ATT_GROUP * HEAD_DIM
    return pl.pallas_call(
        kern,
        grid=(b, ATT_KV_HEADS, t // tq),
        in_specs=[
            pl.BlockSpec((1, tq, gw), lambda bi, h, i: (bi, i, h)),
            pl.BlockSpec((1, 1, HEAD_DIM, tkeys), lambda bi, h, i: (bi, h, 0, 0)),
            pl.BlockSpec((1, 1, tkeys, V7X_LANES), lambda bi, h, i: (bi, h, 0, 0)),
        ],
        out_specs=pl.BlockSpec((1, tq, gw), lambda bi, h, i: (bi, i, h)),
        out_shape=jax.ShapeDtypeStruct((b, t, ATT_WIDTH), BF16),
        scratch_shapes=[
            pltpu.VMEM((ATT_GROUP * tq, V7X_LANES), F32),
            pltpu.VMEM((ATT_GROUP * tq, V7X_LANES), F32),
        ],
        compiler_params=_cparams(3, 48),
        name="gqa_attention",
    )(q, kt, v_ext)


def _hgrn_dir(q_ref, f_ref, v_ref, lb, st_ref, d, o_ref, kin_s, cum_s, v_s, reverse):
    tt, c, w = HG_TILE, HG_SUB, HG_WIDTH
    nsub = tt // c
    q = q_ref[0].astype(F32)
    fpre = f_ref[0]
    v = v_ref[0].astype(F32)
    qs = q * _sigmoid(q)
    lf = jnp.log(lb + (1.0 - lb) * _sigmoid(fpre))
    kin = (1.0 - lb) * _sigmoid(-fpre)

    r = lax.broadcasted_iota(jnp.int32, (tt, tt), 0)
    cc = lax.broadcasted_iota(jnp.int32, (tt, tt), 1)
    same = (r >> 4) == (cc >> 4)
    tri = jnp.where(same & ((cc >= r) if reverse else (cc <= r)), 1.0, 0.0).astype(BF16)
    blk = jnp.where(same, 1.0, 0.0).astype(BF16)
    cum = _dot_sel(tri, lf)
    tot = _dot_sel(blk, lf)

    hr = lax.broadcasted_iota(jnp.int32, (w, w), 0)
    hc = lax.broadcasted_iota(jnp.int32, (w, w), 1)
    head_blk = (hr >> 6) == (hc >> 6)
    gsum = jnp.where(head_blk, 1.0, 0.0).astype(BF16)
    bdmask = jnp.where(head_blk, 1.0, 0.0).astype(F32)

    zpad = jnp.zeros((c, w), F32)
    for buf, val in ((kin_s, kin), (cum_s, cum), (v_s, v)):
        buf[0:c, :] = zpad
        buf[c:c + tt, :] = val
        buf[c + tt:c + tt + c, :] = zpad
    pos = lax.broadcasted_iota(jnp.int32, (tt, w), 0) & (c - 1)
    o = (_dot((qs * kin).astype(BF16), gsum)) * v
    for j in range(1, c):
        start = c + j if reverse else c - j
        valid = (pos <= c - 1 - j) if reverse else (pos >= j)
        dec = jnp.exp(jnp.where(valid, cum - cum_s[start:start + tt, :], 0.0))
        wgt = jnp.where(valid, qs * kin_s[start:start + tt, :] * dec, 0.0)
        o = o + _dot(wgt.astype(BF16), gsum) * v_s[start:start + tt, :]

    qt = (qs * jnp.exp(cum)).astype(BF16)
    kt = (kin * jnp.exp(tot - cum)).astype(BF16)
    vb = v.astype(BF16)
    st = st_ref[d]
    pieces = [None] * nsub
    order = range(nsub - 1, -1, -1) if reverse else range(nsub)
    for n in order:
        rows = slice(n * c, (n + 1) * c)
        pieces[n] = _dot_nt(qt[rows], st.astype(BF16))
        kv = _dot_tn(vb[rows], kt[rows])
        st = jnp.exp(tot[n * c:n * c + 1, :]) * st + kv * bdmask
    st_ref[d] = st
    o_ref[0] = (o + jnp.concatenate(pieces, axis=0)).astype(BF16)


def _hgrn_kernel(qf_ref, ff_ref, vf_ref, qb_ref, fb_ref, vb_ref, lb_ref, s0_ref,
                 of_ref, ob_ref, sfin_ref, st_ref, kin_s, cum_s, v_s):
    @pl.when(pl.program_id(1) == 0)
    def _():
        st_ref[...] = s0_ref[0]

    _hgrn_dir(qf_ref, ff_ref, vf_ref, lb_ref[0:1, :], st_ref, 0, of_ref, kin_s, cum_s, v_s, False)
    _hgrn_dir(qb_ref, fb_ref, vb_ref, lb_ref[1:2, :], st_ref, 1, ob_ref, kin_s, cum_s, v_s, True)
    sfin_ref[0] = st_ref[...]


def _hgrn(hq, hf, hv, lb2, s0):
    b, t, w = hq.shape
    tt = HG_TILE
    nt = t // tt
    fwd = lambda bi, i: (bi, i, 0)
    bwd = lambda bi, i: (bi, nt - 1 - i, 0)
    bwd_f = lambda bi, i: (bi, nt - 1 - i, 1)
    st_spec = pl.BlockSpec((1, 2, w, w), lambda bi, i: (bi, 0, 0, 0))
    pad_rows = tt + 2 * HG_SUB
    return pl.pallas_call(
        _hgrn_kernel,
        grid=(b, nt),
        in_specs=[
            pl.BlockSpec((1, tt, w), fwd), pl.BlockSpec((1, tt, w), fwd), pl.BlockSpec((1, tt, w), fwd),
            pl.BlockSpec((1, tt, w), bwd), pl.BlockSpec((1, tt, w), bwd_f), pl.BlockSpec((1, tt, w), bwd),
            pl.BlockSpec((2, w), lambda bi, i: (0, 0)),
            st_spec,
        ],
        out_specs=[pl.BlockSpec((1, tt, w), fwd), pl.BlockSpec((1, tt, w), bwd), st_spec],
        out_shape=[
            jax.ShapeDtypeStruct((b, t, w), BF16),
            jax.ShapeDtypeStruct((b, t, w), BF16),
            jax.ShapeDtypeStruct((b, 2, w, w), F32),
        ],
        scratch_shapes=[
            pltpu.VMEM((2, w, w), F32),
            pltpu.VMEM((pad_rows, w), F32),
            pltpu.VMEM((pad_rows, w), F32),
            pltpu.VMEM((pad_rows, w), F32),
        ],
        compiler_params=_cparams(2, 32),
        name="hgrn2_scan",
    )(hq, hf, hv, hq, hf, hv, lb2, s0)


def _cpow(a_re, a_im, tau):
    mag = jnp.exp(tau * a_re)
    ang = tau * a_im
    return mag * jnp.cos(ang), mag * jnp.sin(ang)


def _s5_prep_kernel(ar_row, ai_row, ar_col, ai_col, btr, bti, cr, ci, ctr, cti,
                    w_ref, wst_ref, wout_ref, laml_ref, kall_ref):
    lc, p, c = S5_CHUNK, SSM_STATE, SSM_GROUP
    tau_l = lax.broadcasted_iota(jnp.int32, (p, lc), 1).astype(F32)
    tau_s = lax.broadcasted_iota(jnp.int32, (lc, p), 0).astype(F32)

    def cb(d):
        re, im = [], []
        for c1 in range(c):
            b_r, b_i = btr[0, d, c1:c1 + 1, :], bti[0, d, c1:c1 + 1, :]
            re.append(b_r * cr[0, d] - b_i * ci[0, d])
            im.append(b_r * ci[0, d] + b_i * cr[0, d])
        return jnp.concatenate(re, axis=0), jnp.concatenate(im, axis=0)

    cbf_r, cbf_i = cb(0)
    pf_r, pf_i = _cpow(ar_col[0, 0], ai_col[0, 0], tau_l)
    kf = _dot3(cbf_r, pf_r) - _dot3(cbf_i, pf_i)
    cbb_r, cbb_i = cb(1)
    pb_r, pb_i = _cpow(ar_col[0, 1], ai_col[0, 1], lc - tau_l)
    kb = _dot3(cbb_r, pb_r) - _dot3(cbb_i, pb_i)
    lane = lax.broadcasted_iota(jnp.int32, (c * c, lc), 1)
    kf = kf + jnp.where(lane == 0, jnp.sum(cbb_r, axis=1, keepdims=True), 0.0)
    kall_ref[...] = jnp.concatenate([kf, kb], axis=1)

    def toeplitz_rows(c1, carry):
        for c2 in range(c):
            row = kall_ref[pl.ds(c1 * c + c2, 1), :]
            blk = pltpu.roll(jnp.broadcast_to(row, (lc, 2 * lc)), 0, 1, stride=1, stride_axis=0)
            w_ref[0, pl.ds(pl.multiple_of(c1 * lc, lc), lc), c2 * lc:(c2 + 1) * lc] = blk[:, :lc].astype(BF16)
        return carry

    lax.fori_loop(0, c, toeplitz_rows, 0)

    sf_r, sf_i = _cpow(ar_row[0, 0], ai_row[0, 0], (lc - 1) - tau_s)
    sb_r, sb_i = _cpow(ar_row[0, 1], ai_row[0, 1], tau_s)
    for c1 in range(c):
        re, im = [], []
        for d, (p_r, p_i) in enumerate(((sf_r, sf_i), (sb_r, sb_i))):
            b_r, b_i = btr[0, d, c1:c1 + 1, :], bti[0, d, c1:c1 + 1, :]
            re.append(p_r * b_r - p_i * b_i)
            im.append(p_r * b_i + p_i * b_r)
        wst_ref[0, c1 * lc:(c1 + 1) * lc, :] = jnp.concatenate(re + im, axis=1).astype(BF16)

    of_r, of_i = _cpow(ar_col[0, 0], ai_col[0, 0], tau_l + 1.0)
    ob_r, ob_i = _cpow(ar_col[0, 1], ai_col[0, 1], lc - tau_l)
    for c2 in range(c):
        re, im = [], []
        for d, (p_r, p_i) in enumerate(((of_r, of_i), (ob_r, ob_i))):
            c_r, c_i = ctr[0, d, :, c2:c2 + 1], cti[0, d, :, c2:c2 + 1]
            re.append(c_r * p_r - c_i * p_i)
            im.append(-(c_r * p_i + c_i * p_r))
        wout_ref[0, :, c2 * lc:(c2 + 1) * lc] = jnp.concatenate(re + im, axis=0).astype(BF16)

    lf_r, lf_i = _cpow(ar_row[0, 0], ai_row[0, 0], float(lc))
    lb_r, lb_i = _cpow(ar_row[0, 1], ai_row[0, 1], float(lc))
    laml_ref[0, 0:1, :] = jnp.concatenate([lf_r, lb_r], axis=1)
    laml_ref[0, 1:2, :] = jnp.concatenate([lf_i, lb_i], axis=1)


def _s5_prep(a_re, a_im, bbar_re, bbar_im, c_re, c_im):
    g, _, p = a_re.shape
    c, lc = SSM_GROUP, S5_CHUNK
    n = c * lc
    row = lambda x: x.reshape(g, 2, 1, p)
    col = lambda x: x.reshape(g, 2, p, 1)
    tr = lambda x: jnp.swapaxes(x, -1, -2)
    args = [row(a_re), row(a_im), col(a_re), col(a_im), tr(bbar_re), tr(bbar_im), c_re, c_im, tr(c_re), tr(c_im)]
    spec4 = lambda shp: pl.BlockSpec((1,) + shp, lambda gi: (gi, 0, 0, 0))
    spec3 = lambda shp: pl.BlockSpec((1,) + shp, lambda gi: (gi, 0, 0))
    return pl.pallas_call(
        _s5_prep_kernel,
        grid=(g,),
        in_specs=[spec4(a.shape[1:]) for a in args],
        out_specs=[spec3((n, n)), spec3((n, 4 * p)), spec3((4 * p, n)), spec3((2, 2 * p))],
        out_shape=[
            jax.ShapeDtypeStruct((g, n, n), BF16),
            jax.ShapeDtypeStruct((g, n, 4 * p), BF16),
            jax.ShapeDtypeStruct((g, 4 * p, n), BF16),
            jax.ShapeDtypeStruct((g, 2, 2 * p), F32),
        ],
        scratch_shapes=[pltpu.VMEM((c * c, 2 * lc), F32)],
        compiler_params=_cparams(1, 48),
        name="s5_weights",
    )(*args)


def _s5_kernel(u_ref, w_ref, wst_ref, wout_ref, laml_ref, dsk_ref, y_ref, xloc_s, xin_s, *, nctx, nck):
    c, lc, p, p2 = SSM_GROUP, S5_CHUNK, SSM_STATE, 2 * SSM_STATE
    ub = jnp.concatenate([u_ref[0, c1] for c1 in range(c)], axis=1).astype(BF16)
    xloc_s[...] = _dot(ub, wst_ref[0])

    nlat = nck - nctx
    order_f = list(range(nlat, nck)) + list(range(nlat))
    order_b = list(range(nck - 1, nlat - 1, -1)) + list(range(nlat - 1, -1, -1))
    m_r, m_i = laml_ref[0, 0:1, :], laml_ref[0, 1:2, :]
    is_fwd = lax.broadcasted_iota(jnp.int32, (1, p2), 1) < p
    x_r = jnp.zeros((1, p2), F32)
    x_i = jnp.zeros((1, p2), F32)
    for kf, kb in zip(order_f, order_b):
        xin_s[kf:kf + 1, 0:p] = x_r[:, 0:p]
        xin_s[kb:kb + 1, p:p2] = x_r[:, p:p2]
        xin_s[kf:kf + 1, p2:p2 + p] = x_i[:, 0:p]
        xin_s[kb:kb + 1, p2 + p:2 * p2] = x_i[:, p:p2]
        loc_r = jnp.where(is_fwd, xloc_s[kf:kf + 1, 0:p2], xloc_s[kb:kb + 1, 0:p2])
        loc_i = jnp.where(is_fwd, xloc_s[kf:kf + 1, p2:2 * p2], xloc_s[kb:kb + 1, p2:2 * p2])
        x_r, x_i = m_r * x_r - m_i * x_i + loc_r, m_r * x_i + m_i * x_r + loc_i

    y = _dot(ub, w_ref[0]) + _dot(xin_s[...].astype(BF16), wout_ref[0])
    for c2 in range(c):
        y_ref[0, c2] = y[:, c2 * lc:(c2 + 1) * lc] + dsk_ref[0, c2:c2 + 1, :] * u_ref[0, c2]


def _s5(u_t, w, wst, wout, laml, dsk, nctx):
    b, wd, nck, lc = u_t.shape
    g, c = SSM_GROUPS, SSM_GROUP
    n = c * lc
    kern = functools.partial(_s5_kernel, nctx=nctx, nck=nck)
    gspec = lambda shp: pl.BlockSpec((1,) + shp, lambda gi, bi: (gi,) + (0,) * len(shp))
    io_spec = pl.BlockSpec((1, c, nck, lc), lambda gi, bi: (bi, gi, 0, 0))
    return pl.pallas_call(
        kern,
        grid=(g, b),
        in_specs=[io_spec, gspec((n, n)), gspec((n, 4 * SSM_STATE)), gspec((4 * SSM_STATE, n)),
                  gspec((2, 2 * SSM_STATE)), gspec((c, lc))],
        out_specs=io_spec,
        out_shape=jax.ShapeDtypeStruct(u_t.shape, F32),
        scratch_shapes=[
            pltpu.VMEM((nck, 4 * SSM_STATE), F32),
            pltpu.VMEM((nck, 4 * SSM_STATE), F32),
        ],
        compiler_params=_cparams(2, 48),
        name="s5_scan",
    )(u_t, w, wst, wout, laml, dsk)


def _merge_kernel(x_ref, att_ref, of_ref, ob_ref, hg_ref, yt_ref, gate_ref, g1_ref,
                  wa_ref, wr_ref, ws_ref, wo_ref, wglu_ref, bglu_ref, hn_ref, gm_ref, o_ref):
    r = of_ref[0].astype(F32) + ob_ref[0].astype(F32)
    ms = _dot((r * r).astype(BF16), gm_ref[...])
    g = hg_ref[0].astype(F32)
    yrec = (r * lax.rsqrt(ms + RMS_EPS) * hn_ref[...]) * (g * _sigmoid(g))

    ys = yt_ref[0].T
    z = 0.5 * ys * (1.0 + jnp.tanh(math.sqrt(2.0 / math.pi) * (ys + 0.044715 * (ys * ys * ys))))
    yssm = z * _sigmoid(_dot(z.astype(BF16), wglu_ref[...]) + bglu_ref[...])

    d = D_MODEL
    m = gate_ref[0, :, 0:d].astype(F32) * _dot(att_ref[0], wa_ref[...])
    m = m + gate_ref[0, :, d:2 * d].astype(F32) * _dot(yrec.astype(BF16), wr_ref[...])
    m = m + gate_ref[0, :, 2 * d:3 * d].astype(F32) * _dot(yssm.astype(BF16), ws_ref[...])
    y = _dot(m.astype(BF16), wo_ref[...])
    o_ref[0] = x_ref[0] + g1_ref[0] * y


def _merge(x, att, o_f, o_b, hg, y_t, t_off, gates, g1, wa, wr, ws, wo, wglu, bglu, hn, gm):
    b, t, d = x.shape
    tm = min(512, t)
    off = t_off // tm
    row = lambda bi, i: (bi, i, 0)
    vec = lambda bi, i: (bi, 0, 0)
    return pl.pallas_call(
        _merge_kernel,
        grid=(b, t // tm),
        in_specs=[
            pl.BlockSpec((1, tm, d), row),
            pl.BlockSpec((1, tm, ATT_WIDTH), row),
            pl.BlockSpec((1, tm, HG_WIDTH), row),
            pl.BlockSpec((1, tm, HG_WIDTH), row),
            pl.BlockSpec((1, tm, HG_WIDTH), row),
            pl.BlockSpec((1, SSM_WIDTH, tm), lambda bi, i: (bi, 0, i + off)),
            pl.BlockSpec((1, tm, 3 * d), row),
            pl.BlockSpec((1, 1, d), vec),
            _const_spec(wa.shape), _const_spec(wr.shape), _const_spec(ws.shape), _const_spec(wo.shape),
            _const_spec(wglu.shape), _const_spec(bglu.shape), _const_spec(hn.shape), _const_spec(gm.shape),
        ],
        out_specs=pl.BlockSpec((1, tm, d), row),
        out_shape=jax.ShapeDtypeStruct((b, t, d), F32),
        compiler_params=_cparams(2, 48),
        name="merge_branches",
    )(x, att, o_f, o_b, hg, y_t, gates, g1, wa, wr, ws, wo, wglu, bglu, hn, gm)


def _ffn_kernel(x_ref, sh_ref, a_ref, g_ref, wup_ref, wdn_ref, o_ref, *, nj):
    x = x_ref[0]
    ms = jnp.mean(x * x, axis=-1, keepdims=True)
    hb = ((x * lax.rsqrt(ms + RMS_EPS)) * a_ref[0] + sh_ref[0]).astype(BF16)
    f = FFN_HIDDEN
    fc = f // nj
    acc = None
    for j in range(nj):
        a = _dot(hb, wup_ref[:, j * fc:(j + 1) * fc])
        bgate = _dot(hb, wup_ref[:, f + j * fc:f + (j + 1) * fc])
        act = ((a * _sigmoid(a)) * bgate).astype(BF16)
        part = _dot(act, wdn_ref[j * fc:(j + 1) * fc, :])
        acc = part if acc is None else acc + part
    o_ref[0] = x + g_ref[0] * acc


def _ffn(x, sh, a, g, wup, wdn):
    b, t, d = x.shape
    tm = min(512, t)
    row = lambda bi, i: (bi, i, 0)
    vec = lambda bi, i: (bi, 0, 0)
    return pl.pallas_call(
        functools.partial(_ffn_kernel, nj=2),
        grid=(b, t // tm),
        in_specs=[
            pl.BlockSpec((1, tm, d), row),
            pl.BlockSpec((1, 1, d), vec), pl.BlockSpec((1, 1, d), vec), pl.BlockSpec((1, 1, d), vec),
            _const_spec(wup.shape), _const_spec(wdn.shape),
        ],
        out_specs=pl.BlockSpec((1, tm, d), row),
        out_shape=jax.ShapeDtypeStruct((b, t, d), F32),
        compiler_params=_cparams(2, 56),
        name="swiglu_ffn",
    )(x, sh, a, g, wup, wdn)


def _rope_tables(t, identity=False):
    if identity:
        z = jnp.zeros((t, V7X_LANES), F32)
        return jnp.ones((t, V7X_LANES), F32), z, z
    pos = jnp.arange(t)
    row = (pos // GRID_W).astype(F32)
    col = (pos % GRID_W).astype(F32)
    axis_dim = HEAD_DIM // 2
    inv = ROPE_THETA ** (-jnp.arange(0, axis_dim, 2, dtype=F32) / axis_dim)
    ang_r, ang_c = row[:, None] * inv, col[:, None] * inv
    cr, sr, cc, sc = jnp.cos(ang_r), jnp.sin(ang_r), jnp.cos(ang_c), jnp.sin(ang_c)
    z = jnp.zeros_like(cr)
    rep = V7X_LANES // HEAD_DIM
    c = jnp.tile(jnp.concatenate([cr, cr, cc, cc], axis=1), (1, rep))
    s1 = jnp.tile(jnp.concatenate([-sr, z, -sc, z], axis=1), (1, rep))
    s2 = jnp.tile(jnp.concatenate([z, sr, z, sc], axis=1), (1, rep))
    return c, s1, s2


def _group_mean_matrix(width, group):
    i = jnp.arange(width) // group
    return jnp.where(i[:, None] == i[None, :], 1.0 / group, 0.0).astype(BF16)


def _kv_layout(k, v):
    b, tk, _ = k.shape
    kt = k.reshape(b, tk, ATT_KV_HEADS, HEAD_DIM).transpose(0, 2, 3, 1)
    vh = v.reshape(b, tk, ATT_KV_HEADS, HEAD_DIM).transpose(0, 2, 1, 3)
    ones = jnp.ones((b, ATT_KV_HEADS, tk, 1), BF16)
    zeros = jnp.zeros((b, ATT_KV_HEADS, tk, V7X_LANES - HEAD_DIM - 1), BF16)
    return kt, jnp.concatenate([vh, ones, zeros], axis=-1)


def kernel(x, c, ctx, c_ctx, w_mod, b_mod, norm1_g, norm2_g, w_in, q_norm_g, k_norm_g, hgrn_lb, hgrn_norm_g,
           ssm_lam_re, ssm_lam_im, ssm_log_dt, ssm_b_re, ssm_b_im, ssm_c_re, ssm_c_im, ssm_d, w_glu, b_glu,
           w_br_attn, w_br_hgrn, w_br_ssm, w_out, w_ffn_up, w_ffn_down):
    bsz, t, d = x.shape
    t_ctx = ctx.shape[1]
    depth = w_mod.shape[0]
    assert t % 512 == 0 and t_ctx % S5_CHUNK == 0 and d == D_MODEL

    lb_soft = jax.nn.softmax(hgrn_lb.astype(F32), axis=0)
    lower_bounds = jnp.cumsum(lb_soft, axis=0) - lb_soft[0]
    rope_lat = _rope_tables(t)
    rope_ctx = _rope_tables(t_ctx, identity=True)
    gmq = _group_mean_matrix(ATT_WIDTH, HEAD_DIM)
    gmk = _group_mean_matrix(ATT_KV_WIDTH, HEAD_DIM)
    gmh = _group_mean_matrix(HG_WIDTH, HG_DK)
    cond8 = jnp.zeros((8, d), F32).at[:bsz].set(c).at[bsz].set(c_ctx)
    mods = _modulation(cond8, w_mod, b_mod)

    x_lat, x_ctx = x, ctx
    for l in range(depth):
        with_ctx = l < depth - 1
        ml = mods[l, :bsz].reshape(bsz, ADALN_CHUNKS, 1, d)
        mc = jnp.broadcast_to(mods[l, bsz].reshape(1, ADALN_CHUNKS, 1, d), (bsz, ADALN_CHUNKS, 1, d))
        sh1, sc1, g1, sh2, sc2, g2 = [ml[:, i] for i in range(ADALN_CHUNKS)]
        csh1, csc1, cg1, csh2, csc2, cg2 = [mc[:, i] for i in range(ADALN_CHUNKS)]
        n1, n2 = norm1_g[l].reshape(1, 1, d), norm2_g[l].reshape(1, 1, d)

        w_in_b = w_in[l].astype(BF16)
        qg = (jnp.tile(q_norm_g[l], ATT_HEADS) * HEAD_DIM ** -0.5).reshape(1, ATT_WIDTH)
        kg = jnp.tile(k_norm_g[l], ATT_KV_HEADS).reshape(1, ATT_KV_WIDTH)
        lat = _inproj(x_lat, sh1, n1 * (1.0 + sc1), w_in_b, qg, kg, rope_lat, gmq, gmk)
        cx = _inproj(x_ctx, csh1, n1 * (1.0 + csc1), w_in_b, qg, kg, rope_ctx, gmq, gmk)
        q_l, k_l, v_l, hq_l, hf_l, hi_l, hg_l, ut_l, gate_l = lat
        q_c, k_c, v_c, hq_c, hf_c, hi_c, hg_c, ut_c, gate_c = cx

        kt_all, v_all = _kv_layout(jnp.concatenate([k_c, k_l], axis=1), jnp.concatenate([v_c, v_l], axis=1))
        a_lat = _attention(q_l, kt_all, v_all)
        if with_ctx:
            a_ctx = _attention(q_c, *_kv_layout(k_c, v_c))

        lb2 = lower_bounds[l]
        s_zero = jnp.zeros((bsz, 2, HG_WIDTH, HG_WIDTH), F32)
        of_c, ob_c, s_ctx = _hgrn(hq_c, hf_c, hi_c, lb2, s_zero)
        of_l, ob_l, _ = _hgrn(hq_l, hf_l, hi_l, lb2, s_ctx)

        dt = jnp.exp(ssm_log_dt[l].astype(F32))[..., None]
        lam = lax.complex(ssm_lam_re[l].astype(F32), ssm_lam_im[l].astype(F32))
        lam_bar = jnp.exp(lam * dt)
        b_bar = ((lam_bar - 1.0) / lam)[..., None] * lax.complex(ssm_b_re[l].astype(F32), ssm_b_im[l].astype(F32))
        gd = lambda a: jnp.swapaxes(a, 0, 1)
        w_s5, wst, wout, laml = _s5_prep(
            gd(ssm_lam_re[l] * dt), gd(ssm_lam_im[l] * dt), gd(jnp.real(b_bar)), gd(jnp.imag(b_bar)),
            gd(ssm_c_re[l].astype(F32)), gd(ssm_c_im[l].astype(F32)))
        dsk = jnp.broadcast_to(ssm_d[l].astype(F32).reshape(SSM_GROUPS, SSM_GROUP, 1), (SSM_GROUPS, SSM_GROUP, S5_CHUNK))
        u_t = jnp.concatenate([ut_l, ut_c], axis=2).reshape(bsz, SSM_WIDTH, (t_ctx + t) // S5_CHUNK, S5_CHUNK)
        y_t = _s5(u_t, w_s5, wst, wout, laml, dsk, t_ctx // S5_CHUNK).reshape(bsz, SSM_WIDTH, t_ctx + t)

        hn = jnp.tile(hgrn_norm_g[l], HG_HEADS).reshape(1, HG_WIDTH)
        mw = (w_br_attn[l].astype(BF16), w_br_hgrn[l].astype(BF16), w_br_ssm[l].astype(BF16), w_out[l].astype(BF16),
              w_glu[l].astype(BF16), b_glu[l].reshape(1, SSM_WIDTH), hn, gmh)
        wup, wdn = w_ffn_up[l].astype(BF16), w_ffn_down[l].astype(BF16)
        x_lat = _merge(x_lat, a_lat, of_l, ob_l, hg_l, y_t, 0, gate_l, g1, *mw)
        x_lat = _ffn(x_lat, sh2, n2 * (1.0 + sc2), g2, wup, wdn)
        if with_ctx:
            x_ctx = _merge(x_ctx, a_ctx, of_c, ob_c, hg_c, y_t, t, gate_c, cg1, *mw)
            x_ctx = _ffn(x_ctx, csh2, n2 * (1.0 + csc2), cg2, wup, wdn)
    return x_lat
```

```python
import functools
import math

import jax
import jax.numpy as jnp
from jax import lax
from jax.experimental import pallas as pl
from jax.experimental.pallas import tpu as pltpu

F32 = jnp.float32
BF16 = jnp.bfloat16

D_MODEL = 1024
GRID_W = 64
RMS_EPS = 1e-6
ADALN_CHUNKS = 6
ATT_HEADS = 8
ATT_KV_HEADS = 2
ATT_GROUP = ATT_HEADS // ATT_KV_HEADS
HEAD_DIM = 64
ATT_WIDTH = ATT_HEADS * HEAD_DIM
ATT_KV_WIDTH = ATT_KV_HEADS * HEAD_DIM
ROPE_THETA = 10000.0
HG_HEADS = 4
HG_DK = 64
HG_WIDTH = HG_HEADS * HG_DK
SSM_WIDTH = 256
SSM_GROUP = 16
SSM_GROUPS = SSM_WIDTH // SSM_GROUP
SSM_STATE = 64
FFN_HIDDEN = 2816
N_IN = 5376
O_Q, O_K, O_V, O_HQ, O_HF, O_HI, O_HG, O_U, O_GATE = 0, 512, 640, 768, 1024, 1536, 1792, 2048, 2304

V7X_LANES = 128
V7X_VMEM_BYTES = 64 * 1024 * 1024
MIB = 1024 * 1024

HG_TILE = 128
HG_SUB = 16
S5_CHUNK = 128


def _dot(a, b):
    return jnp.dot(a, b, preferred_element_type=F32)


def _dot_nt(a, b):
    return lax.dot_general(a, b, (((1,), (1,)), ((), ())), preferred_element_type=F32)


def _dot_tn(a, b):
    return lax.dot_general(a, b, (((0,), (0,)), ((), ())), preferred_element_type=F32)


def _split(x):
    hi = x.astype(BF16)
    lo = (x - hi.astype(F32)).astype(BF16)
    return hi, lo


def _dot_sel(sel_bf16, x):
    hi, lo = _split(x)
    return _dot(sel_bf16, hi) + _dot(sel_bf16, lo)


def _dot3(a, b):
    ah, al = _split(a)
    bh, bl = _split(b)
    return _dot(ah, bh) + (_dot(ah, bl) + _dot(al, bh))


def _sigmoid(x):
    return jax.nn.sigmoid(x)


def _cparams(n_axes, vmem_mib):
    return pltpu.CompilerParams(
        dimension_semantics=("arbitrary",) * n_axes,
        vmem_limit_bytes=min(vmem_mib * MIB, V7X_VMEM_BYTES - 4 * MIB),
    )


def _const_spec(shape):
    nd = len(shape)
    return pl.BlockSpec(shape, lambda *_: (0,) * nd, pipeline_mode=pl.Buffered(1))


def _mod_kernel(c_ref, w_ref, b_ref, o_ref):
    c = c_ref[...]
    s = c * _sigmoid(c)
    o_ref[0] = _dot3(s, w_ref[0]) + b_ref[0]


def _modulation(cond8, w_mod, b_mod):
    n_layers, d, n = w_mod.shape
    nb = 1536
    return pl.pallas_call(
        _mod_kernel,
        grid=(n_layers, n // nb),
        in_specs=[
            pl.BlockSpec((8, d), lambda l, j: (0, 0)),
            pl.BlockSpec((1, d, nb), lambda l, j: (l, 0, j)),
            pl.BlockSpec((1, 1, nb), lambda l, j: (l, 0, j)),
        ],
        out_specs=pl.BlockSpec((1, 8, nb), lambda l, j: (l, 0, j)),
        out_shape=jax.ShapeDtypeStruct((n_layers, 8, n), F32),
        compiler_params=_cparams(2, 40),
        name="adaln_modulation",
    )(cond8, w_mod, b_mod.reshape(n_layers, 1, n))


def _rope128(x, c, s1, s2):
    return x * c + pltpu.roll(x, V7X_LANES - 16, 1) * s1 + pltpu.roll(x, 16, 1) * s2


def _inproj_kernel(x_ref, sh_ref, a_ref, w_ref, qg_ref, kg_ref, c_ref, s1_ref, s2_ref, gmq_ref, gmk_ref,
                   q_ref, k_ref, v_ref, hq_ref, hf_ref, hi_ref, hg_ref, ut_ref, gate_ref):
    x = x_ref[0]
    ms = jnp.mean(x * x, axis=-1, keepdims=True)
    h = (x * lax.rsqrt(ms + RMS_EPS)) * a_ref[0] + sh_ref[0]
    hb = h.astype(BF16)

    def proj(lo, hi):
        return _dot(hb, w_ref[:, lo:hi])

    c, s1, s2 = c_ref[...], s1_ref[...], s2_ref[...]

    zq = proj(O_Q, O_K)
    msq = _dot((zq * zq).astype(BF16), gmq_ref[...])
    qn = zq * lax.rsqrt(msq + RMS_EPS) * qg_ref[...]
    for j in range(ATT_WIDTH // V7X_LANES):
        sl = slice(j * V7X_LANES, (j + 1) * V7X_LANES)
        q_ref[0, :, sl] = _rope128(qn[:, sl], c, s1, s2).astype(BF16)

    zk = proj(O_K, O_V)
    msk = _dot((zk * zk).astype(BF16), gmk_ref[...])
    kn = zk * lax.rsqrt(msk + RMS_EPS) * kg_ref[...]
    k_ref[0] = _rope128(kn, c, s1, s2).astype(BF16)

    v_ref[0] = proj(O_V, O_HQ).astype(BF16)
    hq_ref[0] = proj(O_HQ, O_HF).astype(BF16)
    hf_ref[0] = proj(O_HF, O_HI)
    hi_ref[0] = proj(O_HI, O_HG).astype(BF16)
    hg_ref[0] = proj(O_HG, O_U).astype(BF16)
    ut_ref[0] = proj(O_U, O_GATE).T
    for j in range(3):
        lo = O_GATE + j * D_MODEL
        gate_ref[0, :, j * D_MODEL:(j + 1) * D_MODEL] = _sigmoid(proj(lo, lo + D_MODEL)).astype(BF16)


def _inproj(x, sh, a, w_in, qg, kg, rope, gmq, gmk):
    b, t, d = x.shape
    tm = min(512, t)
    c, s1, s2 = rope
    row = lambda bi, i: (bi, i, 0)
    vec = lambda bi, i: (bi, 0, 0)
    tab = lambda bi, i: (i, 0)
    outs = [
        (ATT_WIDTH, BF16), (ATT_KV_WIDTH, BF16), (ATT_KV_WIDTH, BF16),
        (HG_WIDTH, BF16), (2 * HG_WIDTH, F32), (HG_WIDTH, BF16), (HG_WIDTH, BF16),
    ]
    out_shape = [jax.ShapeDtypeStruct((b, t, w), dt) for w, dt in outs]
    out_specs = [pl.BlockSpec((1, tm, w), row) for w, _ in outs]
    out_shape.append(jax.ShapeDtypeStruct((b, SSM_WIDTH, t), F32))
    out_specs.append(pl.BlockSpec((1, SSM_WIDTH, tm), lambda bi, i: (bi, 0, i)))
    out_shape.append(jax.ShapeDtypeStruct((b, t, 3 * D_MODEL), BF16))
    out_specs.append(pl.BlockSpec((1, tm, 3 * D_MODEL), row))
    return pl.pallas_call(
        _inproj_kernel,
        grid=(b, t // tm),
        in_specs=[
            pl.BlockSpec((1, tm, d), row),
            pl.BlockSpec((1, 1, d), vec),
            pl.BlockSpec((1, 1, d), vec),
            _const_spec((d, N_IN)),
            _const_spec((1, ATT_WIDTH)),
            _const_spec((1, ATT_KV_WIDTH)),
            pl.BlockSpec((tm, V7X_LANES), tab),
            pl.BlockSpec((tm, V7X_LANES), tab),
            pl.BlockSpec((tm, V7X_LANES), tab),
            _const_spec((ATT_WIDTH, ATT_WIDTH)),
            _const_spec((ATT_KV_WIDTH, ATT_KV_WIDTH)),
        ],
        out_specs=out_specs,
        out_shape=out_shape,
        compiler_params=_cparams(2, 56),
        name="in_projection",
    )(x, sh, a, w_in, qg, kg, c, s1, s2, gmq, gmk)


def _attn_kernel(q_ref, kt_ref, v_ref, o_ref, qs_ref, m_ref, acc_ref, sa_ref, sb_ref, *, tk, nkb):
    tq = q_ref.shape[1]
    q = q_ref[0].astype(F32)
    qs_ref[...] = jnp.concatenate(
        [q[:, HEAD_DIM * g:HEAD_DIM * (g + 1)] for g in range(ATT_GROUP)], axis=0).astype(BF16)
    m_ref[...] = jnp.full(m_ref.shape, -jnp.inf, F32)
    acc_ref[...] = jnp.zeros(acc_ref.shape, F32)
    ncol = tk // V7X_LANES

    def scores(kb, s_ref):
        off = pl.multiple_of(kb * tk, tk)
        s_ref[...] = _dot(qs_ref[...], kt_ref[0, 0, :, pl.ds(off, tk)])

    def consume(kb, s_ref):
        off = pl.multiple_of(kb * tk, tk)
        mx = s_ref[:, 0:V7X_LANES]
        for j in range(1, ncol):
            mx = jnp.maximum(mx, s_ref[:, j * V7X_LANES:(j + 1) * V7X_LANES])
        m_prev = m_ref[...]
        m_new = jnp.maximum(m_prev, jnp.max(mx, axis=1, keepdims=True))
        alpha = jnp.exp2(m_prev - m_new)
        p = jnp.concatenate(
            [jnp.exp2(s_ref[:, j * V7X_LANES:(j + 1) * V7X_LANES] - m_new).astype(BF16) for j in range(ncol)], axis=1)
        acc_ref[...] = alpha * acc_ref[...] + _dot(p, v_ref[0, 0, pl.ds(off, tk), :])
        m_ref[...] = m_new

    scores(0, sa_ref)

    def body(i, carry):
        kb = 2 * i
        scores(kb + 1, sb_ref)
        consume(kb, sa_ref)
        scores(kb + 2, sa_ref)
        consume(kb + 1, sb_ref)
        return carry

    npair = (nkb - 1) // 2
    lax.fori_loop(0, npair, body, 0)
    if nkb % 2 == 1:
        consume(nkb - 1, sa_ref)
    else:
        scores(nkb - 1, sb_ref)
        consume(nkb - 2, sa_ref)
        consume(nkb - 1, sb_ref)

    acc = acc_ref[...]
    out = acc / acc[:, HEAD_DIM:HEAD_DIM + 1]
    o_ref[0] = jnp.concatenate([out[g * tq:(g + 1) * tq, :HEAD_DIM] for g in range(ATT_GROUP)], axis=1).astype(BF16)


ATTN_KEY_BLOCKS = (1280, 1024, 512, 256, 128)


def _attention(q, kt, v_ext):
    b, t, _ = q.shape
    tkeys = kt.shape[-1]
    tq = min(256, t)
    tk = next(c for c in ATTN_KEY_BLOCKS if tkeys % c == 0)
    kern = functools.partial(_attn_kernel, tk=tk, nkb=tkeys // tk)
    gw = ATT_GROUP * HEAD_DIM
    m = ATT_GROUP * tq
    return pl.pallas_call(
        kern,
        grid=(b, ATT_KV_HEADS, t // tq),
        in_specs=[
            pl.BlockSpec((1, tq, gw), lambda bi, h, i: (bi, i, h)),
            pl.BlockSpec((1, 1, HEAD_DIM, tkeys), lambda bi, h, i: (bi, h, 0, 0)),
            pl.BlockSpec((1, 1, tkeys, V7X_LANES), lambda bi, h, i: (bi, h, 0, 0)),
        ],
        out_specs=pl.BlockSpec((1, tq, gw), lambda bi, h, i: (bi, i, h)),
        out_shape=jax.ShapeDtypeStruct((b, t, ATT_WIDTH), BF16),
        scratch_shapes=[
            pltpu.VMEM((m, HEAD_DIM), BF16),
            pltpu.VMEM((m, V7X_LANES), F32),
            pltpu.VMEM((m, V7X_LANES), F32),
            pltpu.VMEM((m, tk), F32),
            pltpu.VMEM((m, tk), F32),
        ],
        compiler_params=_cparams(3, 48),
        name="gqa_attention",
    )(q, kt, v_ext)


def _hgrn_dir(q_ref, f_ref, v_ref, lb, st_ref, d, o_ref, kin_s, cum_s, v_s, reverse):
    tt, c, w = HG_TILE, HG_SUB, HG_WIDTH
    nsub = tt // c
    q = q_ref[0].astype(F32)
    fpre = f_ref[0]
    v = v_ref[0].astype(F32)
    qs = q * _sigmoid(q)
    lf = jnp.log(lb + (1.0 - lb) * _sigmoid(fpre))
    kin = (1.0 - lb) * _sigmoid(-fpre)

    r = lax.broadcasted_iota(jnp.int32, (tt, tt), 0)
    cc = lax.broadcasted_iota(jnp.int32, (tt, tt), 1)
    same = (r >> 4) == (cc >> 4)
    tri = jnp.where(same & ((cc >= r) if reverse else (cc <= r)), 1.0, 0.0).astype(BF16)
    blk = jnp.where(same, 1.0, 0.0).astype(BF16)
    cum = _dot_sel(tri, lf)
    tot = _dot_sel(blk, lf)

    hr = lax.broadcasted_iota(jnp.int32, (w, w), 0)
    hc = lax.broadcasted_iota(jnp.int32, (w, w), 1)
    head_blk = (hr >> 6) == (hc >> 6)
    gsum = jnp.where(head_blk, 1.0, 0.0).astype(BF16)
    bdmask = jnp.where(head_blk, 1.0, 0.0).astype(F32)

    zpad = jnp.zeros((c, w), F32)
    for buf, val in ((kin_s, kin), (cum_s, cum), (v_s, v)):
        buf[0:c, :] = zpad
        buf[c:c + tt, :] = val
        buf[c + tt:c + tt + c, :] = zpad
    pos = lax.broadcasted_iota(jnp.int32, (tt, w), 0) & (c - 1)
    o = (_dot((qs * kin).astype(BF16), gsum)) * v
    for j in range(1, c):
        start = c + j if reverse else c - j
        valid = (pos <= c - 1 - j) if reverse else (pos >= j)
        dec = jnp.exp(jnp.where(valid, cum - cum_s[start:start + tt, :], 0.0))
        wgt = jnp.where(valid, qs * kin_s[start:start + tt, :] * dec, 0.0)
        o = o + _dot(wgt.astype(BF16), gsum) * v_s[start:start + tt, :]

    qt = (qs * jnp.exp(cum)).astype(BF16)
    kt = (kin * jnp.exp(tot - cum)).astype(BF16)
    vb = v.astype(BF16)
    st = st_ref[d]
    pieces = [None] * nsub
    order = range(nsub - 1, -1, -1) if reverse else range(nsub)
    for n in order:
        rows = slice(n * c, (n + 1) * c)
        pieces[n] = _dot_nt(qt[rows], st.astype(BF16))
        kv = _dot_tn(vb[rows], kt[rows])
        st = jnp.exp(tot[n * c:n * c + 1, :]) * st + kv * bdmask
    st_ref[d] = st
    o_ref[0] = (o + jnp.concatenate(pieces, axis=0)).astype(BF16)


def _hgrn_kernel(qf_ref, ff_ref, vf_ref, qb_ref, fb_ref, vb_ref, lb_ref, s0_ref,
                 of_ref, ob_ref, sfin_ref, st_ref, kin_s, cum_s, v_s):
    @pl.when(pl.program_id(1) == 0)
    def _():
        st_ref[...] = s0_ref[0]

    _hgrn_dir(qf_ref, ff_ref, vf_ref, lb_ref[0:1, :], st_ref, 0, of_ref, kin_s, cum_s, v_s, False)
    _hgrn_dir(qb_ref, fb_ref, vb_ref, lb_ref[1:2, :], st_ref, 1, ob_ref, kin_s, cum_s, v_s, True)
    sfin_ref[0] = st_ref[...]


def _hgrn(hq, hf, hv, lb2, s0):
    b, t, w = hq.shape
    tt = HG_TILE
    nt = t // tt
    fwd = lambda bi, i: (bi, i, 0)
    bwd = lambda bi, i: (bi, nt - 1 - i, 0)
    bwd_f = lambda bi, i: (bi, nt - 1 - i, 1)
    st_spec = pl.BlockSpec((1, 2, w, w), lambda bi, i: (bi, 0, 0, 0))
    pad_rows = tt + 2 * HG_SUB
    return pl.pallas_call(
        _hgrn_kernel,
        grid=(b, nt),
        in_specs=[
            pl.BlockSpec((1, tt, w), fwd), pl.BlockSpec((1, tt, w), fwd), pl.BlockSpec((1, tt, w), fwd),
            pl.BlockSpec((1, tt, w), bwd), pl.BlockSpec((1, tt, w), bwd_f), pl.BlockSpec((1, tt, w), bwd),
            pl.BlockSpec((2, w), lambda bi, i: (0, 0)),
            st_spec,
        ],
        out_specs=[pl.BlockSpec((1, tt, w), fwd), pl.BlockSpec((1, tt, w), bwd), st_spec],
        out_shape=[
            jax.ShapeDtypeStruct((b, t, w), BF16),
            jax.ShapeDtypeStruct((b, t, w), BF16),
            jax.ShapeDtypeStruct((b, 2, w, w), F32),
        ],
        scratch_shapes=[
            pltpu.VMEM((2, w, w), F32),
            pltpu.VMEM((pad_rows, w), F32),
            pltpu.VMEM((pad_rows, w), F32),
            pltpu.VMEM((pad_rows, w), F32),
        ],
        compiler_params=_cparams(2, 32),
        name="hgrn2_scan",
    )(hq, hf, hv, hq, hf, hv, lb2, s0)


def _cpow(a_re, a_im, tau):
    mag = jnp.exp(tau * a_re)
    ang = tau * a_im
    return mag * jnp.cos(ang), mag * jnp.sin(ang)


def _s5_prep_kernel(ar_row, ai_row, ar_col, ai_col, lr_row, li_row, bt_r, bt_i, cr, ci, ctr, cti,
                    w_ref, wst_ref, wout_ref, laml_ref, kall_ref):
    lc, p, c = S5_CHUNK, SSM_STATE, SSM_GROUP
    tau_l = lax.broadcasted_iota(jnp.int32, (p, lc), 1).astype(F32)
    tau_s = lax.broadcasted_iota(jnp.int32, (lc, p), 0).astype(F32)

    btr, bti = [], []
    for d in range(2):
        e_r, e_i = _cpow(ar_row[0, d], ai_row[0, d], 1.0)
        l_r, l_i = lr_row[0, d], li_row[0, d]
        den = l_r * l_r + l_i * l_i
        f_r = ((e_r - 1.0) * l_r + e_i * l_i) / den
        f_i = (e_i * l_r - (e_r - 1.0) * l_i) / den
        btr.append(f_r * bt_r[0, d] - f_i * bt_i[0, d])
        bti.append(f_r * bt_i[0, d] + f_i * bt_r[0, d])

    def cb(d):
        re, im = [], []
        for c1 in range(c):
            b_r, b_i = btr[d][c1:c1 + 1, :], bti[d][c1:c1 + 1, :]
            re.append(b_r * cr[0, d] - b_i * ci[0, d])
            im.append(b_r * ci[0, d] + b_i * cr[0, d])
        return jnp.concatenate(re, axis=0), jnp.concatenate(im, axis=0)

    cbf_r, cbf_i = cb(0)
    pf_r, pf_i = _cpow(ar_col[0, 0], ai_col[0, 0], tau_l)
    kf = _dot3(cbf_r, pf_r) - _dot3(cbf_i, pf_i)
    cbb_r, cbb_i = cb(1)
    pb_r, pb_i = _cpow(ar_col[0, 1], ai_col[0, 1], lc - tau_l)
    kb = _dot3(cbb_r, pb_r) - _dot3(cbb_i, pb_i)
    lane = lax.broadcasted_iota(jnp.int32, (c * c, lc), 1)
    kf = kf + jnp.where(lane == 0, jnp.sum(cbb_r, axis=1, keepdims=True), 0.0)
    kall_ref[...] = jnp.concatenate([kf, kb], axis=1)

    def toeplitz_rows(c1, carry):
        for c2 in range(c):
            row = kall_ref[pl.ds(c1 * c + c2, 1), :]
            blk = pltpu.roll(jnp.broadcast_to(row, (lc, 2 * lc)), 0, 1, stride=1, stride_axis=0)
            w_ref[0, pl.ds(pl.multiple_of(c1 * lc, lc), lc), c2 * lc:(c2 + 1) * lc] = blk[:, :lc].astype(BF16)
        return carry

    lax.fori_loop(0, c, toeplitz_rows, 0)

    sf_r, sf_i = _cpow(ar_row[0, 0], ai_row[0, 0], (lc - 1) - tau_s)
    sb_r, sb_i = _cpow(ar_row[0, 1], ai_row[0, 1], tau_s)
    for c1 in range(c):
        re, im = [], []
        for d, (p_r, p_i) in enumerate(((sf_r, sf_i), (sb_r, sb_i))):
            b_r, b_i = btr[d][c1:c1 + 1, :], bti[d][c1:c1 + 1, :]
            re.append(p_r * b_r - p_i * b_i)
            im.append(p_r * b_i + p_i * b_r)
        wst_ref[0, c1 * lc:(c1 + 1) * lc, :] = jnp.concatenate(re + im, axis=1).astype(BF16)

    of_r, of_i = _cpow(ar_col[0, 0], ai_col[0, 0], tau_l + 1.0)
    ob_r, ob_i = _cpow(ar_col[0, 1], ai_col[0, 1], lc - tau_l)
    for c2 in range(c):
        re, im = [], []
        for d, (p_r, p_i) in enumerate(((of_r, of_i), (ob_r, ob_i))):
            c_r, c_i = ctr[0, d, :, c2:c2 + 1], cti[0, d, :, c2:c2 + 1]
            re.append(c_r * p_r - c_i * p_i)
            im.append(-(c_r * p_i + c_i * p_r))
        wout_ref[0, :, c2 * lc:(c2 + 1) * lc] = jnp.concatenate(re + im, axis=0).astype(BF16)

    lf_r, lf_i = _cpow(ar_row[0, 0], ai_row[0, 0], float(lc))
    lb_r, lb_i = _cpow(ar_row[0, 1], ai_row[0, 1], float(lc))
    laml_ref[0, 0:1, :] = jnp.concatenate([lf_r, lb_r], axis=1)
    laml_ref[0, 1:2, :] = jnp.concatenate([lf_i, lb_i], axis=1)


def _s5_prep(a_re, a_im, lam_re, lam_im, b_re, b_im, c_re, c_im):
    g, _, p = a_re.shape
    c, lc = SSM_GROUP, S5_CHUNK
    n = c * lc
    row = lambda x: x.reshape(g, 2, 1, p)
    col = lambda x: x.reshape(g, 2, p, 1)
    tr = lambda x: jnp.swapaxes(x, -1, -2)
    args = [row(a_re), row(a_im), col(a_re), col(a_im), row(lam_re), row(lam_im),
            tr(b_re), tr(b_im), c_re, c_im, tr(c_re), tr(c_im)]
    spec4 = lambda shp: pl.BlockSpec((1,) + shp, lambda gi: (gi, 0, 0, 0))
    spec3 = lambda shp: pl.BlockSpec((1,) + shp, lambda gi: (gi, 0, 0))
    return pl.pallas_call(
        _s5_prep_kernel,
        grid=(g,),
        in_specs=[spec4(a.shape[1:]) for a in args],
        out_specs=[spec3((n, n)), spec3((n, 4 * p)), spec3((4 * p, n)), spec3((2, 2 * p))],
        out_shape=[
            jax.ShapeDtypeStruct((g, n, n), BF16),
            jax.ShapeDtypeStruct((g, n, 4 * p), BF16),
            jax.ShapeDtypeStruct((g, 4 * p, n), BF16),
            jax.ShapeDtypeStruct((g, 2, 2 * p), F32),
        ],
        scratch_shapes=[pltpu.VMEM((c * c, 2 * lc), F32)],
        compiler_params=_cparams(1, 48),
        name="s5_weights",
    )(*args)


def _s5_kernel(u_ref, w_ref, wst_ref, wout_ref, laml_ref, dsk_ref, y_ref, xloc_s, xin_s, *, nctx, nck):
    c, lc, p, p2 = SSM_GROUP, S5_CHUNK, SSM_STATE, 2 * SSM_STATE
    ub = jnp.concatenate([u_ref[0, c1] for c1 in range(c)], axis=1).astype(BF16)
    xloc_s[...] = _dot(ub, wst_ref[0])

    nlat = nck - nctx
    order_f = list(range(nlat, nck)) + list(range(nlat))
    order_b = list(range(nck - 1, nlat - 1, -1)) + list(range(nlat - 1, -1, -1))
    m_r, m_i = laml_ref[0, 0:1, :], laml_ref[0, 1:2, :]
    is_fwd = lax.broadcasted_iota(jnp.int32, (1, p2), 1) < p
    x_r = jnp.zeros((1, p2), F32)
    x_i = jnp.zeros((1, p2), F32)
    for kf, kb in zip(order_f, order_b):
        xin_s[kf:kf + 1, 0:p] = x_r[:, 0:p]
        xin_s[kb:kb + 1, p:p2] = x_r[:, p:p2]
        xin_s[kf:kf + 1, p2:p2 + p] = x_i[:, 0:p]
        xin_s[kb:kb + 1, p2 + p:2 * p2] = x_i[:, p:p2]
        loc_r = jnp.where(is_fwd, xloc_s[kf:kf + 1, 0:p2], xloc_s[kb:kb + 1, 0:p2])
        loc_i = jnp.where(is_fwd, xloc_s[kf:kf + 1, p2:2 * p2], xloc_s[kb:kb + 1, p2:2 * p2])
        x_r, x_i = m_r * x_r - m_i * x_i + loc_r, m_r * x_i + m_i * x_r + loc_i

    y = _dot(ub, w_ref[0]) + _dot(xin_s[...].astype(BF16), wout_ref[0])
    for c2 in range(c):
        y_ref[0, c2] = y[:, c2 * lc:(c2 + 1) * lc] + dsk_ref[0, c2:c2 + 1, :] * u_ref[0, c2]


def _s5(u_t, w, wst, wout, laml, dsk, nctx):
    b, wd, nck, lc = u_t.shape
    g, c = SSM_GROUPS, SSM_GROUP
    n = c * lc
    kern = functools.partial(_s5_kernel, nctx=nctx, nck=nck)
    gspec = lambda shp: pl.BlockSpec((1,) + shp, lambda gi, bi: (gi,) + (0,) * len(shp))
    io_spec = pl.BlockSpec((1, c, nck, lc), lambda gi, bi: (bi, gi, 0, 0))
    return pl.pallas_call(
        kern,
        grid=(g, b),
        in_specs=[io_spec, gspec((n, n)), gspec((n, 4 * SSM_STATE)), gspec((4 * SSM_STATE, n)),
                  gspec((2, 2 * SSM_STATE)), gspec((c, lc))],
        out_specs=io_spec,
        out_shape=jax.ShapeDtypeStruct(u_t.shape, F32),
        scratch_shapes=[
            pltpu.VMEM((nck, 4 * SSM_STATE), F32),
            pltpu.VMEM((nck, 4 * SSM_STATE), F32),
        ],
        compiler_params=_cparams(2, 48),
        name="s5_scan",
    )(u_t, w, wst, wout, laml, dsk)


def _merge_kernel(x_ref, att_ref, of_ref, ob_ref, hg_ref, yt_ref, gate_ref, g1_ref,
                  wa_ref, wr_ref, ws_ref, wo_ref, wglu_ref, bglu_ref, hn_ref, gm_ref, o_ref):
    r = of_ref[0].astype(F32) + ob_ref[0].astype(F32)
    ms = _dot((r * r).astype(BF16), gm_ref[...])
    g = hg_ref[0].astype(F32)
    yrec = (r * lax.rsqrt(ms + RMS_EPS) * hn_ref[...]) * (g * _sigmoid(g))

    ys = yt_ref[0].T
    z = 0.5 * ys * (1.0 + jnp.tanh(math.sqrt(2.0 / math.pi) * (ys + 0.044715 * (ys * ys * ys))))
    yssm = z * _sigmoid(_dot(z.astype(BF16), wglu_ref[...]) + bglu_ref[...])

    d = D_MODEL
    m = gate_ref[0, :, 0:d].astype(F32) * _dot(att_ref[0], wa_ref[...])
    m = m + gate_ref[0, :, d:2 * d].astype(F32) * _dot(yrec.astype(BF16), wr_ref[...])
    m = m + gate_ref[0, :, 2 * d:3 * d].astype(F32) * _dot(yssm.astype(BF16), ws_ref[...])
    y = _dot(m.astype(BF16), wo_ref[...])
    o_ref[0] = x_ref[0] + g1_ref[0] * y


def _merge(x, att, o_f, o_b, hg, y_t, t_off, gates, g1, wa, wr, ws, wo, wglu, bglu, hn, gm):
    b, t, d = x.shape
    tm = min(512, t)
    off = t_off // tm
    row = lambda bi, i: (bi, i, 0)
    vec = lambda bi, i: (bi, 0, 0)
    return pl.pallas_call(
        _merge_kernel,
        grid=(b, t // tm),
        in_specs=[
            pl.BlockSpec((1, tm, d), row),
            pl.BlockSpec((1, tm, ATT_WIDTH), row),
            pl.BlockSpec((1, tm, HG_WIDTH), row),
            pl.BlockSpec((1, tm, HG_WIDTH), row),
            pl.BlockSpec((1, tm, HG_WIDTH), row),
            pl.BlockSpec((1, SSM_WIDTH, tm), lambda bi, i: (bi, 0, i + off)),
            pl.BlockSpec((1, tm, 3 * d), row),
            pl.BlockSpec((1, 1, d), vec),
            _const_spec(wa.shape), _const_spec(wr.shape), _const_spec(ws.shape), _const_spec(wo.shape),
            _const_spec(wglu.shape), _const_spec(bglu.shape), _const_spec(hn.shape), _const_spec(gm.shape),
        ],
        out_specs=pl.BlockSpec((1, tm, d), row),
        out_shape=jax.ShapeDtypeStruct((b, t, d), F32),
        compiler_params=_cparams(2, 48),
        name="merge_branches",
    )(x, att, o_f, o_b, hg, y_t, gates, g1, wa, wr, ws, wo, wglu, bglu, hn, gm)


def _ffn_kernel(x_ref, sh_ref, a_ref, g_ref, wup_ref, wdn_ref, o_ref, *, nj):
    x = x_ref[0]
    ms = jnp.mean(x * x, axis=-1, keepdims=True)
    hb = ((x * lax.rsqrt(ms + RMS_EPS)) * a_ref[0] + sh_ref[0]).astype(BF16)
    f = FFN_HIDDEN
    fc = f // nj
    acc = None
    for j in range(nj):
        a = _dot(hb, wup_ref[:, j * fc:(j + 1) * fc])
        bgate = _dot(hb, wup_ref[:, f + j * fc:f + (j + 1) * fc])
        act = ((a * _sigmoid(a)) * bgate).astype(BF16)
        part = _dot(act, wdn_ref[j * fc:(j + 1) * fc, :])
        acc = part if acc is None else acc + part
    o_ref[0] = x + g_ref[0] * acc


def _ffn(x, sh, a, g, wup, wdn):
    b, t, d = x.shape
    tm = min(512, t)
    row = lambda bi, i: (bi, i, 0)
    vec = lambda bi, i: (bi, 0, 0)
    return pl.pallas_call(
        functools.partial(_ffn_kernel, nj=2),
        grid=(b, t // tm),
        in_specs=[
            pl.BlockSpec((1, tm, d), row),
            pl.BlockSpec((1, 1, d), vec), pl.BlockSpec((1, 1, d), vec), pl.BlockSpec((1, 1, d), vec),
            _const_spec(wup.shape), _const_spec(wdn.shape),
        ],
        out_specs=pl.BlockSpec((1, tm, d), row),
        out_shape=jax.ShapeDtypeStruct((b, t, d), F32),
        compiler_params=_cparams(2, 56),
        name="swiglu_ffn",
    )(x, sh, a, g, wup, wdn)


def _rope_tables(t, identity=False):
    if identity:
        z = jnp.zeros((t, V7X_LANES), F32)
        return jnp.ones((t, V7X_LANES), F32), z, z
    pos = jnp.arange(t)
    row = (pos // GRID_W).astype(F32)
    col = (pos % GRID_W).astype(F32)
    axis_dim = HEAD_DIM // 2
    inv = ROPE_THETA ** (-jnp.arange(0, axis_dim, 2, dtype=F32) / axis_dim)
    ang_r, ang_c = row[:, None] * inv, col[:, None] * inv
    cr, sr, cc, sc = jnp.cos(ang_r), jnp.sin(ang_r), jnp.cos(ang_c), jnp.sin(ang_c)
    z = jnp.zeros_like(cr)
    rep = V7X_LANES // HEAD_DIM
    c = jnp.tile(jnp.concatenate([cr, cr, cc, cc], axis=1), (1, rep))
    s1 = jnp.tile(jnp.concatenate([-sr, z, -sc, z], axis=1), (1, rep))
    s2 = jnp.tile(jnp.concatenate([z, sr, z, sc], axis=1), (1, rep))
    return c, s1, s2


def _group_mean_matrix(width, group):
    i = jnp.arange(width) // group
    return jnp.where(i[:, None] == i[None, :], 1.0 / group, 0.0).astype(BF16)


def _kv_layout(k, v):
    b, tk, _ = k.shape
    kt = k.reshape(b, tk, ATT_KV_HEADS, HEAD_DIM).transpose(0, 2, 3, 1)
    vh = v.reshape(b, tk, ATT_KV_HEADS, HEAD_DIM).transpose(0, 2, 1, 3)
    ones = jnp.ones((b, ATT_KV_HEADS, tk, 1), BF16)
    zeros = jnp.zeros((b, ATT_KV_HEADS, tk, V7X_LANES - HEAD_DIM - 1), BF16)
    return kt, jnp.concatenate([vh, ones, zeros], axis=-1)


def kernel(x, c, ctx, c_ctx, w_mod, b_mod, norm1_g, norm2_g, w_in, q_norm_g, k_norm_g, hgrn_lb, hgrn_norm_g,
           ssm_lam_re, ssm_lam_im, ssm_log_dt, ssm_b_re, ssm_b_im, ssm_c_re, ssm_c_im, ssm_d, w_glu, b_glu,
           w_br_attn, w_br_hgrn, w_br_ssm, w_out, w_ffn_up, w_ffn_down):
    bsz, t, d = x.shape
    t_ctx = ctx.shape[1]
    depth = w_mod.shape[0]
    assert t % 512 == 0 and t_ctx % S5_CHUNK == 0 and d == D_MODEL

    lb_soft = jax.nn.softmax(hgrn_lb.astype(F32), axis=0)
    lower_bounds = jnp.cumsum(lb_soft, axis=0) - lb_soft[0]
    rope_lat = _rope_tables(t)
    rope_ctx = _rope_tables(t_ctx, identity=True)
    gmq = _group_mean_matrix(ATT_WIDTH, HEAD_DIM)
    gmk = _group_mean_matrix(ATT_KV_WIDTH, HEAD_DIM)
    gmh = _group_mean_matrix(HG_WIDTH, HG_DK)
    cond8 = jnp.zeros((8, d), F32).at[:bsz].set(c).at[bsz].set(c_ctx)
    mods = _modulation(cond8, w_mod, b_mod)

    x_lat, x_ctx = x, ctx
    for l in range(depth):
        with_ctx = l < depth - 1
        ml = mods[l, :bsz].reshape(bsz, ADALN_CHUNKS, 1, d)
        mc = jnp.broadcast_to(mods[l, bsz].reshape(1, ADALN_CHUNKS, 1, d), (bsz, ADALN_CHUNKS, 1, d))
        sh1, sc1, g1, sh2, sc2, g2 = [ml[:, i] for i in range(ADALN_CHUNKS)]
        csh1, csc1, cg1, csh2, csc2, cg2 = [mc[:, i] for i in range(ADALN_CHUNKS)]
        n1, n2 = norm1_g[l].reshape(1, 1, d), norm2_g[l].reshape(1, 1, d)

        w_in_b = w_in[l].astype(BF16)
        qg = (jnp.tile(q_norm_g[l], ATT_HEADS) * (HEAD_DIM ** -0.5 * math.log2(math.e))).reshape(1, ATT_WIDTH)
        kg = jnp.tile(k_norm_g[l], ATT_KV_HEADS).reshape(1, ATT_KV_WIDTH)
        lat = _inproj(x_lat, sh1, n1 * (1.0 + sc1), w_in_b, qg, kg, rope_lat, gmq, gmk)
        cx = _inproj(x_ctx, csh1, n1 * (1.0 + csc1), w_in_b, qg, kg, rope_ctx, gmq, gmk)
        q_l, k_l, v_l, hq_l, hf_l, hi_l, hg_l, ut_l, gate_l = lat
        q_c, k_c, v_c, hq_c, hf_c, hi_c, hg_c, ut_c, gate_c = cx

        kt_all, v_all = _kv_layout(jnp.concatenate([k_c, k_l], axis=1), jnp.concatenate([v_c, v_l], axis=1))
        a_lat = _attention(q_l, kt_all, v_all)
        if with_ctx:
            a_ctx = _attention(q_c, *_kv_layout(k_c, v_c))

        lb2 = lower_bounds[l]
        s_zero = jnp.zeros((bsz, 2, HG_WIDTH, HG_WIDTH), F32)
        of_c, ob_c, s_ctx = _hgrn(hq_c, hf_c, hi_c, lb2, s_zero)
        of_l, ob_l, _ = _hgrn(hq_l, hf_l, hi_l, lb2, s_ctx)

        dt = jnp.exp(ssm_log_dt[l].astype(F32))[..., None]
        gd = lambda a: jnp.swapaxes(a.astype(F32), 0, 1)
        w_s5, wst, wout, laml = _s5_prep(
            gd(ssm_lam_re[l] * dt), gd(ssm_lam_im[l] * dt), gd(ssm_lam_re[l]), gd(ssm_lam_im[l]),
            gd(ssm_b_re[l]), gd(ssm_b_im[l]), gd(ssm_c_re[l]), gd(ssm_c_im[l]))
        dsk = jnp.broadcast_to(ssm_d[l].astype(F32).reshape(SSM_GROUPS, SSM_GROUP, 1), (SSM_GROUPS, SSM_GROUP, S5_CHUNK))
        u_t = jnp.concatenate([ut_l, ut_c], axis=2).reshape(bsz, SSM_WIDTH, (t_ctx + t) // S5_CHUNK, S5_CHUNK)
        y_t = _s5(u_t, w_s5, wst, wout, laml, dsk, t_ctx // S5_CHUNK).reshape(bsz, SSM_WIDTH, t_ctx + t)

        hn = jnp.tile(hgrn_norm_g[l], HG_HEADS).reshape(1, HG_WIDTH)
        mw = (w_br_attn[l].astype(BF16), w_br_hgrn[l].astype(BF16), w_br_ssm[l].astype(BF16), w_out[l].astype(BF16),
              w_glu[l].astype(BF16), b_glu[l].reshape(1, SSM_WIDTH), hn, gmh)
        wup, wdn = w_ffn_up[l].astype(BF16), w_ffn_down[l].astype(BF16)
        x_lat = _merge(x_lat, a_lat, of_l, ob_l, hg_l, y_t, 0, gate_l, g1, *mw)
        x_lat = _ffn(x_lat, sh2, n2 * (1.0 + sc2), g2, wup, wdn)
        if with_ctx:
            x_ctx = _merge(x_ctx, a_ctx, of_c, ob_c, hg_c, y_t, t, gate_c, cg1, *mw)
            x_ctx = _ffn(x_ctx, csh2, n2 * (1.0 + csc2), cg2, wup, wdn)
    return x_lat
```

```python
import functools
import math

import jax
import jax.numpy as jnp
from jax import lax
from jax.experimental import pallas as pl
from jax.experimental.pallas import tpu as pltpu

F32 = jnp.float32
BF16 = jnp.bfloat16

D_MODEL = 1024
GRID_W = 64
RMS_EPS = 1e-6
ADALN_CHUNKS = 6
ATT_HEADS = 8
ATT_KV_HEADS = 2
ATT_GROUP = ATT_HEADS // ATT_KV_HEADS
HEAD_DIM = 64
ATT_WIDTH = ATT_HEADS * HEAD_DIM
ATT_KV_WIDTH = ATT_KV_HEADS * HEAD_DIM
ROPE_THETA = 10000.0
HG_HEADS = 4
HG_DK = 64
HG_WIDTH = HG_HEADS * HG_DK
SSM_WIDTH = 256
SSM_GROUP = 16
SSM_GROUPS = SSM_WIDTH // SSM_GROUP
SSM_STATE = 64
FFN_HIDDEN = 2816
N_IN = 5376
O_Q, O_K, O_V, O_HQ, O_HF, O_HI, O_HG, O_U, O_GATE = 0, 512, 640, 768, 1024, 1536, 1792, 2048, 2304

V7X_LANES = 128
V7X_VMEM_BYTES = 64 * 1024 * 1024
MIB = 1024 * 1024

HG_TILE = 128
S5_CHUNK = 128
LOG2E = math.log2(math.e)


def _dot(a, b):
    return jnp.dot(a, b, preferred_element_type=F32)


def _dot_nt(a, b):
    return lax.dot_general(a, b, (((1,), (1,)), ((), ())), preferred_element_type=F32)


def _dot_tn(a, b):
    return lax.dot_general(a, b, (((0,), (0,)), ((), ())), preferred_element_type=F32)


def _split(x):
    hi = x.astype(BF16)
    lo = (x - hi.astype(F32)).astype(BF16)
    return hi, lo


def _dot3(a, b):
    ah, al = _split(a)
    bh, bl = _split(b)
    return _dot(ah, bh) + (_dot(ah, bl) + _dot(al, bh))


def _sigmoid(x):
    return jax.nn.sigmoid(x)


def _cparams(n_axes, vmem_mib):
    return pltpu.CompilerParams(
        dimension_semantics=("arbitrary",) * n_axes,
        vmem_limit_bytes=min(vmem_mib * MIB, V7X_VMEM_BYTES - 4 * MIB),
    )


def _const_spec(shape):
    nd = len(shape)
    return pl.BlockSpec(shape, lambda *_: (0,) * nd, pipeline_mode=pl.Buffered(1))


def _mod_kernel(c_ref, w_ref, b_ref, o_ref):
    c = c_ref[...]
    s = c * _sigmoid(c)
    o_ref[0] = _dot3(s, w_ref[0]) + b_ref[0]


def _modulation(cond8, w_mod, b_mod):
    n_layers, d, n = w_mod.shape
    nb = 1536
    return pl.pallas_call(
        _mod_kernel,
        grid=(n_layers, n // nb),
        in_specs=[
            pl.BlockSpec((8, d), lambda l, j: (0, 0)),
            pl.BlockSpec((1, d, nb), lambda l, j: (l, 0, j)),
            pl.BlockSpec((1, 1, nb), lambda l, j: (l, 0, j)),
        ],
        out_specs=pl.BlockSpec((1, 8, nb), lambda l, j: (l, 0, j)),
        out_shape=jax.ShapeDtypeStruct((n_layers, 8, n), F32),
        compiler_params=_cparams(2, 40),
        name="adaln_modulation",
    )(cond8, w_mod, b_mod.reshape(n_layers, 1, n))


def _rope128(x, c, s1, s2):
    return x * c + pltpu.roll(x, V7X_LANES - 16, 1) * s1 + pltpu.roll(x, 16, 1) * s2


def _inproj_kernel(x_ref, sh_ref, a_ref, w_ref, qg_ref, kg_ref, c_ref, s1_ref, s2_ref, gmq_ref, gmk_ref,
                   q_ref, k_ref, v_ref, hq_ref, hf_ref, hi_ref, hg_ref, ut_ref, gate_ref):
    x = x_ref[0]
    ms = jnp.mean(x * x, axis=-1, keepdims=True)
    h = (x * lax.rsqrt(ms + RMS_EPS)) * a_ref[0] + sh_ref[0]
    hb = h.astype(BF16)

    def proj(lo, hi):
        return _dot(hb, w_ref[:, lo:hi])

    c, s1, s2 = c_ref[...], s1_ref[...], s2_ref[...]

    zq = proj(O_Q, O_K)
    msq = _dot((zq * zq).astype(BF16), gmq_ref[...])
    qn = zq * lax.rsqrt(msq + RMS_EPS) * qg_ref[...]
    for j in range(ATT_WIDTH // V7X_LANES):
        sl = slice(j * V7X_LANES, (j + 1) * V7X_LANES)
        q_ref[0, :, sl] = _rope128(qn[:, sl], c, s1, s2).astype(BF16)

    zk = proj(O_K, O_V)
    msk = _dot((zk * zk).astype(BF16), gmk_ref[...])
    kn = zk * lax.rsqrt(msk + RMS_EPS) * kg_ref[...]
    k_ref[0] = _rope128(kn, c, s1, s2).astype(BF16)

    v_ref[0] = proj(O_V, O_HQ).astype(BF16)
    hq_ref[0] = proj(O_HQ, O_HF).astype(BF16)
    hf_ref[0] = proj(O_HF, O_HI)
    hi_ref[0] = proj(O_HI, O_HG).astype(BF16)
    hg_ref[0] = proj(O_HG, O_U).astype(BF16)
    ut_ref[0] = proj(O_U, O_GATE).T
    for j in range(3):
        lo = O_GATE + j * D_MODEL
        gate_ref[0, :, j * D_MODEL:(j + 1) * D_MODEL] = _sigmoid(proj(lo, lo + D_MODEL)).astype(BF16)


def _inproj(x, sh, a, w_in, qg, kg, rope, gmq, gmk):
    b, t, d = x.shape
    tm = min(512, t)
    c, s1, s2 = rope
    row = lambda bi, i: (bi, i, 0)
    vec = lambda bi, i: (bi, 0, 0)
    tab = lambda bi, i: (i, 0)
    outs = [
        (ATT_WIDTH, BF16), (ATT_KV_WIDTH, BF16), (ATT_KV_WIDTH, BF16),
        (HG_WIDTH, BF16), (2 * HG_WIDTH, F32), (HG_WIDTH, BF16), (HG_WIDTH, BF16),
    ]
    out_shape = [jax.ShapeDtypeStruct((b, t, w), dt) for w, dt in outs]
    out_specs = [pl.BlockSpec((1, tm, w), row) for w, _ in outs]
    out_shape.append(jax.ShapeDtypeStruct((b, SSM_WIDTH, t), F32))
    out_specs.append(pl.BlockSpec((1, SSM_WIDTH, tm), lambda bi, i: (bi, 0, i)))
    out_shape.append(jax.ShapeDtypeStruct((b, t, 3 * D_MODEL), BF16))
    out_specs.append(pl.BlockSpec((1, tm, 3 * D_MODEL), row))
    return pl.pallas_call(
        _inproj_kernel,
        grid=(b, t // tm),
        in_specs=[
            pl.BlockSpec((1, tm, d), row),
            pl.BlockSpec((1, 1, d), vec),
            pl.BlockSpec((1, 1, d), vec),
            _const_spec((d, N_IN)),
            _const_spec((1, ATT_WIDTH)),
            _const_spec((1, ATT_KV_WIDTH)),
            pl.BlockSpec((tm, V7X_LANES), tab),
            pl.BlockSpec((tm, V7X_LANES), tab),
            pl.BlockSpec((tm, V7X_LANES), tab),
            _const_spec((ATT_WIDTH, ATT_WIDTH)),
            _const_spec((ATT_KV_WIDTH, ATT_KV_WIDTH)),
        ],
        out_specs=out_specs,
        out_shape=out_shape,
        compiler_params=_cparams(2, 56),
        name="in_projection",
    )(x, sh, a, w_in, qg, kg, c, s1, s2, gmq, gmk)


ATTN_V_ROWS = 80


def _attn_kernel(q_ref, k_ref, vt_ref, o_ref, qt_ref, m_ref, acc_ref, sa_ref, sb_ref, *, tk, nkb):
    tq = q_ref.shape[1]
    qt = q_ref[0].astype(F32).T
    qt_ref[...] = jnp.concatenate(
        [qt[HEAD_DIM * g:HEAD_DIM * (g + 1), :] for g in range(ATT_GROUP)], axis=1).astype(BF16)
    m_ref[...] = jnp.full(m_ref.shape, -jnp.inf, F32)
    acc_ref[...] = jnp.zeros(acc_ref.shape, F32)

    def scores(kb, s_ref):
        off = pl.multiple_of(kb * tk, tk)
        s_ref[...] = _dot(k_ref[0, 0, pl.ds(off, tk), :], qt_ref[...])

    def consume(kb, s_ref):
        off = pl.multiple_of(kb * tk, tk)
        s = s_ref[...]
        m_prev = m_ref[...]
        m_new = jnp.maximum(m_prev, jnp.max(s, axis=0, keepdims=True))
        alpha = jnp.exp2(m_prev - m_new)
        p = jnp.exp2(s - m_new).astype(BF16)
        acc_ref[...] = alpha * acc_ref[...] + _dot(vt_ref[0, 0, :, pl.ds(off, tk)], p)
        m_ref[...] = m_new

    bufs = (sa_ref, sb_ref)
    scores(0, sa_ref)

    def body(i, carry):
        base = ATTN_BLOCKS_PER_ITER * i
        for j in range(ATTN_BLOCKS_PER_ITER):
            scores(base + j + 1, bufs[(j + 1) % 2])
            consume(base + j, bufs[j % 2])
        return carry

    n_it = (nkb - 1) // ATTN_BLOCKS_PER_ITER
    lax.fori_loop(0, n_it, body, 0)
    for r in range(n_it * ATTN_BLOCKS_PER_ITER, nkb):
        if r + 1 < nkb:
            scores(r + 1, bufs[(r + 1) % 2])
        consume(r, bufs[r % 2])

    acc = acc_ref[...]
    out_t = acc[0:HEAD_DIM, :] / acc[HEAD_DIM:HEAD_DIM + 1, :]
    out_t = jnp.concatenate([out_t[:, g * tq:(g + 1) * tq] for g in range(ATT_GROUP)], axis=0)
    o_ref[0] = out_t.T.astype(BF16)


ATTN_KEY_BLOCKS = (640, 512, 256, 128)
ATTN_BLOCKS_PER_ITER = 6


def _attention(q, k, vt_ext):
    b, t, _ = q.shape
    tkeys = k.shape[2]
    tq = min(256, t)
    tk = next(c for c in ATTN_KEY_BLOCKS if tkeys % c == 0)
    kern = functools.partial(_attn_kernel, tk=tk, nkb=tkeys // tk)
    gw = ATT_GROUP * HEAD_DIM
    m = ATT_GROUP * tq
    return pl.pallas_call(
        kern,
        grid=(b, ATT_KV_HEADS, t // tq),
        in_specs=[
            pl.BlockSpec((1, tq, gw), lambda bi, h, i: (bi, i, h)),
            pl.BlockSpec((1, 1, tkeys, HEAD_DIM), lambda bi, h, i: (bi, h, 0, 0)),
            pl.BlockSpec((1, 1, ATTN_V_ROWS, tkeys), lambda bi, h, i: (bi, h, 0, 0)),
        ],
        out_specs=pl.BlockSpec((1, tq, gw), lambda bi, h, i: (bi, i, h)),
        out_shape=jax.ShapeDtypeStruct((b, t, ATT_WIDTH), BF16),
        scratch_shapes=[
            pltpu.VMEM((HEAD_DIM, m), BF16),
            pltpu.VMEM((1, m), F32),
            pltpu.VMEM((ATTN_V_ROWS, m), F32),
            pltpu.VMEM((tk, m), F32),
            pltpu.VMEM((tk, m), F32),
        ],
        compiler_params=_cparams(3, 48),
        name="gqa_attention",
    )(q, k, vt_ext)


HG_LEVELS = 7
HG_SEL_LEVELS = 3


def _hgrn_kernel(qf_ref, ff_ref, vf_ref, qb_ref, fb_ref, vb_ref, lb_ref, s0_ref,
                 tri_ref, sel_ref, sm_ref, hm_ref, gsum_ref, bd_ref,
                 of_ref, ob_ref, sfin_ref, stf_ref, stb_ref, cf_s, cb_s):
    @pl.when(pl.program_id(1) == 0)
    def _():
        stf_ref[...] = s0_ref[0, 0]
        stb_ref[...] = s0_ref[0, 1]

    tt, w = HG_TILE, HG_WIDTH
    dirs = (0, 1)
    q_refs, f_refs, v_refs = (qf_ref, qb_ref), (ff_ref, fb_ref), (vf_ref, vb_ref)
    o_refs, st_refs, c_refs = (of_ref, ob_ref), (stf_ref, stb_ref), (cf_s, cb_s)
    hm = [hm_ref[h] for h in range(HG_HEADS)]

    qs, kin, vb, c, tot, ref_small, o, scores = [], [], [], [], [], [], [], [None, None]
    for d in dirs:
        q = q_refs[d][0].astype(F32)
        fpre = f_refs[d][0]
        lb = lb_ref[d:d + 1, :]
        qs.append(q * _sigmoid(q))
        kin.append((1.0 - lb) * _sigmoid(-fpre))
        vb.append(v_refs[d][0])
        hi, lo = _split(jnp.log(lb + (1.0 - lb) * _sigmoid(fpre)))
        cd = (_dot(tri_ref[d], hi) + _dot(tri_ref[d], lo)) * LOG2E
        c_refs[d][...] = cd
        c.append(cd)
    for d in dirs:
        last = 0 if d else tt - 1
        tot.append(c_refs[d][last:last + 1, :])
        chi, clo = _split(c[d])
        ref_small.append(_dot(sel_ref[d], chi) + _dot(sel_ref[d], clo))
        o.append(_dot((qs[d] * kin[d]).astype(BF16), gsum_ref[...]) * vb[d].astype(F32))

    for lvl in range(HG_LEVELS):
        m = 1 << lvl
        for d in dirs:
            if lvl < HG_SEL_LEVELS:
                cref = ref_small[d][lvl * tt:(lvl + 1) * tt, :]
            else:
                rows = []
                for blk in range(tt // (2 * m)):
                    r = blk * 2 * m + (m if d else m - 1)
                    rows.append(jnp.broadcast_to(c_refs[d][r:r + 1, :], (2 * m, w)))
                cref = rows[0] if len(rows) == 1 else jnp.concatenate(rows, axis=0)
            wgt = jnp.exp2(-jnp.abs(c[d] - cref))
            ql = (qs[d] * wgt).astype(BF16)
            kl = (kin[d] * wgt).astype(BF16)
            kstack = jnp.concatenate([kl * hm[h] for h in range(HG_HEADS)], axis=0)
            sc = _dot_nt(ql, kstack).astype(BF16) * sm_ref[d, lvl]
            scores[d] = sc if scores[d] is None else scores[d] + sc

    for d in dirs:
        vstack = jnp.concatenate([vb[d] * hm[h] for h in range(HG_HEADS)], axis=0)
        o[d] = o[d] + _dot(scores[d], vstack)
    for d in dirs:
        st = st_refs[d][...]
        o[d] = o[d] + _dot_nt((qs[d] * jnp.exp2(c[d])).astype(BF16), st.astype(BF16))
        kv = _dot_tn(vb[d], (kin[d] * jnp.exp2(tot[d] - c[d])).astype(BF16))
        st_new = jnp.exp2(tot[d]) * st + kv * bd_ref[...]
        st_refs[d][...] = st_new
        sfin_ref[0, d] = st_new
        o_refs[d][0] = o[d].astype(BF16)


def _hgrn_constants():
    tt, w = HG_TILE, HG_WIDTH
    t = jnp.arange(tt)
    tri = jnp.stack([t[None, :] <= t[:, None], t[None, :] >= t[:, None]]).astype(BF16)
    sel = []
    for reverse in (False, True):
        per = []
        for lvl in range(HG_SEL_LEVELS):
            m = 1 << lvl
            r = (t // (2 * m)) * (2 * m) + (m if reverse else m - 1)
            per.append(t[None, :] == r[:, None])
        sel.append(jnp.concatenate(per, axis=0))
    sel = jnp.stack(sel).astype(BF16)
    col = jnp.arange(HG_HEADS * tt) % tt
    sm = []
    for reverse in (False, True):
        per = []
        for lvl in range(HG_LEVELS):
            t_up, s_up = ((t >> lvl) & 1) == 1, ((col >> lvl) & 1) == 1
            same = (t[:, None] >> (lvl + 1)) == (col[None, :] >> (lvl + 1))
            halves = (~t_up[:, None] & s_up[None, :]) if reverse else (t_up[:, None] & ~s_up[None, :])
            per.append(same & halves)
        sm.append(jnp.stack(per))
    sm = jnp.stack(sm).astype(BF16)
    lane_head = jnp.arange(w) // HG_DK
    hm = jnp.stack([jnp.broadcast_to((lane_head == h)[None, :], (tt, w)) for h in range(HG_HEADS)]).astype(BF16)
    same_head = lane_head[:, None] == lane_head[None, :]
    return tri, sel, sm, hm, same_head.astype(BF16), same_head.astype(F32)


def _hgrn(hq, hf, hv, lb2, s0, consts):
    b, t, w = hq.shape
    tt = HG_TILE
    nt = t // tt
    fwd = lambda bi, i: (bi, i, 0)
    bwd = lambda bi, i: (bi, nt - 1 - i, 0)
    bwd_f = lambda bi, i: (bi, nt - 1 - i, 1)
    st_spec = pl.BlockSpec((1, 2, w, w), lambda bi, i: (bi, 0, 0, 0))
    return pl.pallas_call(
        _hgrn_kernel,
        grid=(b, nt),
        in_specs=[
            pl.BlockSpec((1, tt, w), fwd), pl.BlockSpec((1, tt, w), fwd), pl.BlockSpec((1, tt, w), fwd),
            pl.BlockSpec((1, tt, w), bwd), pl.BlockSpec((1, tt, w), bwd_f), pl.BlockSpec((1, tt, w), bwd),
            pl.BlockSpec((2, w), lambda bi, i: (0, 0)),
            st_spec,
        ] + [_const_spec(a.shape) for a in consts],
        out_specs=[pl.BlockSpec((1, tt, w), fwd), pl.BlockSpec((1, tt, w), bwd), st_spec],
        out_shape=[
            jax.ShapeDtypeStruct((b, t, w), BF16),
            jax.ShapeDtypeStruct((b, t, w), BF16),
            jax.ShapeDtypeStruct((b, 2, w, w), F32),
        ],
        scratch_shapes=[
            pltpu.VMEM((w, w), F32),
            pltpu.VMEM((w, w), F32),
            pltpu.VMEM((tt, w), F32),
            pltpu.VMEM((tt, w), F32),
        ],
        compiler_params=_cparams(2, 32),
        name="hgrn2_scan",
    )(hq, hf, hv, hq, hf, hv, lb2, s0, *consts)


def _cpow(a_re, a_im, tau):
    mag = jnp.exp(tau * a_re)
    ang = tau * a_im
    return mag * jnp.cos(ang), mag * jnp.sin(ang)


def _s5_prep_kernel(ar_row, ai_row, ar_col, ai_col, lr_row, li_row, bt_r, bt_i, cr, ci, ctr, cti,
                    w_ref, wst_ref, wout_ref, laml_ref, kall_ref):
    lc, p, c = S5_CHUNK, SSM_STATE, SSM_GROUP
    tau_l = lax.broadcasted_iota(jnp.int32, (p, lc), 1).astype(F32)
    tau_s = lax.broadcasted_iota(jnp.int32, (lc, p), 0).astype(F32)

    btr, bti = [], []
    for d in range(2):
        e_r, e_i = _cpow(ar_row[0, d], ai_row[0, d], 1.0)
        l_r, l_i = lr_row[0, d], li_row[0, d]
        den = l_r * l_r + l_i * l_i
        f_r = ((e_r - 1.0) * l_r + e_i * l_i) / den
        f_i = (e_i * l_r - (e_r - 1.0) * l_i) / den
        btr.append(f_r * bt_r[0, d] - f_i * bt_i[0, d])
        bti.append(f_r * bt_i[0, d] + f_i * bt_r[0, d])

    def cb(d):
        re, im = [], []
        for c1 in range(c):
            b_r, b_i = btr[d][c1:c1 + 1, :], bti[d][c1:c1 + 1, :]
            re.append(b_r * cr[0, d] - b_i * ci[0, d])
            im.append(b_r * ci[0, d] + b_i * cr[0, d])
        return jnp.concatenate(re, axis=0), jnp.concatenate(im, axis=0)

    cbf_r, cbf_i = cb(0)
    pf_r, pf_i = _cpow(ar_col[0, 0], ai_col[0, 0], tau_l)
    kf = _dot3(cbf_r, pf_r) - _dot3(cbf_i, pf_i)
    cbb_r, cbb_i = cb(1)
    pb_r, pb_i = _cpow(ar_col[0, 1], ai_col[0, 1], lc - tau_l)
    kb = _dot3(cbb_r, pb_r) - _dot3(cbb_i, pb_i)
    lane = lax.broadcasted_iota(jnp.int32, (c * c, lc), 1)
    kf = kf + jnp.where(lane == 0, jnp.sum(cbb_r, axis=1, keepdims=True), 0.0)
    kall_ref[...] = jnp.concatenate([kf, kb], axis=1)

    def toeplitz_rows(c1, carry):
        for c2 in range(c):
            row = kall_ref[pl.ds(c1 * c + c2, 1), :]
            blk = pltpu.roll(jnp.broadcast_to(row, (lc, 2 * lc)), 0, 1, stride=1, stride_axis=0)
            w_ref[0, pl.ds(pl.multiple_of(c1 * lc, lc), lc), c2 * lc:(c2 + 1) * lc] = blk[:, :lc].astype(BF16)
        return carry

    lax.fori_loop(0, c, toeplitz_rows, 0)

    sf_r, sf_i = _cpow(ar_row[0, 0], ai_row[0, 0], (lc - 1) - tau_s)
    sb_r, sb_i = _cpow(ar_row[0, 1], ai_row[0, 1], tau_s)
    for c1 in range(c):
        re, im = [], []
        for d, (p_r, p_i) in enumerate(((sf_r, sf_i), (sb_r, sb_i))):
            b_r, b_i = btr[d][c1:c1 + 1, :], bti[d][c1:c1 + 1, :]
            re.append(p_r * b_r - p_i * b_i)
            im.append(p_r * b_i + p_i * b_r)
        wst_ref[0, c1 * lc:(c1 + 1) * lc, :] = jnp.concatenate(re + im, axis=1).astype(BF16)

    of_r, of_i = _cpow(ar_col[0, 0], ai_col[0, 0], tau_l + 1.0)
    ob_r, ob_i = _cpow(ar_col[0, 1], ai_col[0, 1], lc - tau_l)
    for c2 in range(c):
        re, im = [], []
        for d, (p_r, p_i) in enumerate(((of_r, of_i), (ob_r, ob_i))):
            c_r, c_i = ctr[0, d, :, c2:c2 + 1], cti[0, d, :, c2:c2 + 1]
            re.append(c_r * p_r - c_i * p_i)
            im.append(-(c_r * p_i + c_i * p_r))
        wout_ref[0, :, c2 * lc:(c2 + 1) * lc] = jnp.concatenate(re + im, axis=0).astype(BF16)

    lf_r, lf_i = _cpow(ar_row[0, 0], ai_row[0, 0], float(lc))
    lb_r, lb_i = _cpow(ar_row[0, 1], ai_row[0, 1], float(lc))
    laml_ref[0, 0:1, :] = jnp.concatenate([lf_r, lb_r], axis=1)
    laml_ref[0, 1:2, :] = jnp.concatenate([lf_i, lb_i], axis=1)


def _s5_prep(a_re, a_im, lam_re, lam_im, b_re, b_im, c_re, c_im):
    g, _, p = a_re.shape
    c, lc = SSM_GROUP, S5_CHUNK
    n = c * lc
    row = lambda x: x.reshape(g, 2, 1, p)
    col = lambda x: x.reshape(g, 2, p, 1)
    tr = lambda x: jnp.swapaxes(x, -1, -2)
    args = [row(a_re), row(a_im), col(a_re), col(a_im), row(lam_re), row(lam_im),
            tr(b_re), tr(b_im), c_re, c_im, tr(c_re), tr(c_im)]
    spec4 = lambda shp: pl.BlockSpec((1,) + shp, lambda gi: (gi, 0, 0, 0))
    spec3 = lambda shp: pl.BlockSpec((1,) + shp, lambda gi: (gi, 0, 0))
    return pl.pallas_call(
        _s5_prep_kernel,
        grid=(g,),
        in_specs=[spec4(a.shape[1:]) for a in args],
        out_specs=[spec3((n, n)), spec3((n, 4 * p)), spec3((4 * p, n)), spec3((2, 2 * p))],
        out_shape=[
            jax.ShapeDtypeStruct((g, n, n), BF16),
            jax.ShapeDtypeStruct((g, n, 4 * p), BF16),
            jax.ShapeDtypeStruct((g, 4 * p, n), BF16),
            jax.ShapeDtypeStruct((g, 2, 2 * p), F32),
        ],
        scratch_shapes=[pltpu.VMEM((c * c, 2 * lc), F32)],
        compiler_params=_cparams(1, 48),
        name="s5_weights",
    )(*args)


def _s5_kernel(u_ref, w_ref, wst_ref, wout_ref, laml_ref, dsk_ref, y_ref, xloc_s, xin_s, *, nctx, nck):
    c, lc, p, p2 = SSM_GROUP, S5_CHUNK, SSM_STATE, 2 * SSM_STATE
    ub = jnp.concatenate([u_ref[0, c1] for c1 in range(c)], axis=1).astype(BF16)
    xloc_s[...] = _dot(ub, wst_ref[0])

    nlat = nck - nctx
    order_f = list(range(nlat, nck)) + list(range(nlat))
    order_b = list(range(nck - 1, nlat - 1, -1)) + list(range(nlat - 1, -1, -1))
    m_r, m_i = laml_ref[0, 0:1, :], laml_ref[0, 1:2, :]
    is_fwd = lax.broadcasted_iota(jnp.int32, (1, p2), 1) < p
    x_r = jnp.zeros((1, p2), F32)
    x_i = jnp.zeros((1, p2), F32)
    for kf, kb in zip(order_f, order_b):
        xin_s[kf:kf + 1, 0:p] = x_r[:, 0:p]
        xin_s[kb:kb + 1, p:p2] = x_r[:, p:p2]
        xin_s[kf:kf + 1, p2:p2 + p] = x_i[:, 0:p]
        xin_s[kb:kb + 1, p2 + p:2 * p2] = x_i[:, p:p2]
        loc_r = jnp.where(is_fwd, xloc_s[kf:kf + 1, 0:p2], xloc_s[kb:kb + 1, 0:p2])
        loc_i = jnp.where(is_fwd, xloc_s[kf:kf + 1, p2:2 * p2], xloc_s[kb:kb + 1, p2:2 * p2])
        x_r, x_i = m_r * x_r - m_i * x_i + loc_r, m_r * x_i + m_i * x_r + loc_i

    y = _dot(ub, w_ref[0]) + _dot(xin_s[...].astype(BF16), wout_ref[0])
    for c2 in range(c):
        y_ref[0, c2] = y[:, c2 * lc:(c2 + 1) * lc] + dsk_ref[0, c2:c2 + 1, :] * u_ref[0, c2]


def _s5(u_t, w, wst, wout, laml, dsk, nctx):
    b, wd, nck, lc = u_t.shape
    g, c = SSM_GROUPS, SSM_GROUP
    n = c * lc
    kern = functools.partial(_s5_kernel, nctx=nctx, nck=nck)
    gspec = lambda shp: pl.BlockSpec((1,) + shp, lambda gi, bi: (gi,) + (0,) * len(shp))
    io_spec = pl.BlockSpec((1, c, nck, lc), lambda gi, bi: (bi, gi, 0, 0))
    return pl.pallas_call(
        kern,
        grid=(g, b),
        in_specs=[io_spec, gspec((n, n)), gspec((n, 4 * SSM_STATE)), gspec((4 * SSM_STATE, n)),
                  gspec((2, 2 * SSM_STATE)), gspec((c, lc))],
        out_specs=io_spec,
        out_shape=jax.ShapeDtypeStruct(u_t.shape, F32),
        scratch_shapes=[
            pltpu.VMEM((nck, 4 * SSM_STATE), F32),
            pltpu.VMEM((nck, 4 * SSM_STATE), F32),
        ],
        compiler_params=_cparams(2, 48),
        name="s5_scan",
    )(u_t, w, wst, wout, laml, dsk)


def _merge_kernel(x_ref, att_ref, of_ref, ob_ref, hg_ref, yt_ref, gate_ref, g1_ref,
                  wa_ref, wr_ref, ws_ref, wo_ref, wglu_ref, bglu_ref, hn_ref, gm_ref, o_ref):
    r = of_ref[0].astype(F32) + ob_ref[0].astype(F32)
    ms = _dot((r * r).astype(BF16), gm_ref[...])
    g = hg_ref[0].astype(F32)
    yrec = (r * lax.rsqrt(ms + RMS_EPS) * hn_ref[...]) * (g * _sigmoid(g))

    ys = yt_ref[0].T
    z = 0.5 * ys * (1.0 + jnp.tanh(math.sqrt(2.0 / math.pi) * (ys + 0.044715 * (ys * ys * ys))))
    yssm = z * _sigmoid(_dot(z.astype(BF16), wglu_ref[...]) + bglu_ref[...])

    d = D_MODEL
    m = gate_ref[0, :, 0:d].astype(F32) * _dot(att_ref[0], wa_ref[...])
    m = m + gate_ref[0, :, d:2 * d].astype(F32) * _dot(yrec.astype(BF16), wr_ref[...])
    m = m + gate_ref[0, :, 2 * d:3 * d].astype(F32) * _dot(yssm.astype(BF16), ws_ref[...])
    y = _dot(m.astype(BF16), wo_ref[...])
    o_ref[0] = x_ref[0] + g1_ref[0] * y


def _merge(x, att, o_f, o_b, hg, y_t, t_off, gates, g1, wa, wr, ws, wo, wglu, bglu, hn, gm):
    b, t, d = x.shape
    tm = min(512, t)
    off = t_off // tm
    row = lambda bi, i: (bi, i, 0)
    vec = lambda bi, i: (bi, 0, 0)
    return pl.pallas_call(
        _merge_kernel,
        grid=(b, t // tm),
        in_specs=[
            pl.BlockSpec((1, tm, d), row),
            pl.BlockSpec((1, tm, ATT_WIDTH), row),
            pl.BlockSpec((1, tm, HG_WIDTH), row),
            pl.BlockSpec((1, tm, HG_WIDTH), row),
            pl.BlockSpec((1, tm, HG_WIDTH), row),
            pl.BlockSpec((1, SSM_WIDTH, tm), lambda bi, i: (bi, 0, i + off)),
            pl.BlockSpec((1, tm, 3 * d), row),
            pl.BlockSpec((1, 1, d), vec),
            _const_spec(wa.shape), _const_spec(wr.shape), _const_spec(ws.shape), _const_spec(wo.shape),
            _const_spec(wglu.shape), _const_spec(bglu.shape), _const_spec(hn.shape), _const_spec(gm.shape),
        ],
        out_specs=pl.BlockSpec((1, tm, d), row),
        out_shape=jax.ShapeDtypeStruct((b, t, d), F32),
        compiler_params=_cparams(2, 48),
        name="merge_branches",
    )(x, att, o_f, o_b, hg, y_t, gates, g1, wa, wr, ws, wo, wglu, bglu, hn, gm)


def _ffn_kernel(x_ref, sh_ref, a_ref, g_ref, wup_ref, wdn_ref, o_ref, *, nj):
    x = x_ref[0]
    ms = jnp.mean(x * x, axis=-1, keepdims=True)
    hb = ((x * lax.rsqrt(ms + RMS_EPS)) * a_ref[0] + sh_ref[0]).astype(BF16)
    f = FFN_HIDDEN
    fc = f // nj
    acc = None
    for j in range(nj):
        a = _dot(hb, wup_ref[:, j * fc:(j + 1) * fc])
        bgate = _dot(hb, wup_ref[:, f + j * fc:f + (j + 1) * fc])
        act = ((a * _sigmoid(a)) * bgate).astype(BF16)
        part = _dot(act, wdn_ref[j * fc:(j + 1) * fc, :])
        acc = part if acc is None else acc + part
    o_ref[0] = x + g_ref[0] * acc


def _ffn(x, sh, a, g, wup, wdn):
    b, t, d = x.shape
    tm = min(512, t)
    row = lambda bi, i: (bi, i, 0)
    vec = lambda bi, i: (bi, 0, 0)
    return pl.pallas_call(
        functools.partial(_ffn_kernel, nj=2),
        grid=(b, t // tm),
        in_specs=[
            pl.BlockSpec((1, tm, d), row),
            pl.BlockSpec((1, 1, d), vec), pl.BlockSpec((1, 1, d), vec), pl.BlockSpec((1, 1, d), vec),
            _const_spec(wup.shape), _const_spec(wdn.shape),
        ],
        out_specs=pl.BlockSpec((1, tm, d), row),
        out_shape=jax.ShapeDtypeStruct((b, t, d), F32),
        compiler_params=_cparams(2, 56),
        name="swiglu_ffn",
    )(x, sh, a, g, wup, wdn)


def _rope_tables(t, identity=False):
    if identity:
        z = jnp.zeros((t, V7X_LANES), F32)
        return jnp.ones((t, V7X_LANES), F32), z, z
    pos = jnp.arange(t)
    row = (pos // GRID_W).astype(F32)
    col = (pos % GRID_W).astype(F32)
    axis_dim = HEAD_DIM // 2
    inv = ROPE_THETA ** (-jnp.arange(0, axis_dim, 2, dtype=F32) / axis_dim)
    ang_r, ang_c = row[:, None] * inv, col[:, None] * inv
    cr, sr, cc, sc = jnp.cos(ang_r), jnp.sin(ang_r), jnp.cos(ang_c), jnp.sin(ang_c)
    z = jnp.zeros_like(cr)
    rep = V7X_LANES // HEAD_DIM
    c = jnp.tile(jnp.concatenate([cr, cr, cc, cc], axis=1), (1, rep))
    s1 = jnp.tile(jnp.concatenate([-sr, z, -sc, z], axis=1), (1, rep))
    s2 = jnp.tile(jnp.concatenate([z, sr, z, sc], axis=1), (1, rep))
    return c, s1, s2


def _group_mean_matrix(width, group):
    i = jnp.arange(width) // group
    return jnp.where(i[:, None] == i[None, :], 1.0 / group, 0.0).astype(BF16)


def _kv_layout(k, v):
    b, tk, _ = k.shape
    kh = k.reshape(b, tk, ATT_KV_HEADS, HEAD_DIM).transpose(0, 2, 1, 3)
    vt = v.reshape(b, tk, ATT_KV_HEADS, HEAD_DIM).transpose(0, 2, 3, 1)
    ones = jnp.ones((b, ATT_KV_HEADS, 1, tk), BF16)
    zeros = jnp.zeros((b, ATT_KV_HEADS, ATTN_V_ROWS - HEAD_DIM - 1, tk), BF16)
    return kh, jnp.concatenate([vt, ones, zeros], axis=2)


def kernel(x, c, ctx, c_ctx, w_mod, b_mod, norm1_g, norm2_g, w_in, q_norm_g, k_norm_g, hgrn_lb, hgrn_norm_g,
           ssm_lam_re, ssm_lam_im, ssm_log_dt, ssm_b_re, ssm_b_im, ssm_c_re, ssm_c_im, ssm_d, w_glu, b_glu,
           w_br_attn, w_br_hgrn, w_br_ssm, w_out, w_ffn_up, w_ffn_down):
    bsz, t, d = x.shape
    t_ctx = ctx.shape[1]
    depth = w_mod.shape[0]
    assert t % 512 == 0 and t_ctx % S5_CHUNK == 0 and d == D_MODEL

    lb_soft = jax.nn.softmax(hgrn_lb.astype(F32), axis=0)
    lower_bounds = jnp.cumsum(lb_soft, axis=0) - lb_soft[0]
    rope_lat = _rope_tables(t)
    rope_ctx = _rope_tables(t_ctx, identity=True)
    gmq = _group_mean_matrix(ATT_WIDTH, HEAD_DIM)
    gmk = _group_mean_matrix(ATT_KV_WIDTH, HEAD_DIM)
    gmh = _group_mean_matrix(HG_WIDTH, HG_DK)
    hg_consts = _hgrn_constants()
    cond8 = jnp.zeros((8, d), F32).at[:bsz].set(c).at[bsz].set(c_ctx)
    mods = _modulation(cond8, w_mod, b_mod)

    x_lat, x_ctx = x, ctx
    for l in range(depth):
        with_ctx = l < depth - 1
        ml = mods[l, :bsz].reshape(bsz, ADALN_CHUNKS, 1, d)
        mc = jnp.broadcast_to(mods[l, bsz].reshape(1, ADALN_CHUNKS, 1, d), (bsz, ADALN_CHUNKS, 1, d))
        sh1, sc1, g1, sh2, sc2, g2 = [ml[:, i] for i in range(ADALN_CHUNKS)]
        csh1, csc1, cg1, csh2, csc2, cg2 = [mc[:, i] for i in range(ADALN_CHUNKS)]
        n1, n2 = norm1_g[l].reshape(1, 1, d), norm2_g[l].reshape(1, 1, d)

        w_in_b = w_in[l].astype(BF16)
        qg = (jnp.tile(q_norm_g[l], ATT_HEADS) * (HEAD_DIM ** -0.5 * LOG2E)).reshape(1, ATT_WIDTH)
        kg = jnp.tile(k_norm_g[l], ATT_KV_HEADS).reshape(1, ATT_KV_WIDTH)
        lat = _inproj(x_lat, sh1, n1 * (1.0 + sc1), w_in_b, qg, kg, rope_lat, gmq, gmk)
        cx = _inproj(x_ctx, csh1, n1 * (1.0 + csc1), w_in_b, qg, kg, rope_ctx, gmq, gmk)
        q_l, k_l, v_l, hq_l, hf_l, hi_l, hg_l, ut_l, gate_l = lat
        q_c, k_c, v_c, hq_c, hf_c, hi_c, hg_c, ut_c, gate_c = cx

        k_all, vt_all = _kv_layout(jnp.concatenate([k_c, k_l], axis=1), jnp.concatenate([v_c, v_l], axis=1))
        a_lat = _attention(q_l, k_all, vt_all)
        if with_ctx:
            a_ctx = _attention(q_c, *_kv_layout(k_c, v_c))

        lb2 = lower_bounds[l]
        s_zero = jnp.zeros((bsz, 2, HG_WIDTH, HG_WIDTH), F32)
        of_c, ob_c, s_ctx = _hgrn(hq_c, hf_c, hi_c, lb2, s_zero, hg_consts)
        of_l, ob_l, _ = _hgrn(hq_l, hf_l, hi_l, lb2, s_ctx, hg_consts)

        dt = jnp.exp(ssm_log_dt[l].astype(F32))[..., None]
        gd = lambda a: jnp.swapaxes(a.astype(F32), 0, 1)
        w_s5, wst, wout, laml = _s5_prep(
            gd(ssm_lam_re[l] * dt), gd(ssm_lam_im[l] * dt), gd(ssm_lam_re[l]), gd(ssm_lam_im[l]),
            gd(ssm_b_re[l]), gd(ssm_b_im[l]), gd(ssm_c_re[l]), gd(ssm_c_im[l]))
        dsk = jnp.broadcast_to(ssm_d[l].astype(F32).reshape(SSM_GROUPS, SSM_GROUP, 1), (SSM_GROUPS, SSM_GROUP, S5_CHUNK))
        u_t = jnp.concatenate([ut_l, ut_c], axis=2).reshape(bsz, SSM_WIDTH, (t_ctx + t) // S5_CHUNK, S5_CHUNK)
        y_t = _s5(u_t, w_s5, wst, wout, laml, dsk, t_ctx // S5_CHUNK).reshape(bsz, SSM_WIDTH, t_ctx + t)

        hn = jnp.tile(hgrn_norm_g[l], HG_HEADS).reshape(1, HG_WIDTH)
        mw = (w_br_attn[l].astype(BF16), w_br_hgrn[l].astype(BF16), w_br_ssm[l].astype(BF16), w_out[l].astype(BF16),
              w_glu[l].astype(BF16), b_glu[l].reshape(1, SSM_WIDTH), hn, gmh)
        wup, wdn = w_ffn_up[l].astype(BF16), w_ffn_down[l].astype(BF16)
        x_lat = _merge(x_lat, a_lat, of_l, ob_l, hg_l, y_t, 0, gate_l, g1, *mw)
        x_lat = _ffn(x_lat, sh2, n2 * (1.0 + sc2), g2, wup, wdn)
        if with_ctx:
            x_ctx = _merge(x_ctx, a_ctx, of_c, ob_c, hg_c, y_t, t, gate_c, cg1, *mw)
            x_ctx = _ffn(x_ctx, csh2, n2 * (1.0 + csc2), cg2, wup, wdn)
    return x_lat
```

```python
import functools
import math

import jax
import jax.numpy as jnp
import numpy as np
from jax import lax
from jax.experimental import pallas as pl
from jax.experimental.pallas import tpu as pltpu

F32 = jnp.float32
BF16 = jnp.bfloat16

D_MODEL = 1024
GRID_W = 64
RMS_EPS = 1e-6
ADALN_CHUNKS = 6
ATT_HEADS = 8
ATT_KV_HEADS = 2
ATT_GROUP = ATT_HEADS // ATT_KV_HEADS
HEAD_DIM = 64
ATT_WIDTH = ATT_HEADS * HEAD_DIM
ATT_KV_WIDTH = ATT_KV_HEADS * HEAD_DIM
ROPE_THETA = 10000.0
HG_HEADS = 4
HG_DK = 64
HG_WIDTH = HG_HEADS * HG_DK
SSM_WIDTH = 256
SSM_GROUP = 16
SSM_GROUPS = SSM_WIDTH // SSM_GROUP
SSM_STATE = 64
FFN_HIDDEN = 2816
N_IN = 5376
O_Q, O_K, O_V, O_HQ, O_HF, O_HI, O_HG, O_U, O_GATE = 0, 512, 640, 768, 1024, 1536, 1792, 2048, 2304

V7X_LANES = 128
V7X_VMEM_BYTES = 64 * 1024 * 1024
MIB = 1024 * 1024

INPROJ_TILE = 256
ATTN_Q_TILE = 256
ATTN_V_ROWS = 80
HG_TILE = 128
S5_CHUNK = 128
LOG2E = math.log2(math.e)


def _dot(a, b):
    return jnp.dot(a, b, preferred_element_type=F32)


def _dot_nt(a, b):
    return lax.dot_general(a, b, (((1,), (1,)), ((), ())), preferred_element_type=F32)


def _dot_tn(a, b):
    return lax.dot_general(a, b, (((0,), (0,)), ((), ())), preferred_element_type=F32)


def _split(x):
    hi = x.astype(BF16)
    lo = (x - hi.astype(F32)).astype(BF16)
    return hi, lo


def _dot3(a, b):
    ah, al = _split(a)
    bh, bl = _split(b)
    return _dot(ah, bh) + (_dot(ah, bl) + _dot(al, bh))


def _sigmoid(x):
    return jax.nn.sigmoid(x)


def _cparams(n_axes, vmem_mib):
    return pltpu.CompilerParams(
        dimension_semantics=("arbitrary",) * n_axes,
        vmem_limit_bytes=min(vmem_mib * MIB, V7X_VMEM_BYTES - 4 * MIB),
    )


def _const_spec(shape):
    nd = len(shape)
    return pl.BlockSpec(shape, lambda *_: (0,) * nd, pipeline_mode=pl.Buffered(1))


def _mod_kernel(c_ref, w_ref, b_ref, o_ref):
    c = c_ref[...]
    s = c * _sigmoid(c)
    o_ref[0] = _dot3(s, w_ref[0]) + b_ref[0]


def _modulation(cond8, w_mod, b_mod):
    n_layers, d, n = w_mod.shape
    nb = 1536
    return pl.pallas_call(
        _mod_kernel,
        grid=(n_layers, n // nb),
        in_specs=[
            pl.BlockSpec((8, d), lambda l, j: (0, 0)),
            pl.BlockSpec((1, d, nb), lambda l, j: (l, 0, j)),
            pl.BlockSpec((1, 1, nb), lambda l, j: (l, 0, j)),
        ],
        out_specs=pl.BlockSpec((1, 8, nb), lambda l, j: (l, 0, j)),
        out_shape=jax.ShapeDtypeStruct((n_layers, 8, n), F32),
        compiler_params=_cparams(2, 40),
        name="adaln_modulation",
    )(cond8, w_mod, b_mod.reshape(n_layers, 1, n))


def _rope128(x, c, s1, s2):
    return x * c + pltpu.roll(x, V7X_LANES - 16, 1) * s1 + pltpu.roll(x, 16, 1) * s2


def _inproj_kernel(x_ref, cx_ref, sh_ref, a_ref, csh_ref, ca_ref, w_ref, qg_ref, kg_ref, c_ref, s1_ref, s2_ref,
                   gmq_ref, gmk_ref, q_ref, k_ref, vt_ref, hq_ref, hf_ref, hi_ref, hg_ref, ut_ref, gate_ref, *, n_lat):
    is_ctx = pl.program_id(1) >= n_lat
    x = jnp.where(is_ctx, cx_ref[0], x_ref[0])
    a = jnp.where(is_ctx, ca_ref[0], a_ref[0])
    sh = jnp.where(is_ctx, csh_ref[0], sh_ref[0])
    ms = jnp.mean(x * x, axis=-1, keepdims=True)
    h = (x * lax.rsqrt(ms + RMS_EPS)) * a + sh
    hb = h.astype(BF16)

    def proj(lo, hi):
        return _dot(hb, w_ref[:, lo:hi])

    c, s1, s2 = c_ref[...], s1_ref[...], s2_ref[...]

    zq = proj(O_Q, O_K)
    msq = _dot((zq * zq).astype(BF16), gmq_ref[...])
    qn = zq * lax.rsqrt(msq + RMS_EPS) * qg_ref[...]
    for j in range(ATT_WIDTH // V7X_LANES):
        sl = slice(j * V7X_LANES, (j + 1) * V7X_LANES)
        q_ref[0, :, sl] = _rope128(qn[:, sl], c, s1, s2).astype(BF16)

    zk = proj(O_K, O_V)
    msk = _dot((zk * zk).astype(BF16), gmk_ref[...])
    kn = _rope128(zk * lax.rsqrt(msk + RMS_EPS) * kg_ref[...], c, s1, s2)
    vt = proj(O_V, O_HQ).T
    tm = x.shape[0]
    ones_row = lax.broadcasted_iota(jnp.int32, (ATTN_V_ROWS - HEAD_DIM, tm), 0) == 0
    for hd in range(ATT_KV_HEADS):
        k_ref[0, hd] = kn[:, hd * HEAD_DIM:(hd + 1) * HEAD_DIM].astype(BF16)
        vt_ref[0, hd, 0:HEAD_DIM, :] = vt[hd * HEAD_DIM:(hd + 1) * HEAD_DIM, :].astype(BF16)
        vt_ref[0, hd, HEAD_DIM:ATTN_V_ROWS, :] = jnp.where(ones_row, 1.0, 0.0).astype(BF16)

    hq_ref[0] = proj(O_HQ, O_HF).astype(BF16)
    hf_ref[0] = proj(O_HF, O_HI)
    hi_ref[0] = proj(O_HI, O_HG).astype(BF16)
    hg_ref[0] = proj(O_HG, O_U).astype(BF16)
    ut_ref[0] = proj(O_U, O_GATE).T
    for j in range(3):
        lo = O_GATE + j * D_MODEL
        gate_ref[0, :, j * D_MODEL:(j + 1) * D_MODEL] = _sigmoid(proj(lo, lo + D_MODEL)).astype(BF16)


def _inproj(x, cx, sh, a, csh, ca, w_in, qg, kg, rope, gmq, gmk):
    b, t, d = x.shape
    t_ctx = cx.shape[1]
    tm = INPROJ_TILE
    n_lat, n_ctx = t // tm, t_ctx // tm
    t_all = t + t_ctx
    c, s1, s2 = rope
    row = lambda bi, i: (bi, i, 0)
    vec = lambda bi, i: (bi, 0, 0)
    tab = lambda bi, i: (i, 0)

    def widths(specs):
        shapes = [jax.ShapeDtypeStruct((b, t_all, w), dt) for w, dt in specs]
        return shapes, [pl.BlockSpec((1, tm, w), row) for w, _ in specs]

    q_shape, q_spec = widths([(ATT_WIDTH, BF16)])
    h_shape, h_spec = widths([(HG_WIDTH, BF16), (2 * HG_WIDTH, F32), (HG_WIDTH, BF16), (HG_WIDTH, BF16)])
    g_shape, g_spec = widths([(3 * D_MODEL, BF16)])
    out_shape = q_shape + [
        jax.ShapeDtypeStruct((b, ATT_KV_HEADS, t_all, HEAD_DIM), BF16),
        jax.ShapeDtypeStruct((b, ATT_KV_HEADS, ATTN_V_ROWS, t_all), BF16),
    ] + h_shape + [jax.ShapeDtypeStruct((b, SSM_WIDTH, t_all), F32)] + g_shape
    out_specs = q_spec + [
        pl.BlockSpec((1, ATT_KV_HEADS, tm, HEAD_DIM), lambda bi, i: (bi, 0, i, 0)),
        pl.BlockSpec((1, ATT_KV_HEADS, ATTN_V_ROWS, tm), lambda bi, i: (bi, 0, 0, i)),
    ] + h_spec + [pl.BlockSpec((1, SSM_WIDTH, tm), lambda bi, i: (bi, 0, i))] + g_spec
    return pl.pallas_call(
        functools.partial(_inproj_kernel, n_lat=n_lat),
        grid=(b, n_lat + n_ctx),
        in_specs=[
            pl.BlockSpec((1, tm, d), lambda bi, i: (bi, jnp.minimum(i, n_lat - 1), 0)),
            pl.BlockSpec((1, tm, d), lambda bi, i: (bi, jnp.maximum(i - n_lat, 0), 0)),
            pl.BlockSpec((1, 1, d), vec),
            pl.BlockSpec((1, 1, d), vec),
            pl.BlockSpec((1, 1, d), lambda bi, i: (0, 0, 0)),
            pl.BlockSpec((1, 1, d), lambda bi, i: (0, 0, 0)),
            _const_spec((d, N_IN)),
            _const_spec((1, ATT_WIDTH)),
            _const_spec((1, ATT_KV_WIDTH)),
            pl.BlockSpec((tm, V7X_LANES), tab),
            pl.BlockSpec((tm, V7X_LANES), tab),
            pl.BlockSpec((tm, V7X_LANES), tab),
            _const_spec((ATT_WIDTH, ATT_WIDTH)),
            _const_spec((ATT_KV_WIDTH, ATT_KV_WIDTH)),
        ],
        out_specs=out_specs,
        out_shape=out_shape,
        compiler_params=_cparams(2, 56),
        name="in_projection",
    )(x, cx, sh, a, csh, ca, w_in, qg, kg, c, s1, s2, gmq, gmk)


def _attn_kernel(q_ref, k_ref, vt_ref, o_ref, qt_ref, m_ref, acc_ref, sa_ref, sb_ref, mxa_ref, mxb_ref, *, tk, nkb):
    tq = q_ref.shape[1]
    qt = q_ref[0].astype(F32).T
    qt_ref[...] = jnp.concatenate(
        [qt[HEAD_DIM * g:HEAD_DIM * (g + 1), :] for g in range(ATT_GROUP)], axis=1).astype(BF16)
    m_ref[...] = jnp.full(m_ref.shape, -jnp.inf, F32)
    acc_ref[...] = jnp.zeros(acc_ref.shape, F32)

    def scores(kb, s_ref, mx_ref):
        off = pl.multiple_of(kb * tk, tk)
        s = _dot(k_ref[0, 0, pl.ds(off, tk), :], qt_ref[...])
        s_ref[...] = s
        mx_ref[...] = jnp.max(s, axis=0, keepdims=True)

    def consume(kb, s_ref, mx_ref):
        off = pl.multiple_of(kb * tk, tk)
        m_prev = m_ref[...]
        m_new = jnp.maximum(m_prev, mx_ref[...])
        alpha = jnp.exp2(m_prev - m_new)
        p = jnp.exp2(s_ref[...] - m_new).astype(BF16)
        acc_ref[...] = alpha * acc_ref[...] + _dot(vt_ref[0, 0, :, pl.ds(off, tk)], p)
        m_ref[...] = m_new

    bufs = ((sa_ref, mxa_ref), (sb_ref, mxb_ref))
    scores(0, *bufs[0])

    def body(i, carry):
        base = ATTN_BLOCKS_PER_ITER * i
        for j in range(ATTN_BLOCKS_PER_ITER):
            scores(base + j + 1, *bufs[(j + 1) % 2])
            consume(base + j, *bufs[j % 2])
        return carry

    n_it = (nkb - 1) // ATTN_BLOCKS_PER_ITER
    lax.fori_loop(0, n_it, body, 0)
    for r in range(n_it * ATTN_BLOCKS_PER_ITER, nkb):
        if r + 1 < nkb:
            scores(r + 1, *bufs[(r + 1) % 2])
        consume(r, *bufs[r % 2])

    acc = acc_ref[...]
    out_t = acc[0:HEAD_DIM, :] / acc[HEAD_DIM:HEAD_DIM + 1, :]
    out_t = jnp.concatenate([out_t[:, g * tq:(g + 1) * tq] for g in range(ATT_GROUP)], axis=0)
    o_ref[0] = out_t.T.astype(BF16)


ATTN_KEY_BLOCKS = (640, 512, 256, 128)
ATTN_BLOCKS_PER_ITER = 6


def _attention(q, k, vt_ext, q_start, q_len, k_start, k_len):
    b = q.shape[0]
    tq = ATTN_Q_TILE
    tk = next(c for c in ATTN_KEY_BLOCKS if k_len % c == 0)
    assert q_start % tq == 0 and q_len % tq == 0 and k_start % k_len == 0
    q_off, k_blk = q_start // tq, k_start // k_len
    kern = functools.partial(_attn_kernel, tk=tk, nkb=k_len // tk)
    gw = ATT_GROUP * HEAD_DIM
    m = ATT_GROUP * tq
    return pl.pallas_call(
        kern,
        grid=(b, ATT_KV_HEADS, q_len // tq),
        in_specs=[
            pl.BlockSpec((1, tq, gw), lambda bi, h, i: (bi, i + q_off, h)),
            pl.BlockSpec((1, 1, k_len, HEAD_DIM), lambda bi, h, i: (bi, h, k_blk, 0)),
            pl.BlockSpec((1, 1, ATTN_V_ROWS, k_len), lambda bi, h, i: (bi, h, 0, k_blk)),
        ],
        out_specs=pl.BlockSpec((1, tq, gw), lambda bi, h, i: (bi, i, h)),
        out_shape=jax.ShapeDtypeStruct((b, q_len, ATT_WIDTH), BF16),
        scratch_shapes=[
            pltpu.VMEM((HEAD_DIM, m), BF16),
            pltpu.VMEM((1, m), F32),
            pltpu.VMEM((ATTN_V_ROWS, m), F32),
            pltpu.VMEM((tk, m), F32),
            pltpu.VMEM((tk, m), F32),
            pltpu.VMEM((1, m), F32),
            pltpu.VMEM((1, m), F32),
        ],
        compiler_params=_cparams(3, 48),
        name="gqa_attention",
    )(q, k, vt_ext)


HG_LEVELS = 7
HG_SEL_LEVELS = 3


def _hgrn_kernel(qf_ref, ff_ref, vf_ref, qb_ref, fb_ref, vb_ref, lb_ref, s0_ref,
                 tri_ref, sel_ref, sm_ref, hm_ref, gsum_ref, bd_ref,
                 of_ref, ob_ref, sfin_ref, stf_ref, stb_ref, cf_s, cb_s):
    @pl.when(pl.program_id(1) == 0)
    def _():
        stf_ref[...] = s0_ref[0, 0]
        stb_ref[...] = s0_ref[0, 1]

    tt, w = HG_TILE, HG_WIDTH
    dirs = (0, 1)
    q_refs, f_refs, v_refs = (qf_ref, qb_ref), (ff_ref, fb_ref), (vf_ref, vb_ref)
    o_refs, st_refs, c_refs = (of_ref, ob_ref), (stf_ref, stb_ref), (cf_s, cb_s)
    hm = [hm_ref[h] for h in range(HG_HEADS)]

    qs, kin, vb, c, tot, ref_small, o, scores = [], [], [], [], [], [], [], [None, None]
    for d in dirs:
        q = q_refs[d][0].astype(F32)
        fpre = f_refs[d][0]
        lb = lb_ref[d:d + 1, :]
        qs.append(q * _sigmoid(q))
        kin.append((1.0 - lb) * _sigmoid(-fpre))
        vb.append(v_refs[d][0])
        hi, lo = _split(jnp.log(lb + (1.0 - lb) * _sigmoid(fpre)))
        cd = (_dot(tri_ref[d], hi) + _dot(tri_ref[d], lo)) * LOG2E
        c_refs[d][...] = cd
        c.append(cd)
    for d in dirs:
        last = 0 if d else tt - 1
        tot.append(c_refs[d][last:last + 1, :])
        chi, clo = _split(c[d])
        ref_small.append(_dot(sel_ref[d], chi) + _dot(sel_ref[d], clo))
        o.append(_dot((qs[d] * kin[d]).astype(BF16), gsum_ref[...]) * vb[d].astype(F32))

    for lvl in range(HG_LEVELS):
        m = 1 << lvl
        for d in dirs:
            if lvl < HG_SEL_LEVELS:
                cref = ref_small[d][lvl * tt:(lvl + 1) * tt, :]
            else:
                rows = []
                for blk in range(tt // (2 * m)):
                    r = blk * 2 * m + (m if d else m - 1)
                    rows.append(jnp.broadcast_to(c_refs[d][r:r + 1, :], (2 * m, w)))
                cref = rows[0] if len(rows) == 1 else jnp.concatenate(rows, axis=0)
            wgt = jnp.exp2(-jnp.abs(c[d] - cref))
            ql = (qs[d] * wgt).astype(BF16)
            kl = (kin[d] * wgt).astype(BF16)
            kstack = jnp.concatenate([kl * hm[h] for h in range(HG_HEADS)], axis=0)
            sc = _dot_nt(ql, kstack).astype(BF16) * sm_ref[d, lvl]
            scores[d] = sc if scores[d] is None else scores[d] + sc

    for d in dirs:
        vstack = jnp.concatenate([vb[d] * hm[h] for h in range(HG_HEADS)], axis=0)
        o[d] = o[d] + _dot(scores[d], vstack)
    for d in dirs:
        st = st_refs[d][...]
        o[d] = o[d] + _dot_nt((qs[d] * jnp.exp2(c[d])).astype(BF16), st.astype(BF16))
        kv = _dot_tn(vb[d], (kin[d] * jnp.exp2(tot[d] - c[d])).astype(BF16))
        st_new = jnp.exp2(tot[d]) * st + kv * bd_ref[...]
        st_refs[d][...] = st_new
        sfin_ref[0, d] = st_new
        o_refs[d][0] = o[d].astype(BF16)


def _hgrn_constants():
    tt, w = HG_TILE, HG_WIDTH
    t = np.arange(tt)
    tri = np.stack([t[None, :] <= t[:, None], t[None, :] >= t[:, None]]).astype(BF16)
    sel = []
    for reverse in (False, True):
        per = []
        for lvl in range(HG_SEL_LEVELS):
            m = 1 << lvl
            r = (t // (2 * m)) * (2 * m) + (m if reverse else m - 1)
            per.append(t[None, :] == r[:, None])
        sel.append(np.concatenate(per, axis=0))
    sel = np.stack(sel).astype(BF16)
    col = np.arange(HG_HEADS * tt) % tt
    sm = []
    for reverse in (False, True):
        per = []
        for lvl in range(HG_LEVELS):
            t_up, s_up = ((t >> lvl) & 1) == 1, ((col >> lvl) & 1) == 1
            same = (t[:, None] >> (lvl + 1)) == (col[None, :] >> (lvl + 1))
            halves = (~t_up[:, None] & s_up[None, :]) if reverse else (t_up[:, None] & ~s_up[None, :])
            per.append(same & halves)
        sm.append(np.stack(per))
    sm = np.stack(sm).astype(BF16)
    lane_head = np.arange(w) // HG_DK
    hm = np.stack([np.broadcast_to((lane_head == h)[None, :], (tt, w)) for h in range(HG_HEADS)]).astype(BF16)
    same_head = lane_head[:, None] == lane_head[None, :]
    return tri, sel, sm, hm, same_head.astype(BF16), same_head.astype(F32)


def _hgrn(hq, hf, hv, lb2, s0, consts, start, t):
    b, _, w = hq.shape
    tt = HG_TILE
    nt, off = t // tt, start // tt
    fwd = lambda bi, i: (bi, off + i, 0)
    bwd = lambda bi, i: (bi, off + nt - 1 - i, 0)
    bwd_f = lambda bi, i: (bi, off + nt - 1 - i, 1)
    fwd_o = lambda bi, i: (bi, i, 0)
    bwd_o = lambda bi, i: (bi, nt - 1 - i, 0)
    st_spec = pl.BlockSpec((1, 2, w, w), lambda bi, i: (bi, 0, 0, 0))
    return pl.pallas_call(
        _hgrn_kernel,
        grid=(b, nt),
        in_specs=[
            pl.BlockSpec((1, tt, w), fwd), pl.BlockSpec((1, tt, w), fwd), pl.BlockSpec((1, tt, w), fwd),
            pl.BlockSpec((1, tt, w), bwd), pl.BlockSpec((1, tt, w), bwd_f), pl.BlockSpec((1, tt, w), bwd),
            pl.BlockSpec((2, w), lambda bi, i: (0, 0)),
            st_spec,
        ] + [_const_spec(a.shape) for a in consts],
        out_specs=[pl.BlockSpec((1, tt, w), fwd_o), pl.BlockSpec((1, tt, w), bwd_o), st_spec],
        out_shape=[
            jax.ShapeDtypeStruct((b, t, w), BF16),
            jax.ShapeDtypeStruct((b, t, w), BF16),
            jax.ShapeDtypeStruct((b, 2, w, w), F32),
        ],
        scratch_shapes=[
            pltpu.VMEM((w, w), F32),
            pltpu.VMEM((w, w), F32),
            pltpu.VMEM((tt, w), F32),
            pltpu.VMEM((tt, w), F32),
        ],
        compiler_params=_cparams(2, 32),
        name="hgrn2_scan",
    )(hq, hf, hv, hq, hf, hv, lb2, s0, *consts)


def _cpow(a_re, a_im, tau):
    mag = jnp.exp(tau * a_re)
    ang = tau * a_im
    return mag * jnp.cos(ang), mag * jnp.sin(ang)


def _s5_prep_kernel(ar_row, ai_row, ar_col, ai_col, lr_row, li_row, bt_r, bt_i, cr, ci, ctr, cti,
                    w_ref, wst_ref, wout_ref, laml_ref, kall_ref):
    lc, p, c = S5_CHUNK, SSM_STATE, SSM_GROUP
    tau_l = lax.broadcasted_iota(jnp.int32, (p, lc), 1).astype(F32)
    tau_s = lax.broadcasted_iota(jnp.int32, (lc, p), 0).astype(F32)

    btr, bti = [], []
    for d in range(2):
        e_r, e_i = _cpow(ar_row[0, d], ai_row[0, d], 1.0)
        l_r, l_i = lr_row[0, d], li_row[0, d]
        den = l_r * l_r + l_i * l_i
        f_r = ((e_r - 1.0) * l_r + e_i * l_i) / den
        f_i = (e_i * l_r - (e_r - 1.0) * l_i) / den
        btr.append(f_r * bt_r[0, d] - f_i * bt_i[0, d])
        bti.append(f_r * bt_i[0, d] + f_i * bt_r[0, d])

    def cb(d):
        re, im = [], []
        for c1 in range(c):
            b_r, b_i = btr[d][c1:c1 + 1, :], bti[d][c1:c1 + 1, :]
            re.append(b_r * cr[0, d] - b_i * ci[0, d])
            im.append(b_r * ci[0, d] + b_i * cr[0, d])
        return jnp.concatenate(re, axis=0), jnp.concatenate(im, axis=0)

    cbf_r, cbf_i = cb(0)
    pf_r, pf_i = _cpow(ar_col[0, 0], ai_col[0, 0], tau_l)
    kf = _dot3(cbf_r, pf_r) - _dot3(cbf_i, pf_i)
    cbb_r, cbb_i = cb(1)
    pb_r, pb_i = _cpow(ar_col[0, 1], ai_col[0, 1], lc - tau_l)
    kb = _dot3(cbb_r, pb_r) - _dot3(cbb_i, pb_i)
    lane = lax.broadcasted_iota(jnp.int32, (c * c, lc), 1)
    kf = kf + jnp.where(lane == 0, jnp.sum(cbb_r, axis=1, keepdims=True), 0.0)
    kall_ref[...] = jnp.concatenate([kf, kb], axis=1)

    def toeplitz_rows(c1, carry):
        for c2 in range(c):
            row = kall_ref[pl.ds(c1 * c + c2, 1), :]
            blk = pltpu.roll(jnp.broadcast_to(row, (lc, 2 * lc)), 0, 1, stride=1, stride_axis=0)
            w_ref[0, pl.ds(pl.multiple_of(c1 * lc, lc), lc), c2 * lc:(c2 + 1) * lc] = blk[:, :lc].astype(BF16)
        return carry

    lax.fori_loop(0, c, toeplitz_rows, 0)

    sf_r, sf_i = _cpow(ar_row[0, 0], ai_row[0, 0], (lc - 1) - tau_s)
    sb_r, sb_i = _cpow(ar_row[0, 1], ai_row[0, 1], tau_s)
    for c1 in range(c):
        re, im = [], []
        for d, (p_r, p_i) in enumerate(((sf_r, sf_i), (sb_r, sb_i))):
            b_r, b_i = btr[d][c1:c1 + 1, :], bti[d][c1:c1 + 1, :]
            re.append(p_r * b_r - p_i * b_i)
            im.append(p_r * b_i + p_i * b_r)
        wst_ref[0, c1 * lc:(c1 + 1) * lc, :] = jnp.concatenate(re + im, axis=1).astype(BF16)

    of_r, of_i = _cpow(ar_col[0, 0], ai_col[0, 0], tau_l + 1.0)
    ob_r, ob_i = _cpow(ar_col[0, 1], ai_col[0, 1], lc - tau_l)
    for c2 in range(c):
        re, im = [], []
        for d, (p_r, p_i) in enumerate(((of_r, of_i), (ob_r, ob_i))):
            c_r, c_i = ctr[0, d, :, c2:c2 + 1], cti[0, d, :, c2:c2 + 1]
            re.append(c_r * p_r - c_i * p_i)
            im.append(-(c_r * p_i + c_i * p_r))
        wout_ref[0, :, c2 * lc:(c2 + 1) * lc] = jnp.concatenate(re + im, axis=0).astype(BF16)

    lf_r, lf_i = _cpow(ar_row[0, 0], ai_row[0, 0], float(lc))
    lb_r, lb_i = _cpow(ar_row[0, 1], ai_row[0, 1], float(lc))
    laml_ref[0, 0:1, :] = jnp.concatenate([lf_r, lb_r], axis=1)
    laml_ref[0, 1:2, :] = jnp.concatenate([lf_i, lb_i], axis=1)


def _s5_prep(a_re, a_im, lam_re, lam_im, b_re, b_im, c_re, c_im):
    g, _, p = a_re.shape
    c, lc = SSM_GROUP, S5_CHUNK
    n = c * lc
    row = lambda x: x.reshape(g, 2, 1, p)
    col = lambda x: x.reshape(g, 2, p, 1)
    tr = lambda x: jnp.swapaxes(x, -1, -2)
    args = [row(a_re), row(a_im), col(a_re), col(a_im), row(lam_re), row(lam_im),
            tr(b_re), tr(b_im), c_re, c_im, tr(c_re), tr(c_im)]
    spec4 = lambda shp: pl.BlockSpec((1,) + shp, lambda gi: (gi, 0, 0, 0))
    spec3 = lambda shp: pl.BlockSpec((1,) + shp, lambda gi: (gi, 0, 0))
    return pl.pallas_call(
        _s5_prep_kernel,
        grid=(g,),
        in_specs=[spec4(a.shape[1:]) for a in args],
        out_specs=[spec3((n, n)), spec3((n, 4 * p)), spec3((4 * p, n)), spec3((2, 2 * p))],
        out_shape=[
            jax.ShapeDtypeStruct((g, n, n), BF16),
            jax.ShapeDtypeStruct((g, n, 4 * p), BF16),
            jax.ShapeDtypeStruct((g, 4 * p, n), BF16),
            jax.ShapeDtypeStruct((g, 2, 2 * p), F32),
        ],
        scratch_shapes=[pltpu.VMEM((c * c, 2 * lc), F32)],
        compiler_params=_cparams(1, 48),
        name="s5_weights",
    )(*args)


def _s5_kernel(u_ref, w_ref, wst_ref, wout_ref, laml_ref, dsk_ref, y_ref, xloc_s, xin_s, *, nctx, nck):
    c, lc, p, p2 = SSM_GROUP, S5_CHUNK, SSM_STATE, 2 * SSM_STATE
    ub = jnp.concatenate([u_ref[0, c1] for c1 in range(c)], axis=1).astype(BF16)
    xloc_s[...] = _dot(ub, wst_ref[0])

    nlat = nck - nctx
    order_f = list(range(nlat, nck)) + list(range(nlat))
    order_b = list(range(nck - 1, nlat - 1, -1)) + list(range(nlat - 1, -1, -1))
    m_r, m_i = laml_ref[0, 0:1, :], laml_ref[0, 1:2, :]
    is_fwd = lax.broadcasted_iota(jnp.int32, (1, p2), 1) < p
    x_r = jnp.zeros((1, p2), F32)
    x_i = jnp.zeros((1, p2), F32)
    for kf, kb in zip(order_f, order_b):
        xin_s[kf:kf + 1, 0:p] = x_r[:, 0:p]
        xin_s[kb:kb + 1, p:p2] = x_r[:, p:p2]
        xin_s[kf:kf + 1, p2:p2 + p] = x_i[:, 0:p]
        xin_s[kb:kb + 1, p2 + p:2 * p2] = x_i[:, p:p2]
        loc_r = jnp.where(is_fwd, xloc_s[kf:kf + 1, 0:p2], xloc_s[kb:kb + 1, 0:p2])
        loc_i = jnp.where(is_fwd, xloc_s[kf:kf + 1, p2:2 * p2], xloc_s[kb:kb + 1, p2:2 * p2])
        x_r, x_i = m_r * x_r - m_i * x_i + loc_r, m_r * x_i + m_i * x_r + loc_i

    y = _dot(ub, w_ref[0]) + _dot(xin_s[...].astype(BF16), wout_ref[0])
    for c2 in range(c):
        y_ref[0, c2] = y[:, c2 * lc:(c2 + 1) * lc] + dsk_ref[0, c2:c2 + 1, :] * u_ref[0, c2]


def _s5(u_t, w, wst, wout, laml, dsk, nctx):
    b, wd, nck, lc = u_t.shape
    g, c = SSM_GROUPS, SSM_GROUP
    n = c * lc
    kern = functools.partial(_s5_kernel, nctx=nctx, nck=nck)
    gspec = lambda shp: pl.BlockSpec((1,) + shp, lambda gi, bi: (gi,) + (0,) * len(shp))
    io_spec = pl.BlockSpec((1, c, nck, lc), lambda gi, bi: (bi, gi, 0, 0))
    return pl.pallas_call(
        kern,
        grid=(g, b),
        in_specs=[io_spec, gspec((n, n)), gspec((n, 4 * SSM_STATE)), gspec((4 * SSM_STATE, n)),
                  gspec((2, 2 * SSM_STATE)), gspec((c, lc))],
        out_specs=io_spec,
        out_shape=jax.ShapeDtypeStruct(u_t.shape, F32),
        scratch_shapes=[
            pltpu.VMEM((nck, 4 * SSM_STATE), F32),
            pltpu.VMEM((nck, 4 * SSM_STATE), F32),
        ],
        compiler_params=_cparams(2, 48),
        name="s5_scan",
    )(u_t, w, wst, wout, laml, dsk)


def _merge_kernel(x_ref, att_ref, of_ref, ob_ref, hg_ref, yt_ref, gate_ref, g1_ref,
                  wa_ref, wr_ref, ws_ref, wo_ref, wglu_ref, bglu_ref, hn_ref, gm_ref, o_ref):
    r = of_ref[0].astype(F32) + ob_ref[0].astype(F32)
    ms = _dot((r * r).astype(BF16), gm_ref[...])
    g = hg_ref[0].astype(F32)
    yrec = (r * lax.rsqrt(ms + RMS_EPS) * hn_ref[...]) * (g * _sigmoid(g))

    ys = yt_ref[0].T
    z = 0.5 * ys * (1.0 + jnp.tanh(math.sqrt(2.0 / math.pi) * (ys + 0.044715 * (ys * ys * ys))))
    yssm = z * _sigmoid(_dot(z.astype(BF16), wglu_ref[...]) + bglu_ref[...])

    d = D_MODEL
    m = gate_ref[0, :, 0:d].astype(F32) * _dot(att_ref[0], wa_ref[...])
    m = m + gate_ref[0, :, d:2 * d].astype(F32) * _dot(yrec.astype(BF16), wr_ref[...])
    m = m + gate_ref[0, :, 2 * d:3 * d].astype(F32) * _dot(yssm.astype(BF16), ws_ref[...])
    y = _dot(m.astype(BF16), wo_ref[...])
    o_ref[0] = x_ref[0] + g1_ref[0] * y


def _merge(x, att, o_f, o_b, hg, y_t, t_off, gates, g1, wa, wr, ws, wo, wglu, bglu, hn, gm):
    b, t, d = x.shape
    tm = min(512, t)
    off = t_off // tm
    row = lambda bi, i: (bi, i, 0)
    row_off = lambda bi, i: (bi, i + off, 0)
    vec = lambda bi, i: (bi, 0, 0)
    return pl.pallas_call(
        _merge_kernel,
        grid=(b, t // tm),
        in_specs=[
            pl.BlockSpec((1, tm, d), row),
            pl.BlockSpec((1, tm, ATT_WIDTH), row),
            pl.BlockSpec((1, tm, HG_WIDTH), row),
            pl.BlockSpec((1, tm, HG_WIDTH), row),
            pl.BlockSpec((1, tm, HG_WIDTH), row_off),
            pl.BlockSpec((1, SSM_WIDTH, tm), lambda bi, i: (bi, 0, i + off)),
            pl.BlockSpec((1, tm, 3 * d), row_off),
            pl.BlockSpec((1, 1, d), vec),
            _const_spec(wa.shape), _const_spec(wr.shape), _const_spec(ws.shape), _const_spec(wo.shape),
            _const_spec(wglu.shape), _const_spec(bglu.shape), _const_spec(hn.shape), _const_spec(gm.shape),
        ],
        out_specs=pl.BlockSpec((1, tm, d), row),
        out_shape=jax.ShapeDtypeStruct((b, t, d), F32),
        compiler_params=_cparams(2, 48),
        name="merge_branches",
    )(x, att, o_f, o_b, hg, y_t, gates, g1, wa, wr, ws, wo, wglu, bglu, hn, gm)


def _ffn_kernel(x_ref, sh_ref, a_ref, g_ref, wup_ref, wdn_ref, o_ref, *, nj):
    x = x_ref[0]
    ms = jnp.mean(x * x, axis=-1, keepdims=True)
    hb = ((x * lax.rsqrt(ms + RMS_EPS)) * a_ref[0] + sh_ref[0]).astype(BF16)
    f = FFN_HIDDEN
    fc = f // nj
    acc = None
    for j in range(nj):
        a = _dot(hb, wup_ref[:, j * fc:(j + 1) * fc])
        bgate = _dot(hb, wup_ref[:, f + j * fc:f + (j + 1) * fc])
        act = ((a * _sigmoid(a)) * bgate).astype(BF16)
        part = _dot(act, wdn_ref[j * fc:(j + 1) * fc, :])
        acc = part if acc is None else acc + part
    o_ref[0] = x + g_ref[0] * acc


def _ffn(x, sh, a, g, wup, wdn):
    b, t, d = x.shape
    tm = min(512, t)
    row = lambda bi, i: (bi, i, 0)
    vec = lambda bi, i: (bi, 0, 0)
    return pl.pallas_call(
        functools.partial(_ffn_kernel, nj=2),
        grid=(b, t // tm),
        in_specs=[
            pl.BlockSpec((1, tm, d), row),
            pl.BlockSpec((1, 1, d), vec), pl.BlockSpec((1, 1, d), vec), pl.BlockSpec((1, 1, d), vec),
            _const_spec(wup.shape), _const_spec(wdn.shape),
        ],
        out_specs=pl.BlockSpec((1, tm, d), row),
        out_shape=jax.ShapeDtypeStruct((b, t, d), F32),
        compiler_params=_cparams(2, 56),
        name="swiglu_ffn",
    )(x, sh, a, g, wup, wdn)


def _rope_tables(t, t_ctx):
    pos = jnp.arange(t)
    row = (pos // GRID_W).astype(F32)
    col = (pos % GRID_W).astype(F32)
    axis_dim = HEAD_DIM // 2
    inv = ROPE_THETA ** (-jnp.arange(0, axis_dim, 2, dtype=F32) / axis_dim)
    ang_r, ang_c = row[:, None] * inv, col[:, None] * inv
    cr, sr, cc, sc = jnp.cos(ang_r), jnp.sin(ang_r), jnp.cos(ang_c), jnp.sin(ang_c)
    z = jnp.zeros_like(cr)
    rep = V7X_LANES // HEAD_DIM
    c = jnp.tile(jnp.concatenate([cr, cr, cc, cc], axis=1), (1, rep))
    s1 = jnp.tile(jnp.concatenate([-sr, z, -sc, z], axis=1), (1, rep))
    s2 = jnp.tile(jnp.concatenate([z, sr, z, sc], axis=1), (1, rep))
    pad = lambda a, fill: jnp.concatenate([a, jnp.full((t_ctx, V7X_LANES), fill, F32)], axis=0)
    return pad(c, 1.0), pad(s1, 0.0), pad(s2, 0.0)


def _group_mean_matrix(width, group):
    i = np.arange(width) // group
    return np.where(i[:, None] == i[None, :], 1.0 / group, 0.0).astype(BF16)


def kernel(x, c, ctx, c_ctx, w_mod, b_mod, norm1_g, norm2_g, w_in, q_norm_g, k_norm_g, hgrn_lb, hgrn_norm_g,
           ssm_lam_re, ssm_lam_im, ssm_log_dt, ssm_b_re, ssm_b_im, ssm_c_re, ssm_c_im, ssm_d, w_glu, b_glu,
           w_br_attn, w_br_hgrn, w_br_ssm, w_out, w_ffn_up, w_ffn_down):
    bsz, t, d = x.shape
    t_ctx = ctx.shape[1]
    t_all = t + t_ctx
    depth = w_mod.shape[0]
    assert t % 512 == 0 and t_ctx % INPROJ_TILE == 0 and t % t_ctx == 0 and d == D_MODEL

    lb_soft = jax.nn.softmax(hgrn_lb.astype(F32), axis=0)
    lower_bounds = jnp.cumsum(lb_soft, axis=0) - lb_soft[0]
    rope = _rope_tables(t, t_ctx)
    gmq = _group_mean_matrix(ATT_WIDTH, HEAD_DIM)
    gmk = _group_mean_matrix(ATT_KV_WIDTH, HEAD_DIM)
    gmh = _group_mean_matrix(HG_WIDTH, HG_DK)
    hg_consts = _hgrn_constants()
    cond8 = jnp.zeros((8, d), F32).at[:bsz].set(c).at[bsz].set(c_ctx)
    mods = _modulation(cond8, w_mod, b_mod)

    x_lat, x_ctx = x, ctx
    for l in range(depth):
        with_ctx = l < depth - 1
        ml = mods[l, :bsz].reshape(bsz, ADALN_CHUNKS, 1, d)
        mc = mods[l, bsz].reshape(1, ADALN_CHUNKS, 1, d)
        sh1, sc1, g1, sh2, sc2, g2 = [ml[:, i] for i in range(ADALN_CHUNKS)]
        csh1, csc1, cg1, csh2, csc2, cg2 = [mc[:, i] for i in range(ADALN_CHUNKS)]
        n1, n2 = norm1_g[l].reshape(1, 1, d), norm2_g[l].reshape(1, 1, d)

        qg = (jnp.tile(q_norm_g[l], ATT_HEADS) * (HEAD_DIM ** -0.5 * LOG2E)).reshape(1, ATT_WIDTH)
        kg = jnp.tile(k_norm_g[l], ATT_KV_HEADS).reshape(1, ATT_KV_WIDTH)
        q, k, vt, hq, hf, hi, hg, u_t, gates = _inproj(
            x_lat, x_ctx, sh1, n1 * (1.0 + sc1), csh1, n1 * (1.0 + csc1), w_in[l].astype(BF16), qg, kg, rope, gmq, gmk)

        a_lat = _attention(q, k, vt, 0, t, 0, t_all)
        if with_ctx:
            a_ctx = _attention(q, k, vt, t, t_ctx, t, t_ctx)

        lb2 = lower_bounds[l]
        s_zero = jnp.zeros((bsz, 2, HG_WIDTH, HG_WIDTH), F32)
        of_c, ob_c, s_ctx = _hgrn(hq, hf, hi, lb2, s_zero, hg_consts, t, t_ctx)
        of_l, ob_l, _ = _hgrn(hq, hf, hi, lb2, s_ctx, hg_consts, 0, t)

        dt = jnp.exp(ssm_log_dt[l].astype(F32))[..., None]
        gd = lambda a: jnp.swapaxes(a.astype(F32), 0, 1)
        w_s5, wst, wout, laml = _s5_prep(
            gd(ssm_lam_re[l] * dt), gd(ssm_lam_im[l] * dt), gd(ssm_lam_re[l]), gd(ssm_lam_im[l]),
            gd(ssm_b_re[l]), gd(ssm_b_im[l]), gd(ssm_c_re[l]), gd(ssm_c_im[l]))
        dsk = jnp.broadcast_to(ssm_d[l].astype(F32).reshape(SSM_GROUPS, SSM_GROUP, 1), (SSM_GROUPS, SSM_GROUP, S5_CHUNK))
        y_t = _s5(u_t.reshape(bsz, SSM_WIDTH, t_all // S5_CHUNK, S5_CHUNK), w_s5, wst, wout, laml, dsk,
                  t_ctx // S5_CHUNK).reshape(bsz, SSM_WIDTH, t_all)

        hn = jnp.tile(hgrn_norm_g[l], HG_HEADS).reshape(1, HG_WIDTH)
        mw = (w_br_attn[l].astype(BF16), w_br_hgrn[l].astype(BF16), w_br_ssm[l].astype(BF16), w_out[l].astype(BF16),
              w_glu[l].astype(BF16), b_glu[l].reshape(1, SSM_WIDTH), hn, gmh)
        wup, wdn = w_ffn_up[l].astype(BF16), w_ffn_down[l].astype(BF16)
        x_lat = _merge(x_lat, a_lat, of_l, ob_l, hg, y_t, 0, gates, g1, *mw)
        x_lat = _ffn(x_lat, sh2, n2 * (1.0 + sc2), g2, wup, wdn)
        if with_ctx:
            bc = lambda v: jnp.broadcast_to(v, (bsz, 1, d))
            x_ctx = _merge(x_ctx, a_ctx, of_c, ob_c, hg, y_t, t, gates, bc(cg1), *mw)
            x_ctx = _ffn(x_ctx, bc(csh2), bc(n2 * (1.0 + csc2)), bc(cg2), wup, wdn)
    return x_lat
```

```python
import functools
import math

import jax
import jax.numpy as jnp
import numpy as np
from jax import lax
from jax.experimental import pallas as pl
from jax.experimental.pallas import tpu as pltpu

F32 = jnp.float32
BF16 = jnp.bfloat16

D_MODEL = 1024
GRID_W = 64
RMS_EPS = 1e-6
ADALN_CHUNKS = 6
ATT_HEADS = 8
ATT_KV_HEADS = 2
ATT_GROUP = ATT_HEADS // ATT_KV_HEADS
HEAD_DIM = 64
ATT_WIDTH = ATT_HEADS * HEAD_DIM
ATT_KV_WIDTH = ATT_KV_HEADS * HEAD_DIM
ROPE_THETA = 10000.0
HG_HEADS = 4
HG_DK = 64
HG_WIDTH = HG_HEADS * HG_DK
SSM_WIDTH = 256
SSM_GROUP = 16
SSM_GROUPS = SSM_WIDTH // SSM_GROUP
SSM_STATE = 64
FFN_HIDDEN = 2816
N_IN = 5376
O_Q, O_K, O_V, O_HQ, O_HF, O_HI, O_HG, O_U, O_GATE = 0, 512, 640, 768, 1024, 1536, 1792, 2048, 2304

V7X_LANES = 128
V7X_VMEM_BYTES = 64 * 1024 * 1024
MIB = 1024 * 1024

INPROJ_TILE = 256
ATTN_Q_TILE = 256
ATTN_V_ROWS = 80
HG_TILE = 128
S5_CHUNK = 128
LOG2E = math.log2(math.e)


def _dot(a, b):
    return jnp.dot(a, b, preferred_element_type=F32)


def _dot_nt(a, b):
    return lax.dot_general(a, b, (((1,), (1,)), ((), ())), preferred_element_type=F32)


def _dot_tn(a, b):
    return lax.dot_general(a, b, (((0,), (0,)), ((), ())), preferred_element_type=F32)


def _split(x):
    hi = x.astype(BF16)
    lo = (x - hi.astype(F32)).astype(BF16)
    return hi, lo


def _dot3(a, b):
    ah, al = _split(a)
    bh, bl = _split(b)
    return _dot(ah, bh) + (_dot(ah, bl) + _dot(al, bh))


def _sigmoid(x):
    return jax.nn.sigmoid(x)


def _cparams(n_axes, vmem_mib):
    return pltpu.CompilerParams(
        dimension_semantics=("arbitrary",) * n_axes,
        vmem_limit_bytes=min(vmem_mib * MIB, V7X_VMEM_BYTES - 4 * MIB),
    )


def _const_spec(shape):
    nd = len(shape)
    return pl.BlockSpec(shape, lambda *_: (0,) * nd, pipeline_mode=pl.Buffered(1))


def _mod_kernel(c_ref, w_ref, b_ref, o_ref):
    c = c_ref[...]
    s = c * _sigmoid(c)
    o_ref[0] = _dot3(s, w_ref[0]) + b_ref[0]


def _modulation(cond8, w_mod, b_mod):
    n_layers, d, n = w_mod.shape
    nb = 1536
    return pl.pallas_call(
        _mod_kernel,
        grid=(n_layers, n // nb),
        in_specs=[
            pl.BlockSpec((8, d), lambda l, j: (0, 0)),
            pl.BlockSpec((1, d, nb), lambda l, j: (l, 0, j)),
            pl.BlockSpec((1, 1, nb), lambda l, j: (l, 0, j)),
        ],
        out_specs=pl.BlockSpec((1, 8, nb), lambda l, j: (l, 0, j)),
        out_shape=jax.ShapeDtypeStruct((n_layers, 8, n), F32),
        compiler_params=_cparams(2, 40),
        name="adaln_modulation",
    )(cond8, w_mod, b_mod.reshape(n_layers, 1, n))


def _rope128(x, c, s1, s2):
    return x * c + pltpu.roll(x, V7X_LANES - 16, 1) * s1 + pltpu.roll(x, 16, 1) * s2


def _inproj_kernel(x_ref, cx_ref, sh_ref, a_ref, csh_ref, ca_ref, w_ref, qg_ref, kg_ref, c_ref, s1_ref, s2_ref,
                   gmq_ref, gmk_ref, q_ref, k_ref, vt_ref, hq_ref, hf_ref, hi_ref, hg_ref, ut_ref, gate_ref, *, n_lat):
    is_ctx = pl.program_id(1) >= n_lat
    x = jnp.where(is_ctx, cx_ref[0], x_ref[0])
    a = jnp.where(is_ctx, ca_ref[0], a_ref[0])
    sh = jnp.where(is_ctx, csh_ref[0], sh_ref[0])
    ms = jnp.mean(x * x, axis=-1, keepdims=True)
    h = (x * lax.rsqrt(ms + RMS_EPS)) * a + sh
    hb = h.astype(BF16)

    def proj(lo, hi):
        return _dot(hb, w_ref[:, lo:hi])

    c, s1, s2 = c_ref[...], s1_ref[...], s2_ref[...]

    zq = proj(O_Q, O_K)
    msq = _dot((zq * zq).astype(BF16), gmq_ref[...])
    qn = zq * lax.rsqrt(msq + RMS_EPS) * qg_ref[...]
    for j in range(ATT_WIDTH // V7X_LANES):
        sl = slice(j * V7X_LANES, (j + 1) * V7X_LANES)
        q_ref[0, :, sl] = _rope128(qn[:, sl], c, s1, s2).astype(BF16)

    zk = proj(O_K, O_V)
    msk = _dot((zk * zk).astype(BF16), gmk_ref[...])
    kn = _rope128(zk * lax.rsqrt(msk + RMS_EPS) * kg_ref[...], c, s1, s2)
    vt = proj(O_V, O_HQ).T
    tm = x.shape[0]
    ones_row = lax.broadcasted_iota(jnp.int32, (ATTN_V_ROWS - HEAD_DIM, tm), 0) == 0
    for hd in range(ATT_KV_HEADS):
        k_ref[0, hd] = kn[:, hd * HEAD_DIM:(hd + 1) * HEAD_DIM].astype(BF16)
        vt_ref[0, hd, 0:HEAD_DIM, :] = vt[hd * HEAD_DIM:(hd + 1) * HEAD_DIM, :].astype(BF16)
        vt_ref[0, hd, HEAD_DIM:ATTN_V_ROWS, :] = jnp.where(ones_row, 1.0, 0.0).astype(BF16)

    hq_ref[0] = proj(O_HQ, O_HF).astype(BF16)
    hf_ref[0] = proj(O_HF, O_HI)
    hi_ref[0] = proj(O_HI, O_HG).astype(BF16)
    hg_ref[0] = proj(O_HG, O_U).astype(BF16)
    ut = proj(O_U, O_GATE).T
    for j in range(tm // S5_CHUNK):
        ut_ref[0, j] = ut[:, j * S5_CHUNK:(j + 1) * S5_CHUNK]
    for j in range(3):
        lo = O_GATE + j * D_MODEL
        gate_ref[0, :, j * D_MODEL:(j + 1) * D_MODEL] = _sigmoid(proj(lo, lo + D_MODEL)).astype(BF16)


def _inproj(x, cx, sh, a, csh, ca, w_in, qg, kg, rope, gmq, gmk):
    b, t, d = x.shape
    t_ctx = cx.shape[1]
    tm = INPROJ_TILE
    n_lat, n_ctx = t // tm, t_ctx // tm
    t_all = t + t_ctx
    c, s1, s2 = rope
    row = lambda bi, i: (bi, i, 0)
    vec = lambda bi, i: (bi, 0, 0)
    tab = lambda bi, i: (i, 0)

    def widths(specs):
        shapes = [jax.ShapeDtypeStruct((b, t_all, w), dt) for w, dt in specs]
        return shapes, [pl.BlockSpec((1, tm, w), row) for w, _ in specs]

    q_shape, q_spec = widths([(ATT_WIDTH, BF16)])
    h_shape, h_spec = widths([(HG_WIDTH, BF16), (2 * HG_WIDTH, F32), (HG_WIDTH, BF16), (HG_WIDTH, BF16)])
    g_shape, g_spec = widths([(3 * D_MODEL, BF16)])
    out_shape = q_shape + [
        jax.ShapeDtypeStruct((b, ATT_KV_HEADS, t_all, HEAD_DIM), BF16),
        jax.ShapeDtypeStruct((b, ATT_KV_HEADS, ATTN_V_ROWS, t_all), BF16),
    ] + h_shape + [jax.ShapeDtypeStruct((b, t_all // S5_CHUNK, SSM_WIDTH, S5_CHUNK), F32)] + g_shape
    out_specs = q_spec + [
        pl.BlockSpec((1, ATT_KV_HEADS, tm, HEAD_DIM), lambda bi, i: (bi, 0, i, 0)),
        pl.BlockSpec((1, ATT_KV_HEADS, ATTN_V_ROWS, tm), lambda bi, i: (bi, 0, 0, i)),
    ] + h_spec + [pl.BlockSpec((1, tm // S5_CHUNK, SSM_WIDTH, S5_CHUNK), lambda bi, i: (bi, i, 0, 0))] + g_spec
    return pl.pallas_call(
        functools.partial(_inproj_kernel, n_lat=n_lat),
        grid=(b, n_lat + n_ctx),
        in_specs=[
            pl.BlockSpec((1, tm, d), lambda bi, i: (bi, jnp.minimum(i, n_lat - 1), 0)),
            pl.BlockSpec((1, tm, d), lambda bi, i: (bi, jnp.maximum(i - n_lat, 0), 0)),
            pl.BlockSpec((1, 1, d), vec),
            pl.BlockSpec((1, 1, d), vec),
            pl.BlockSpec((1, 1, d), lambda bi, i: (0, 0, 0)),
            pl.BlockSpec((1, 1, d), lambda bi, i: (0, 0, 0)),
            _const_spec((d, N_IN)),
            _const_spec((1, ATT_WIDTH)),
            _const_spec((1, ATT_KV_WIDTH)),
            pl.BlockSpec((tm, V7X_LANES), tab),
            pl.BlockSpec((tm, V7X_LANES), tab),
            pl.BlockSpec((tm, V7X_LANES), tab),
            _const_spec((ATT_WIDTH, ATT_WIDTH)),
            _const_spec((ATT_KV_WIDTH, ATT_KV_WIDTH)),
        ],
        out_specs=out_specs,
        out_shape=out_shape,
        compiler_params=_cparams(2, 56),
        name="in_projection",
    )(x, cx, sh, a, csh, ca, w_in, qg, kg, c, s1, s2, gmq, gmk)


def _attn_kernel(q_ref, k_ref, vt_ref, o_ref, qt_ref, m_ref, acc_ref, sa_ref, sb_ref, mxa_ref, mxb_ref, *, tk, nkb):
    tq = q_ref.shape[1]
    qt = q_ref[0].astype(F32).T
    qt_ref[...] = jnp.concatenate(
        [qt[HEAD_DIM * g:HEAD_DIM * (g + 1), :] for g in range(ATT_GROUP)], axis=1).astype(BF16)
    m_ref[...] = jnp.full(m_ref.shape, -jnp.inf, F32)
    acc_ref[...] = jnp.zeros(acc_ref.shape, F32)

    def scores(kb, s_ref, mx_ref):
        off = pl.multiple_of(kb * tk, tk)
        s = _dot(k_ref[0, 0, pl.ds(off, tk), :], qt_ref[...])
        s_ref[...] = s
        mx_ref[...] = jnp.max(s, axis=0, keepdims=True)

    def consume(kb, s_ref, mx_ref):
        off = pl.multiple_of(kb * tk, tk)
        m_prev = m_ref[...]
        m_new = jnp.maximum(m_prev, mx_ref[...])
        alpha = jnp.exp2(m_prev - m_new)
        p = jnp.exp2(s_ref[...] - m_new).astype(BF16)
        acc_ref[...] = alpha * acc_ref[...] + _dot(vt_ref[0, 0, :, pl.ds(off, tk)], p)
        m_ref[...] = m_new

    bufs = ((sa_ref, mxa_ref), (sb_ref, mxb_ref))
    scores(0, *bufs[0])

    def body(i, carry):
        base = ATTN_BLOCKS_PER_ITER * i
        for j in range(ATTN_BLOCKS_PER_ITER):
            scores(base + j + 1, *bufs[(j + 1) % 2])
            consume(base + j, *bufs[j % 2])
        return carry

    n_it = (nkb - 1) // ATTN_BLOCKS_PER_ITER
    lax.fori_loop(0, n_it, body, 0)
    for r in range(n_it * ATTN_BLOCKS_PER_ITER, nkb):
        if r + 1 < nkb:
            scores(r + 1, *bufs[(r + 1) % 2])
        consume(r, *bufs[r % 2])

    acc = acc_ref[...]
    out_t = acc[0:HEAD_DIM, :] / acc[HEAD_DIM:HEAD_DIM + 1, :]
    out_t = jnp.concatenate([out_t[:, g * tq:(g + 1) * tq] for g in range(ATT_GROUP)], axis=0)
    o_ref[0] = out_t.T.astype(BF16)


ATTN_KEY_BLOCKS = (640, 512, 256, 128)
ATTN_BLOCKS_PER_ITER = 6


def _attention(q, k, vt_ext, q_start, q_len, k_start, k_len):
    b = q.shape[0]
    tq = ATTN_Q_TILE
    tk = next(c for c in ATTN_KEY_BLOCKS if k_len % c == 0)
    assert q_start % tq == 0 and q_len % tq == 0 and k_start % k_len == 0
    q_off, k_blk = q_start // tq, k_start // k_len
    kern = functools.partial(_attn_kernel, tk=tk, nkb=k_len // tk)
    gw = ATT_GROUP * HEAD_DIM
    m = ATT_GROUP * tq
    return pl.pallas_call(
        kern,
        grid=(b, ATT_KV_HEADS, q_len // tq),
        in_specs=[
            pl.BlockSpec((1, tq, gw), lambda bi, h, i: (bi, i + q_off, h)),
            pl.BlockSpec((1, 1, k_len, HEAD_DIM), lambda bi, h, i: (bi, h, k_blk, 0)),
            pl.BlockSpec((1, 1, ATTN_V_ROWS, k_len), lambda bi, h, i: (bi, h, 0, k_blk)),
        ],
        out_specs=pl.BlockSpec((1, tq, gw), lambda bi, h, i: (bi, i, h)),
        out_shape=jax.ShapeDtypeStruct((b, q_len, ATT_WIDTH), BF16),
        scratch_shapes=[
            pltpu.VMEM((HEAD_DIM, m), BF16),
            pltpu.VMEM((1, m), F32),
            pltpu.VMEM((ATTN_V_ROWS, m), F32),
            pltpu.VMEM((tk, m), F32),
            pltpu.VMEM((tk, m), F32),
            pltpu.VMEM((1, m), F32),
            pltpu.VMEM((1, m), F32),
        ],
        compiler_params=_cparams(3, 48),
        name="gqa_attention",
    )(q, k, vt_ext)


HG_LEVELS = 7
HG_SEL_LEVELS = 3


def _hgrn_kernel(qf_ref, ff_ref, vf_ref, qb_ref, fb_ref, vb_ref, lb_ref, s0_ref,
                 tri_ref, sel_ref, sm_ref, hm_ref, hmt_ref, gsum_ref, bd_ref,
                 of_ref, ob_ref, sfin_ref, stf_ref, stb_ref, cf_s, cb_s):
    @pl.when(pl.program_id(1) == 0)
    def _():
        stf_ref[...] = s0_ref[0, 0]
        stb_ref[...] = s0_ref[0, 1]

    tt, w = HG_TILE, HG_WIDTH
    dirs = (0, 1)
    q_refs, f_refs, v_refs = (qf_ref, qb_ref), (ff_ref, fb_ref), (vf_ref, vb_ref)
    o_refs, st_refs, c_refs = (of_ref, ob_ref), (stf_ref, stb_ref), (cf_s, cb_s)
    hm = [hm_ref[h] for h in range(HG_HEADS)]
    hmt = [hmt_ref[h] for h in range(HG_HEADS)]

    qs, kin, vb, c, tot, ref_small, o, scores = [], [], [], [], [], [], [], [None, None]
    for d in dirs:
        q = q_refs[d][0].astype(F32)
        fpre = f_refs[d][0]
        lb = lb_ref[d:d + 1, :]
        qs.append(q * _sigmoid(q))
        kin.append((1.0 - lb) * _sigmoid(-fpre))
        vb.append(v_refs[d][0])
        hi, lo = _split(jnp.log(lb + (1.0 - lb) * _sigmoid(fpre)))
        cd = (_dot(tri_ref[d], hi) + _dot(tri_ref[d], lo)) * LOG2E
        c_refs[d][...] = cd
        c.append(cd)
    for d in dirs:
        last = 0 if d else tt - 1
        tot.append(c_refs[d][last:last + 1, :])
        chi, clo = _split(c[d])
        ref_small.append(_dot(sel_ref[d], chi) + _dot(sel_ref[d], clo))
        o.append(_dot((qs[d] * kin[d]).astype(BF16), gsum_ref[...]) * vb[d].astype(F32))

    for lvl in range(HG_LEVELS):
        m = 1 << lvl
        for d in dirs:
            if lvl < HG_SEL_LEVELS:
                cref = ref_small[d][lvl * tt:(lvl + 1) * tt, :]
            else:
                rows = []
                for blk in range(tt // (2 * m)):
                    r = blk * 2 * m + (m if d else m - 1)
                    rows.append(jnp.broadcast_to(c_refs[d][r:r + 1, :], (2 * m, w)))
                cref = rows[0] if len(rows) == 1 else jnp.concatenate(rows, axis=0)
            wgt = jnp.exp2(-jnp.abs(c[d] - cref))
            ql = (qs[d] * wgt).astype(BF16)
            kl_t = (kin[d] * wgt).T.astype(BF16)
            kstack_t = jnp.concatenate([kl_t * hmt[h] for h in range(HG_HEADS)], axis=1)
            sc = _dot(ql, kstack_t).astype(BF16) * sm_ref[d, lvl]
            scores[d] = sc if scores[d] is None else scores[d] + sc

    for d in dirs:
        vstack = jnp.concatenate([vb[d] * hm[h] for h in range(HG_HEADS)], axis=0)
        o[d] = o[d] + _dot(scores[d], vstack)
    for d in dirs:
        st = st_refs[d][...]
        o[d] = o[d] + _dot_nt((qs[d] * jnp.exp2(c[d])).astype(BF16), st.astype(BF16))
        kv = _dot_tn(vb[d], (kin[d] * jnp.exp2(tot[d] - c[d])).astype(BF16))
        st_new = jnp.exp2(tot[d]) * st + kv * bd_ref[...]
        st_refs[d][...] = st_new
        sfin_ref[0, d] = st_new
        o_refs[d][0] = o[d].astype(BF16)


def _hgrn_constants():
    tt, w = HG_TILE, HG_WIDTH
    t = np.arange(tt)
    tri = np.stack([t[None, :] <= t[:, None], t[None, :] >= t[:, None]]).astype(BF16)
    sel = []
    for reverse in (False, True):
        per = []
        for lvl in range(HG_SEL_LEVELS):
            m = 1 << lvl
            r = (t // (2 * m)) * (2 * m) + (m if reverse else m - 1)
            per.append(t[None, :] == r[:, None])
        sel.append(np.concatenate(per, axis=0))
    sel = np.stack(sel).astype(BF16)
    col = np.arange(HG_HEADS * tt) % tt
    sm = []
    for reverse in (False, True):
        per = []
        for lvl in range(HG_LEVELS):
            t_up, s_up = ((t >> lvl) & 1) == 1, ((col >> lvl) & 1) == 1
            same = (t[:, None] >> (lvl + 1)) == (col[None, :] >> (lvl + 1))
            halves = (~t_up[:, None] & s_up[None, :]) if reverse else (t_up[:, None] & ~s_up[None, :])
            per.append(same & halves)
        sm.append(np.stack(per))
    sm = np.stack(sm).astype(BF16)
    lane_head = np.arange(w) // HG_DK
    hm = np.stack([np.broadcast_to((lane_head == h)[None, :], (tt, w)) for h in range(HG_HEADS)]).astype(BF16)
    same_head = lane_head[:, None] == lane_head[None, :]
    hmt = np.ascontiguousarray(np.swapaxes(hm, 1, 2))
    return tri, sel, sm, hm, hmt, same_head.astype(BF16), same_head.astype(F32)


def _hgrn(hq, hf, hv, lb2, s0, consts, start, t):
    b, _, w = hq.shape
    tt = HG_TILE
    nt, off = t // tt, start // tt
    fwd = lambda bi, i: (bi, off + i, 0)
    bwd = lambda bi, i: (bi, off + nt - 1 - i, 0)
    bwd_f = lambda bi, i: (bi, off + nt - 1 - i, 1)
    fwd_o = lambda bi, i: (bi, i, 0)
    bwd_o = lambda bi, i: (bi, nt - 1 - i, 0)
    st_spec = pl.BlockSpec((1, 2, w, w), lambda bi, i: (bi, 0, 0, 0))
    return pl.pallas_call(
        _hgrn_kernel,
        grid=(b, nt),
        in_specs=[
            pl.BlockSpec((1, tt, w), fwd), pl.BlockSpec((1, tt, w), fwd), pl.BlockSpec((1, tt, w), fwd),
            pl.BlockSpec((1, tt, w), bwd), pl.BlockSpec((1, tt, w), bwd_f), pl.BlockSpec((1, tt, w), bwd),
            pl.BlockSpec((2, w), lambda bi, i: (0, 0)),
            st_spec,
        ] + [_const_spec(a.shape) for a in consts],
        out_specs=[pl.BlockSpec((1, tt, w), fwd_o), pl.BlockSpec((1, tt, w), bwd_o), st_spec],
        out_shape=[
            jax.ShapeDtypeStruct((b, t, w), BF16),
            jax.ShapeDtypeStruct((b, t, w), BF16),
            jax.ShapeDtypeStruct((b, 2, w, w), F32),
        ],
        scratch_shapes=[
            pltpu.VMEM((w, w), F32),
            pltpu.VMEM((w, w), F32),
            pltpu.VMEM((tt, w), F32),
            pltpu.VMEM((tt, w), F32),
        ],
        compiler_params=_cparams(2, 32),
        name="hgrn2_scan",
    )(hq, hf, hv, hq, hf, hv, lb2, s0, *consts)


def _cpow(a_re, a_im, tau):
    mag = jnp.exp(tau * a_re)
    ang = tau * a_im
    return mag * jnp.cos(ang), mag * jnp.sin(ang)


def _s5_prep_kernel(ar_row, ai_row, ar_col, ai_col, lr_row, li_row, bt_r, bt_i, cr, ci, ctr, cti,
                    w_ref, wst_ref, wout_ref, laml_ref, kall_ref):
    lc, p, c = S5_CHUNK, SSM_STATE, SSM_GROUP
    tau_l = lax.broadcasted_iota(jnp.int32, (p, lc), 1).astype(F32)
    tau_s = lax.broadcasted_iota(jnp.int32, (lc, p), 0).astype(F32)

    btr, bti = [], []
    for d in range(2):
        e_r, e_i = _cpow(ar_row[0, d], ai_row[0, d], 1.0)
        l_r, l_i = lr_row[0, d], li_row[0, d]
        den = l_r * l_r + l_i * l_i
        f_r = ((e_r - 1.0) * l_r + e_i * l_i) / den
        f_i = (e_i * l_r - (e_r - 1.0) * l_i) / den
        btr.append(f_r * bt_r[0, d] - f_i * bt_i[0, d])
        bti.append(f_r * bt_i[0, d] + f_i * bt_r[0, d])

    def cb(d):
        re, im = [], []
        for c1 in range(c):
            b_r, b_i = btr[d][c1:c1 + 1, :], bti[d][c1:c1 + 1, :]
            re.append(b_r * cr[0, d] - b_i * ci[0, d])
            im.append(b_r * ci[0, d] + b_i * cr[0, d])
        return jnp.concatenate(re, axis=0), jnp.concatenate(im, axis=0)

    cbf_r, cbf_i = cb(0)
    pf_r, pf_i = _cpow(ar_col[0, 0], ai_col[0, 0], tau_l)
    kf = _dot3(cbf_r, pf_r) - _dot3(cbf_i, pf_i)
    cbb_r, cbb_i = cb(1)
    pb_r, pb_i = _cpow(ar_col[0, 1], ai_col[0, 1], lc - tau_l)
    kb = _dot3(cbb_r, pb_r) - _dot3(cbb_i, pb_i)
    lane = lax.broadcasted_iota(jnp.int32, (c * c, lc), 1)
    kf = kf + jnp.where(lane == 0, jnp.sum(cbb_r, axis=1, keepdims=True), 0.0)
    kall_ref[...] = jnp.concatenate([kf, kb], axis=1)

    def toeplitz_rows(c1, carry):
        for c2 in range(c):
            row = kall_ref[pl.ds(c1 * c + c2, 1), :]
            blk = pltpu.roll(jnp.broadcast_to(row, (lc, 2 * lc)), 0, 1, stride=1, stride_axis=0)
            w_ref[0, pl.ds(pl.multiple_of(c1 * lc, lc), lc), c2 * lc:(c2 + 1) * lc] = blk[:, :lc].astype(BF16)
        return carry

    lax.fori_loop(0, c, toeplitz_rows, 0)

    sf_r, sf_i = _cpow(ar_row[0, 0], ai_row[0, 0], (lc - 1) - tau_s)
    sb_r, sb_i = _cpow(ar_row[0, 1], ai_row[0, 1], tau_s)
    for c1 in range(c):
        re, im = [], []
        for d, (p_r, p_i) in enumerate(((sf_r, sf_i), (sb_r, sb_i))):
            b_r, b_i = btr[d][c1:c1 + 1, :], bti[d][c1:c1 + 1, :]
            re.append(p_r * b_r - p_i * b_i)
            im.append(p_r * b_i + p_i * b_r)
        wst_ref[0, c1 * lc:(c1 + 1) * lc, :] = jnp.concatenate(re + im, axis=1).astype(BF16)

    of_r, of_i = _cpow(ar_col[0, 0], ai_col[0, 0], tau_l + 1.0)
    ob_r, ob_i = _cpow(ar_col[0, 1], ai_col[0, 1], lc - tau_l)
    for c2 in range(c):
        re, im = [], []
        for d, (p_r, p_i) in enumerate(((of_r, of_i), (ob_r, ob_i))):
            c_r, c_i = ctr[0, d, :, c2:c2 + 1], cti[0, d, :, c2:c2 + 1]
            re.append(c_r * p_r - c_i * p_i)
            im.append(-(c_r * p_i + c_i * p_r))
        wout_ref[0, :, c2 * lc:(c2 + 1) * lc] = jnp.concatenate(re + im, axis=0).astype(BF16)

    lf_r, lf_i = _cpow(ar_row[0, 0], ai_row[0, 0], float(lc))
    lb_r, lb_i = _cpow(ar_row[0, 1], ai_row[0, 1], float(lc))
    laml_ref[0, 0:1, :] = jnp.concatenate([lf_r, lb_r], axis=1)
    laml_ref[0, 1:2, :] = jnp.concatenate([lf_i, lb_i], axis=1)


def _s5_prep(a_re, a_im, lam_re, lam_im, b_re, b_im, c_re, c_im):
    g, _, p = a_re.shape
    c, lc = SSM_GROUP, S5_CHUNK
    n = c * lc
    row = lambda x: x.reshape(g, 2, 1, p)
    col = lambda x: x.reshape(g, 2, p, 1)
    tr = lambda x: jnp.swapaxes(x, -1, -2)
    args = [row(a_re), row(a_im), col(a_re), col(a_im), row(lam_re), row(lam_im),
            tr(b_re), tr(b_im), c_re, c_im, tr(c_re), tr(c_im)]
    spec4 = lambda shp: pl.BlockSpec((1,) + shp, lambda gi: (gi, 0, 0, 0))
    spec3 = lambda shp: pl.BlockSpec((1,) + shp, lambda gi: (gi, 0, 0))
    return pl.pallas_call(
        _s5_prep_kernel,
        grid=(g,),
        in_specs=[spec4(a.shape[1:]) for a in args],
        out_specs=[spec3((n, n)), spec3((n, 4 * p)), spec3((4 * p, n)), spec3((2, 2 * p))],
        out_shape=[
            jax.ShapeDtypeStruct((g, n, n), BF16),
            jax.ShapeDtypeStruct((g, n, 4 * p), BF16),
            jax.ShapeDtypeStruct((g, 4 * p, n), BF16),
            jax.ShapeDtypeStruct((g, 2, 2 * p), F32),
        ],
        scratch_shapes=[pltpu.VMEM((c * c, 2 * lc), F32)],
        compiler_params=_cparams(1, 48),
        name="s5_weights",
    )(*args)


def _s5_kernel(u_ref, w_ref, wst_ref, wout_ref, laml_ref, dsk_ref, y_ref, xloc_s, xin_s, *, nctx, nck):
    c, lc, p, p2 = SSM_GROUP, S5_CHUNK, SSM_STATE, 2 * SSM_STATE
    ub = jnp.concatenate([u_ref[0, :, c1, :] for c1 in range(c)], axis=1).astype(BF16)
    xloc_s[...] = _dot(ub, wst_ref[0])

    nlat = nck - nctx
    order_f = list(range(nlat, nck)) + list(range(nlat))
    order_b = list(range(nck - 1, nlat - 1, -1)) + list(range(nlat - 1, -1, -1))
    m_r, m_i = laml_ref[0, 0:1, :], laml_ref[0, 1:2, :]
    is_fwd = lax.broadcasted_iota(jnp.int32, (1, p2), 1) < p
    x_r = jnp.zeros((1, p2), F32)
    x_i = jnp.zeros((1, p2), F32)
    for kf, kb in zip(order_f, order_b):
        xin_s[kf:kf + 1, 0:p] = x_r[:, 0:p]
        xin_s[kb:kb + 1, p:p2] = x_r[:, p:p2]
        xin_s[kf:kf + 1, p2:p2 + p] = x_i[:, 0:p]
        xin_s[kb:kb + 1, p2 + p:2 * p2] = x_i[:, p:p2]
        loc_r = jnp.where(is_fwd, xloc_s[kf:kf + 1, 0:p2], xloc_s[kb:kb + 1, 0:p2])
        loc_i = jnp.where(is_fwd, xloc_s[kf:kf + 1, p2:2 * p2], xloc_s[kb:kb + 1, p2:2 * p2])
        x_r, x_i = m_r * x_r - m_i * x_i + loc_r, m_r * x_i + m_i * x_r + loc_i

    y = _dot(ub, w_ref[0]) + _dot(xin_s[...].astype(BF16), wout_ref[0])
    for c2 in range(c):
        y_ref[0, :, c2, :] = y[:, c2 * lc:(c2 + 1) * lc] + dsk_ref[0, c2:c2 + 1, :] * u_ref[0, :, c2, :]


def _s5(u_t, w, wst, wout, laml, dsk, nctx):
    b, nck, wd, lc = u_t.shape
    g, c = SSM_GROUPS, SSM_GROUP
    n = c * lc
    kern = functools.partial(_s5_kernel, nctx=nctx, nck=nck)
    gspec = lambda shp: pl.BlockSpec((1,) + shp, lambda gi, bi: (gi,) + (0,) * len(shp))
    io_spec = pl.BlockSpec((1, nck, c, lc), lambda gi, bi: (bi, 0, gi, 0))
    return pl.pallas_call(
        kern,
        grid=(g, b),
        in_specs=[io_spec, gspec((n, n)), gspec((n, 4 * SSM_STATE)), gspec((4 * SSM_STATE, n)),
                  gspec((2, 2 * SSM_STATE)), gspec((c, lc))],
        out_specs=io_spec,
        out_shape=jax.ShapeDtypeStruct(u_t.shape, F32),
        scratch_shapes=[
            pltpu.VMEM((nck, 4 * SSM_STATE), F32),
            pltpu.VMEM((nck, 4 * SSM_STATE), F32),
        ],
        compiler_params=_cparams(2, 48),
        name="s5_scan",
    )(u_t, w, wst, wout, laml, dsk)


def _merge_kernel(x_ref, att_ref, of_ref, ob_ref, hg_ref, yt_ref, gate_ref, g1_ref,
                  wa_ref, wr_ref, ws_ref, wo_ref, wglu_ref, bglu_ref, hn_ref, gm_ref, o_ref):
    r = of_ref[0].astype(F32) + ob_ref[0].astype(F32)
    ms = _dot((r * r).astype(BF16), gm_ref[...])
    g = hg_ref[0].astype(F32)
    yrec = (r * lax.rsqrt(ms + RMS_EPS) * hn_ref[...]) * (g * _sigmoid(g))

    ys = jnp.concatenate([yt_ref[0, j].T for j in range(yt_ref.shape[1])], axis=0)
    z = 0.5 * ys * (1.0 + jnp.tanh(math.sqrt(2.0 / math.pi) * (ys + 0.044715 * (ys * ys * ys))))
    yssm = z * _sigmoid(_dot(z.astype(BF16), wglu_ref[...]) + bglu_ref[...])

    d = D_MODEL
    m = gate_ref[0, :, 0:d].astype(F32) * _dot(att_ref[0], wa_ref[...])
    m = m + gate_ref[0, :, d:2 * d].astype(F32) * _dot(yrec.astype(BF16), wr_ref[...])
    m = m + gate_ref[0, :, 2 * d:3 * d].astype(F32) * _dot(yssm.astype(BF16), ws_ref[...])
    y = _dot(m.astype(BF16), wo_ref[...])
    o_ref[0] = x_ref[0] + g1_ref[0] * y


def _merge(x, att, o_f, o_b, hg, y_t, t_off, gates, g1, wa, wr, ws, wo, wglu, bglu, hn, gm):
    b, t, d = x.shape
    tm = min(512, t)
    off = t_off // tm
    row = lambda bi, i: (bi, i, 0)
    row_off = lambda bi, i: (bi, i + off, 0)
    vec = lambda bi, i: (bi, 0, 0)
    return pl.pallas_call(
        _merge_kernel,
        grid=(b, t // tm),
        in_specs=[
            pl.BlockSpec((1, tm, d), row),
            pl.BlockSpec((1, tm, ATT_WIDTH), row),
            pl.BlockSpec((1, tm, HG_WIDTH), row),
            pl.BlockSpec((1, tm, HG_WIDTH), row),
            pl.BlockSpec((1, tm, HG_WIDTH), row_off),
            pl.BlockSpec((1, tm // S5_CHUNK, SSM_WIDTH, S5_CHUNK), lambda bi, i: (bi, i + off, 0, 0)),
            pl.BlockSpec((1, tm, 3 * d), row_off),
            pl.BlockSpec((1, 1, d), vec),
            _const_spec(wa.shape), _const_spec(wr.shape), _const_spec(ws.shape), _const_spec(wo.shape),
            _const_spec(wglu.shape), _const_spec(bglu.shape), _const_spec(hn.shape), _const_spec(gm.shape),
        ],
        out_specs=pl.BlockSpec((1, tm, d), row),
        out_shape=jax.ShapeDtypeStruct((b, t, d), F32),
        compiler_params=_cparams(2, 48),
        name="merge_branches",
    )(x, att, o_f, o_b, hg, y_t, gates, g1, wa, wr, ws, wo, wglu, bglu, hn, gm)


def _ffn_kernel(x_ref, sh_ref, a_ref, g_ref, wup_ref, wdn_ref, o_ref, *, nj):
    x = x_ref[0]
    ms = jnp.mean(x * x, axis=-1, keepdims=True)
    hb = ((x * lax.rsqrt(ms + RMS_EPS)) * a_ref[0] + sh_ref[0]).astype(BF16)
    f = FFN_HIDDEN
    fc = f // nj
    acc = None
    for j in range(nj):
        a = _dot(hb, wup_ref[:, j * fc:(j + 1) * fc])
        bgate = _dot(hb, wup_ref[:, f + j * fc:f + (j + 1) * fc])
        act = ((a * _sigmoid(a)) * bgate).astype(BF16)
        part = _dot(act, wdn_ref[j * fc:(j + 1) * fc, :])
        acc = part if acc is None else acc + part
    o_ref[0] = x + g_ref[0] * acc


def _ffn(x, sh, a, g, wup, wdn):
    b, t, d = x.shape
    tm = min(512, t)
    row = lambda bi, i: (bi, i, 0)
    vec = lambda bi, i: (bi, 0, 0)
    return pl.pallas_call(
        functools.partial(_ffn_kernel, nj=2),
        grid=(b, t // tm),
        in_specs=[
            pl.BlockSpec((1, tm, d), row),
            pl.BlockSpec((1, 1, d), vec), pl.BlockSpec((1, 1, d), vec), pl.BlockSpec((1, 1, d), vec),
            _const_spec(wup.shape), _const_spec(wdn.shape),
        ],
        out_specs=pl.BlockSpec((1, tm, d), row),
        out_shape=jax.ShapeDtypeStruct((b, t, d), F32),
        compiler_params=_cparams(2, 56),
        name="swiglu_ffn",
    )(x, sh, a, g, wup, wdn)


def _rope_tables(t, t_ctx):
    pos = jnp.arange(t)
    row = (pos // GRID_W).astype(F32)
    col = (pos % GRID_W).astype(F32)
    axis_dim = HEAD_DIM // 2
    inv = ROPE_THETA ** (-jnp.arange(0, axis_dim, 2, dtype=F32) / axis_dim)
    ang_r, ang_c = row[:, None] * inv, col[:, None] * inv
    cr, sr, cc, sc = jnp.cos(ang_r), jnp.sin(ang_r), jnp.cos(ang_c), jnp.sin(ang_c)
    z = jnp.zeros_like(cr)
    rep = V7X_LANES // HEAD_DIM
    c = jnp.tile(jnp.concatenate([cr, cr, cc, cc], axis=1), (1, rep))
    s1 = jnp.tile(jnp.concatenate([-sr, z, -sc, z], axis=1), (1, rep))
    s2 = jnp.tile(jnp.concatenate([z, sr, z, sc], axis=1), (1, rep))
    pad = lambda a, fill: jnp.concatenate([a, jnp.full((t_ctx, V7X_LANES), fill, F32)], axis=0)
    return pad(c, 1.0), pad(s1, 0.0), pad(s2, 0.0)


def _group_mean_matrix(width, group):
    i = np.arange(width) // group
    return np.where(i[:, None] == i[None, :], 1.0 / group, 0.0).astype(BF16)


def kernel(x, c, ctx, c_ctx, w_mod, b_mod, norm1_g, norm2_g, w_in, q_norm_g, k_norm_g, hgrn_lb, hgrn_norm_g,
           ssm_lam_re, ssm_lam_im, ssm_log_dt, ssm_b_re, ssm_b_im, ssm_c_re, ssm_c_im, ssm_d, w_glu, b_glu,
           w_br_attn, w_br_hgrn, w_br_ssm, w_out, w_ffn_up, w_ffn_down):
    bsz, t, d = x.shape
    t_ctx = ctx.shape[1]
    t_all = t + t_ctx
    depth = w_mod.shape[0]
    assert t % 512 == 0 and t_ctx % INPROJ_TILE == 0 and t % t_ctx == 0 and d == D_MODEL

    lb_soft = jax.nn.softmax(hgrn_lb.astype(F32), axis=0)
    lower_bounds = jnp.cumsum(lb_soft, axis=0) - lb_soft[0]
    rope = _rope_tables(t, t_ctx)
    gmq = _group_mean_matrix(ATT_WIDTH, HEAD_DIM)
    gmk = _group_mean_matrix(ATT_KV_WIDTH, HEAD_DIM)
    gmh = _group_mean_matrix(HG_WIDTH, HG_DK)
    hg_consts = _hgrn_constants()
    cond8 = jnp.zeros((8, d), F32).at[:bsz].set(c).at[bsz].set(c_ctx)
    mods = _modulation(cond8, w_mod, b_mod)

    x_lat, x_ctx = x, ctx
    for l in range(depth):
        with_ctx = l < depth - 1
        ml = mods[l, :bsz].reshape(bsz, ADALN_CHUNKS, 1, d)
        mc = mods[l, bsz].reshape(1, ADALN_CHUNKS, 1, d)
        sh1, sc1, g1, sh2, sc2, g2 = [ml[:, i] for i in range(ADALN_CHUNKS)]
        csh1, csc1, cg1, csh2, csc2, cg2 = [mc[:, i] for i in range(ADALN_CHUNKS)]
        n1, n2 = norm1_g[l].reshape(1, 1, d), norm2_g[l].reshape(1, 1, d)

        qg = (jnp.tile(q_norm_g[l], ATT_HEADS) * (HEAD_DIM ** -0.5 * LOG2E)).reshape(1, ATT_WIDTH)
        kg = jnp.tile(k_norm_g[l], ATT_KV_HEADS).reshape(1, ATT_KV_WIDTH)
        q, k, vt, hq, hf, hi, hg, u_t, gates = _inproj(
            x_lat, x_ctx, sh1, n1 * (1.0 + sc1), csh1, n1 * (1.0 + csc1), w_in[l].astype(BF16), qg, kg, rope, gmq, gmk)

        a_lat = _attention(q, k, vt, 0, t, 0, t_all)
        if with_ctx:
            a_ctx = _attention(q, k, vt, t, t_ctx, t, t_ctx)

        lb2 = lower_bounds[l]
        s_zero = jnp.zeros((bsz, 2, HG_WIDTH, HG_WIDTH), F32)
        of_c, ob_c, s_ctx = _hgrn(hq, hf, hi, lb2, s_zero, hg_consts, t, t_ctx)
        of_l, ob_l, _ = _hgrn(hq, hf, hi, lb2, s_ctx, hg_consts, 0, t)

        dt = jnp.exp(ssm_log_dt[l].astype(F32))[..., None]
        gd = lambda a: jnp.swapaxes(a.astype(F32), 0, 1)
        w_s5, wst, wout, laml = _s5_prep(
            gd(ssm_lam_re[l] * dt), gd(ssm_lam_im[l] * dt), gd(ssm_lam_re[l]), gd(ssm_lam_im[l]),
            gd(ssm_b_re[l]), gd(ssm_b_im[l]), gd(ssm_c_re[l]), gd(ssm_c_im[l]))
        dsk = jnp.broadcast_to(ssm_d[l].astype(F32).reshape(SSM_GROUPS, SSM_GROUP, 1), (SSM_GROUPS, SSM_GROUP, S5_CHUNK))
        y_t = _s5(u_t, w_s5, wst, wout, laml, dsk, t_ctx // S5_CHUNK)

        hn = jnp.tile(hgrn_norm_g[l], HG_HEADS).reshape(1, HG_WIDTH)
        mw = (w_br_attn[l].astype(BF16), w_br_hgrn[l].astype(BF16), w_br_ssm[l].astype(BF16), w_out[l].astype(BF16),
              w_glu[l].astype(BF16), b_glu[l].reshape(1, SSM_WIDTH), hn, gmh)
        wup, wdn = w_ffn_up[l].astype(BF16), w_ffn_down[l].astype(BF16)
        x_lat = _merge(x_lat, a_lat, of_l, ob_l, hg, y_t, 0, gates, g1, *mw)
        x_lat = _ffn(x_lat, sh2, n2 * (1.0 + sc2), g2, wup, wdn)
        if with_ctx:
            bc = lambda v: jnp.broadcast_to(v, (bsz, 1, d))
            x_ctx = _merge(x_ctx, a_ctx, of_c, ob_c, hg, y_t, t, gates, bc(cg1), *mw)
            x_ctx = _ffn(x_ctx, bc(csh2), bc(n2 * (1.0 + csc2)), bc(cg2), wup, wdn)
    return x_lat
```

```python
import functools
import math

import jax
import jax.numpy as jnp
import numpy as np
from jax import lax
from jax.experimental import pallas as pl
from jax.experimental.pallas import tpu as pltpu

F32 = jnp.float32
BF16 = jnp.bfloat16

D_MODEL = 1024
GRID_W = 64
RMS_EPS = 1e-6
ADALN_CHUNKS = 6
ATT_HEADS = 8
ATT_KV_HEADS = 2
ATT_GROUP = ATT_HEADS // ATT_KV_HEADS
HEAD_DIM = 64
ATT_WIDTH = ATT_HEADS * HEAD_DIM
ATT_KV_WIDTH = ATT_KV_HEADS * HEAD_DIM
ROPE_THETA = 10000.0
HG_HEADS = 4
HG_DK = 64
HG_WIDTH = HG_HEADS * HG_DK
SSM_WIDTH = 256
SSM_GROUP = 16
SSM_GROUPS = SSM_WIDTH // SSM_GROUP
SSM_STATE = 64
FFN_HIDDEN = 2816
N_IN = 5376
O_Q, O_K, O_V, O_HQ, O_HF, O_HI, O_HG, O_U, O_GATE = 0, 512, 640, 768, 1024, 1536, 1792, 2048, 2304

V7X_LANES = 128
V7X_VMEM_BYTES = 64 * 1024 * 1024
MIB = 1024 * 1024

INPROJ_TILE = 256
ATTN_Q_TILE = 128
ATTN_V_ROWS = 80
HG_TILE = 128
S5_CHUNK = 128
LOG2E = math.log2(math.e)


def _dot(a, b):
    return jnp.dot(a, b, preferred_element_type=F32)


def _dot_nt(a, b):
    return lax.dot_general(a, b, (((1,), (1,)), ((), ())), preferred_element_type=F32)


def _dot_tn(a, b):
    return lax.dot_general(a, b, (((0,), (0,)), ((), ())), preferred_element_type=F32)


def _split(x):
    hi = x.astype(BF16)
    lo = (x - hi.astype(F32)).astype(BF16)
    return hi, lo


def _dot3(a, b):
    ah, al = _split(a)
    bh, bl = _split(b)
    return _dot(ah, bh) + (_dot(ah, bl) + _dot(al, bh))


def _sigmoid(x):
    return jax.nn.sigmoid(x)


def _cparams(n_axes, vmem_mib):
    return pltpu.CompilerParams(
        dimension_semantics=("arbitrary",) * n_axes,
        vmem_limit_bytes=min(vmem_mib * MIB, V7X_VMEM_BYTES - 4 * MIB),
    )


def _const_spec(shape):
    nd = len(shape)
    return pl.BlockSpec(shape, lambda *_: (0,) * nd, pipeline_mode=pl.Buffered(1))


def _mod_kernel(c_ref, w_ref, b_ref, o_ref):
    c = c_ref[...]
    s = c * _sigmoid(c)
    o_ref[0] = _dot3(s, w_ref[0]) + b_ref[0]


def _modulation(cond8, w_mod, b_mod):
    n_layers, d, n = w_mod.shape
    nb = 1536
    return pl.pallas_call(
        _mod_kernel,
        grid=(n_layers, n // nb),
        in_specs=[
            pl.BlockSpec((8, d), lambda l, j: (0, 0)),
            pl.BlockSpec((1, d, nb), lambda l, j: (l, 0, j)),
            pl.BlockSpec((1, 1, nb), lambda l, j: (l, 0, j)),
        ],
        out_specs=pl.BlockSpec((1, 8, nb), lambda l, j: (l, 0, j)),
        out_shape=jax.ShapeDtypeStruct((n_layers, 8, n), F32),
        compiler_params=_cparams(2, 40),
        name="adaln_modulation",
    )(cond8, w_mod, b_mod.reshape(n_layers, 1, n))


def _rope128(x, c, s1, s2):
    return x * c + pltpu.roll(x, V7X_LANES - 16, 1) * s1 + pltpu.roll(x, 16, 1) * s2


def _inproj_kernel(x_ref, cx_ref, sh_ref, a_ref, csh_ref, ca_ref, w_ref, qg_ref, kg_ref, c_ref, s1_ref, s2_ref,
                   gmq_ref, gmk_ref, q_ref, k_ref, vt_ref, hq_ref, hf_ref, hi_ref, hg_ref, ut_ref, gate_ref, *, n_lat):
    is_ctx = pl.program_id(1) >= n_lat
    x = jnp.where(is_ctx, cx_ref[0], x_ref[0])
    a = jnp.where(is_ctx, ca_ref[0], a_ref[0])
    sh = jnp.where(is_ctx, csh_ref[0], sh_ref[0])
    ms = jnp.mean(x * x, axis=-1, keepdims=True)
    h = (x * lax.rsqrt(ms + RMS_EPS)) * a + sh
    hb = h.astype(BF16)

    def proj(lo, hi):
        return _dot(hb, w_ref[:, lo:hi])

    c, s1, s2 = c_ref[...], s1_ref[...], s2_ref[...]

    zq = proj(O_Q, O_K)
    msq = _dot((zq * zq).astype(BF16), gmq_ref[...])
    qn = zq * lax.rsqrt(msq + RMS_EPS) * qg_ref[...]
    for j in range(ATT_WIDTH // V7X_LANES):
        sl = slice(j * V7X_LANES, (j + 1) * V7X_LANES)
        q_ref[0, :, sl] = _rope128(qn[:, sl], c, s1, s2).astype(BF16)

    zk = proj(O_K, O_V)
    msk = _dot((zk * zk).astype(BF16), gmk_ref[...])
    kn = _rope128(zk * lax.rsqrt(msk + RMS_EPS) * kg_ref[...], c, s1, s2)
    vt = proj(O_V, O_HQ).T
    tm = x.shape[0]
    ones_row = lax.broadcasted_iota(jnp.int32, (ATTN_V_ROWS - HEAD_DIM, tm), 0) == 0
    for hd in range(ATT_KV_HEADS):
        k_ref[0, hd] = kn[:, hd * HEAD_DIM:(hd + 1) * HEAD_DIM].astype(BF16)
        vt_ref[0, hd, 0:HEAD_DIM, :] = vt[hd * HEAD_DIM:(hd + 1) * HEAD_DIM, :].astype(BF16)
        vt_ref[0, hd, HEAD_DIM:ATTN_V_ROWS, :] = jnp.where(ones_row, 1.0, 0.0).astype(BF16)

    hq_ref[0] = proj(O_HQ, O_HF).astype(BF16)
    hf_ref[0] = proj(O_HF, O_HI)
    hi_ref[0] = proj(O_HI, O_HG).astype(BF16)
    hg_ref[0] = proj(O_HG, O_U).astype(BF16)
    ut = proj(O_U, O_GATE).T
    for j in range(tm // S5_CHUNK):
        ut_ref[0, j] = ut[:, j * S5_CHUNK:(j + 1) * S5_CHUNK]
    for j in range(3):
        lo = O_GATE + j * D_MODEL
        gate_ref[0, :, j * D_MODEL:(j + 1) * D_MODEL] = _sigmoid(proj(lo, lo + D_MODEL)).astype(BF16)


def _inproj(x, cx, sh, a, csh, ca, w_in, qg, kg, rope, gmq, gmk):
    b, t, d = x.shape
    t_ctx = cx.shape[1]
    tm = INPROJ_TILE
    n_lat, n_ctx = t // tm, t_ctx // tm
    t_all = t + t_ctx
    c, s1, s2 = rope
    row = lambda bi, i: (bi, i, 0)
    vec = lambda bi, i: (bi, 0, 0)
    tab = lambda bi, i: (i, 0)

    def widths(specs):
        shapes = [jax.ShapeDtypeStruct((b, t_all, w), dt) for w, dt in specs]
        return shapes, [pl.BlockSpec((1, tm, w), row) for w, _ in specs]

    q_shape, q_spec = widths([(ATT_WIDTH, BF16)])
    h_shape, h_spec = widths([(HG_WIDTH, BF16), (2 * HG_WIDTH, F32), (HG_WIDTH, BF16), (HG_WIDTH, BF16)])
    g_shape, g_spec = widths([(3 * D_MODEL, BF16)])
    out_shape = q_shape + [
        jax.ShapeDtypeStruct((b, ATT_KV_HEADS, t_all, HEAD_DIM), BF16),
        jax.ShapeDtypeStruct((b, ATT_KV_HEADS, ATTN_V_ROWS, t_all), BF16),
    ] + h_shape + [jax.ShapeDtypeStruct((b, t_all // S5_CHUNK, SSM_WIDTH, S5_CHUNK), F32)] + g_shape
    out_specs = q_spec + [
        pl.BlockSpec((1, ATT_KV_HEADS, tm, HEAD_DIM), lambda bi, i: (bi, 0, i, 0)),
        pl.BlockSpec((1, ATT_KV_HEADS, ATTN_V_ROWS, tm), lambda bi, i: (bi, 0, 0, i)),
    ] + h_spec + [pl.BlockSpec((1, tm // S5_CHUNK, SSM_WIDTH, S5_CHUNK), lambda bi, i: (bi, i, 0, 0))] + g_spec
    return pl.pallas_call(
        functools.partial(_inproj_kernel, n_lat=n_lat),
        grid=(b, n_lat + n_ctx),
        in_specs=[
            pl.BlockSpec((1, tm, d), lambda bi, i: (bi, jnp.minimum(i, n_lat - 1), 0)),
            pl.BlockSpec((1, tm, d), lambda bi, i: (bi, jnp.maximum(i - n_lat, 0), 0)),
            pl.BlockSpec((1, 1, d), vec),
            pl.BlockSpec((1, 1, d), vec),
            pl.BlockSpec((1, 1, d), lambda bi, i: (0, 0, 0)),
            pl.BlockSpec((1, 1, d), lambda bi, i: (0, 0, 0)),
            _const_spec((d, N_IN)),
            _const_spec((1, ATT_WIDTH)),
            _const_spec((1, ATT_KV_WIDTH)),
            pl.BlockSpec((tm, V7X_LANES), tab),
            pl.BlockSpec((tm, V7X_LANES), tab),
            pl.BlockSpec((tm, V7X_LANES), tab),
            _const_spec((ATT_WIDTH, ATT_WIDTH)),
            _const_spec((ATT_KV_WIDTH, ATT_KV_WIDTH)),
        ],
        out_specs=out_specs,
        out_shape=out_shape,
        compiler_params=_cparams(2, 56),
        name="in_projection",
    )(x, cx, sh, a, csh, ca, w_in, qg, kg, c, s1, s2, gmq, gmk)


def _attn_kernel(q_ref, k_ref, vt_ref, o_ref, qt_ref, m_ref, acc_ref, sa_ref, sb_ref, mxa_ref, mxb_ref, *, tk, nkb):
    tq = q_ref.shape[1]
    qt = q_ref[0].astype(F32).T
    qt_ref[...] = jnp.concatenate(
        [qt[HEAD_DIM * g:HEAD_DIM * (g + 1), :] for g in range(ATT_GROUP)], axis=1).astype(BF16)
    m_ref[...] = jnp.full(m_ref.shape, -jnp.inf, F32)
    acc_ref[...] = jnp.zeros(acc_ref.shape, F32)

    def scores(kb, s_ref, mx_ref):
        off = pl.multiple_of(kb * tk, tk)
        s = _dot(k_ref[0, 0, pl.ds(off, tk), :], qt_ref[...])
        s_ref[...] = s
        mx_ref[...] = jnp.max(s, axis=0, keepdims=True)

    def consume(kb, s_ref, mx_ref):
        off = pl.multiple_of(kb * tk, tk)
        m_prev = m_ref[...]
        m_new = jnp.maximum(m_prev, mx_ref[...])
        alpha = jnp.exp2(m_prev - m_new)
        p = jnp.exp2(s_ref[...] - m_new).astype(BF16)
        acc_ref[...] = alpha * acc_ref[...] + _dot(vt_ref[0, 0, :, pl.ds(off, tk)], p)
        m_ref[...] = m_new

    bufs = ((sa_ref, mxa_ref), (sb_ref, mxb_ref))
    scores(0, *bufs[0])

    def body(i, carry):
        base = ATTN_BLOCKS_PER_ITER * i
        for j in range(ATTN_BLOCKS_PER_ITER):
            scores(base + j + 1, *bufs[(j + 1) % 2])
            consume(base + j, *bufs[j % 2])
        return carry

    n_it = (nkb - 1) // ATTN_BLOCKS_PER_ITER
    lax.fori_loop(0, n_it, body, 0)
    for r in range(n_it * ATTN_BLOCKS_PER_ITER, nkb):
        if r + 1 < nkb:
            scores(r + 1, *bufs[(r + 1) % 2])
        consume(r, *bufs[r % 2])

    acc = acc_ref[...]
    out_t = acc[0:HEAD_DIM, :] / acc[HEAD_DIM:HEAD_DIM + 1, :]
    out_t = jnp.concatenate([out_t[:, g * tq:(g + 1) * tq] for g in range(ATT_GROUP)], axis=0)
    o_ref[0] = out_t.T.astype(BF16)


ATTN_KEY_BLOCKS = (640, 512, 256, 128)
ATTN_BLOCKS_PER_ITER = 12


def _attention(q, k, vt_ext, q_start, q_len, k_start, k_len):
    b = q.shape[0]
    tq = ATTN_Q_TILE
    tk = next(c for c in ATTN_KEY_BLOCKS if k_len % c == 0)
    assert q_start % tq == 0 and q_len % tq == 0 and k_start % k_len == 0
    q_off, k_blk = q_start // tq, k_start // k_len
    kern = functools.partial(_attn_kernel, tk=tk, nkb=k_len // tk)
    gw = ATT_GROUP * HEAD_DIM
    m = ATT_GROUP * tq
    return pl.pallas_call(
        kern,
        grid=(b, ATT_KV_HEADS, q_len // tq),
        in_specs=[
            pl.BlockSpec((1, tq, gw), lambda bi, h, i: (bi, i + q_off, h)),
            pl.BlockSpec((1, 1, k_len, HEAD_DIM), lambda bi, h, i: (bi, h, k_blk, 0)),
            pl.BlockSpec((1, 1, ATTN_V_ROWS, k_len), lambda bi, h, i: (bi, h, 0, k_blk)),
        ],
        out_specs=pl.BlockSpec((1, tq, gw), lambda bi, h, i: (bi, i, h)),
        out_shape=jax.ShapeDtypeStruct((b, q_len, ATT_WIDTH), BF16),
        scratch_shapes=[
            pltpu.VMEM((HEAD_DIM, m), BF16),
            pltpu.VMEM((1, m), F32),
            pltpu.VMEM((ATTN_V_ROWS, m), F32),
            pltpu.VMEM((tk, m), F32),
            pltpu.VMEM((tk, m), F32),
            pltpu.VMEM((1, m), F32),
            pltpu.VMEM((1, m), F32),
        ],
        compiler_params=_cparams(3, 48),
        name="gqa_attention",
    )(q, k, vt_ext)


HG_LEVELS = 7
HG_SEL_LEVELS = 3


def _hgrn_kernel(qf_ref, ff_ref, vf_ref, qb_ref, fb_ref, vb_ref, lb_ref, s0_ref,
                 tri_ref, sel_ref, sm_ref, hm_ref, hmt_ref, gsum_ref, bd_ref,
                 of_ref, ob_ref, sfin_ref, stf_ref, stb_ref, cf_s, cb_s):
    @pl.when(pl.program_id(1) == 0)
    def _():
        stf_ref[...] = s0_ref[0, 0]
        stb_ref[...] = s0_ref[0, 1]

    tt, w = HG_TILE, HG_WIDTH
    dirs = (0, 1)
    q_refs, f_refs, v_refs = (qf_ref, qb_ref), (ff_ref, fb_ref), (vf_ref, vb_ref)
    o_refs, st_refs, c_refs = (of_ref, ob_ref), (stf_ref, stb_ref), (cf_s, cb_s)
    hm = [hm_ref[h] for h in range(HG_HEADS)]
    hmt = [hmt_ref[h] for h in range(HG_HEADS)]

    qs, kin, vb, c, tot, ref_small, o, scores = [], [], [], [], [], [], [], [None, None]
    for d in dirs:
        q = q_refs[d][0].astype(F32)
        fpre = f_refs[d][0]
        lb = lb_ref[d:d + 1, :]
        qs.append(q * _sigmoid(q))
        kin.append((1.0 - lb) * _sigmoid(-fpre))
        vb.append(v_refs[d][0])
        hi, lo = _split(jnp.log(lb + (1.0 - lb) * _sigmoid(fpre)))
        cd = (_dot(tri_ref[d], hi) + _dot(tri_ref[d], lo)) * LOG2E
        c_refs[d][...] = cd
        c.append(cd)
    for d in dirs:
        last = 0 if d else tt - 1
        tot.append(c_refs[d][last:last + 1, :])
        chi, clo = _split(c[d])
        ref_small.append(_dot(sel_ref[d], chi) + _dot(sel_ref[d], clo))
        o.append(_dot((qs[d] * kin[d]).astype(BF16), gsum_ref[...]) * vb[d].astype(F32))

    for lvl in range(HG_LEVELS):
        m = 1 << lvl
        for d in dirs:
            if lvl < HG_SEL_LEVELS:
                cref = ref_small[d][lvl * tt:(lvl + 1) * tt, :]
            else:
                rows = []
                for blk in range(tt // (2 * m)):
                    r = blk * 2 * m + (m if d else m - 1)
                    rows.append(jnp.broadcast_to(c_refs[d][r:r + 1, :], (2 * m, w)))
                cref = rows[0] if len(rows) == 1 else jnp.concatenate(rows, axis=0)
            wgt = jnp.exp2(-jnp.abs(c[d] - cref))
            ql = (qs[d] * wgt).astype(BF16)
            kl_t = (kin[d] * wgt).T.astype(BF16)
            kstack_t = jnp.concatenate([kl_t * hmt[h] for h in range(HG_HEADS)], axis=1)
            sc = _dot(ql, kstack_t).astype(BF16) * sm_ref[d, lvl]
            scores[d] = sc if scores[d] is None else scores[d] + sc

    for d in dirs:
        vstack = jnp.concatenate([vb[d] * hm[h] for h in range(HG_HEADS)], axis=0)
        o[d] = o[d] + _dot(scores[d], vstack)
    for d in dirs:
        st = st_refs[d][...]
        o[d] = o[d] + _dot_nt((qs[d] * jnp.exp2(c[d])).astype(BF16), st.astype(BF16))
        kv = _dot_tn(vb[d], (kin[d] * jnp.exp2(tot[d] - c[d])).astype(BF16))
        st_new = jnp.exp2(tot[d]) * st + kv * bd_ref[...]
        st_refs[d][...] = st_new
        sfin_ref[0, d] = st_new
        o_refs[d][0] = o[d].astype(BF16)


def _hgrn_constants():
    tt, w = HG_TILE, HG_WIDTH
    t = np.arange(tt)
    tri = np.stack([t[None, :] <= t[:, None], t[None, :] >= t[:, None]]).astype(BF16)
    sel = []
    for reverse in (False, True):
        per = []
        for lvl in range(HG_SEL_LEVELS):
            m = 1 << lvl
            r = (t // (2 * m)) * (2 * m) + (m if reverse else m - 1)
            per.append(t[None, :] == r[:, None])
        sel.append(np.concatenate(per, axis=0))
    sel = np.stack(sel).astype(BF16)
    col = np.arange(HG_HEADS * tt) % tt
    sm = []
    for reverse in (False, True):
        per = []
        for lvl in range(HG_LEVELS):
            t_up, s_up = ((t >> lvl) & 1) == 1, ((col >> lvl) & 1) == 1
            same = (t[:, None] >> (lvl + 1)) == (col[None, :] >> (lvl + 1))
            halves = (~t_up[:, None] & s_up[None, :]) if reverse else (t_up[:, None] & ~s_up[None, :])
            per.append(same & halves)
        sm.append(np.stack(per))
    sm = np.stack(sm).astype(BF16)
    lane_head = np.arange(w) // HG_DK
    hm = np.stack([np.broadcast_to((lane_head == h)[None, :], (tt, w)) for h in range(HG_HEADS)]).astype(BF16)
    same_head = lane_head[:, None] == lane_head[None, :]
    hmt = np.ascontiguousarray(np.swapaxes(hm, 1, 2))
    return tri, sel, sm, hm, hmt, same_head.astype(BF16), same_head.astype(F32)


def _hgrn(hq, hf, hv, lb2, s0, consts, start, t):
    b, _, w = hq.shape
    tt = HG_TILE
    nt, off = t // tt, start // tt
    fwd = lambda bi, i: (bi, off + i, 0)
    bwd = lambda bi, i: (bi, off + nt - 1 - i, 0)
    bwd_f = lambda bi, i: (bi, off + nt - 1 - i, 1)
    fwd_o = lambda bi, i: (bi, i, 0)
    bwd_o = lambda bi, i: (bi, nt - 1 - i, 0)
    st_spec = pl.BlockSpec((1, 2, w, w), lambda bi, i: (bi, 0, 0, 0))
    return pl.pallas_call(
        _hgrn_kernel,
        grid=(b, nt),
        in_specs=[
            pl.BlockSpec((1, tt, w), fwd), pl.BlockSpec((1, tt, w), fwd), pl.BlockSpec((1, tt, w), fwd),
            pl.BlockSpec((1, tt, w), bwd), pl.BlockSpec((1, tt, w), bwd_f), pl.BlockSpec((1, tt, w), bwd),
            pl.BlockSpec((2, w), lambda bi, i: (0, 0)),
            st_spec,
        ] + [_const_spec(a.shape) for a in consts],
        out_specs=[pl.BlockSpec((1, tt, w), fwd_o), pl.BlockSpec((1, tt, w), bwd_o), st_spec],
        out_shape=[
            jax.ShapeDtypeStruct((b, t, w), BF16),
            jax.ShapeDtypeStruct((b, t, w), BF16),
            jax.ShapeDtypeStruct((b, 2, w, w), F32),
        ],
        scratch_shapes=[
            pltpu.VMEM((w, w), F32),
            pltpu.VMEM((w, w), F32),
            pltpu.VMEM((tt, w), F32),
            pltpu.VMEM((tt, w), F32),
        ],
        compiler_params=_cparams(2, 32),
        name="hgrn2_scan",
    )(hq, hf, hv, hq, hf, hv, lb2, s0, *consts)


def _cpow(a_re, a_im, tau):
    mag = jnp.exp(tau * a_re)
    ang = tau * a_im
    return mag * jnp.cos(ang), mag * jnp.sin(ang)


def _s5_prep_kernel(ar_row, ai_row, ar_col, ai_col, lr_row, li_row, bt_r, bt_i, cr, ci, ctr, cti,
                    w_ref, wst_ref, wout_ref, laml_ref, kall_ref):
    lc, p, c = S5_CHUNK, SSM_STATE, SSM_GROUP
    tau_l = lax.broadcasted_iota(jnp.int32, (p, lc), 1).astype(F32)
    tau_s = lax.broadcasted_iota(jnp.int32, (lc, p), 0).astype(F32)

    btr, bti = [], []
    for d in range(2):
        e_r, e_i = _cpow(ar_row[0, d], ai_row[0, d], 1.0)
        l_r, l_i = lr_row[0, d], li_row[0, d]
        den = l_r * l_r + l_i * l_i
        f_r = ((e_r - 1.0) * l_r + e_i * l_i) / den
        f_i = (e_i * l_r - (e_r - 1.0) * l_i) / den
        btr.append(f_r * bt_r[0, d] - f_i * bt_i[0, d])
        bti.append(f_r * bt_i[0, d] + f_i * bt_r[0, d])

    def cb(d):
        re, im = [], []
        for c1 in range(c):
            b_r, b_i = btr[d][c1:c1 + 1, :], bti[d][c1:c1 + 1, :]
            re.append(b_r * cr[0, d] - b_i * ci[0, d])
            im.append(b_r * ci[0, d] + b_i * cr[0, d])
        return jnp.concatenate(re, axis=0), jnp.concatenate(im, axis=0)

    cbf_r, cbf_i = cb(0)
    pf_r, pf_i = _cpow(ar_col[0, 0], ai_col[0, 0], tau_l)
    kf = _dot3(cbf_r, pf_r) - _dot3(cbf_i, pf_i)
    cbb_r, cbb_i = cb(1)
    pb_r, pb_i = _cpow(ar_col[0, 1], ai_col[0, 1], lc - tau_l)
    kb = _dot3(cbb_r, pb_r) - _dot3(cbb_i, pb_i)
    lane = lax.broadcasted_iota(jnp.int32, (c * c, lc), 1)
    kf = kf + jnp.where(lane == 0, jnp.sum(cbb_r, axis=1, keepdims=True), 0.0)
    kall_ref[...] = jnp.concatenate([kf, kb], axis=1)

    def toeplitz_rows(c1, carry):
        for c2 in range(c):
            row = kall_ref[pl.ds(c1 * c + c2, 1), :]
            blk = pltpu.roll(jnp.broadcast_to(row, (lc, 2 * lc)), 0, 1, stride=1, stride_axis=0)
            w_ref[0, pl.ds(pl.multiple_of(c1 * lc, lc), lc), c2 * lc:(c2 + 1) * lc] = blk[:, :lc].astype(BF16)
        return carry

    lax.fori_loop(0, c, toeplitz_rows, 0)

    sf_r, sf_i = _cpow(ar_row[0, 0], ai_row[0, 0], (lc - 1) - tau_s)
    sb_r, sb_i = _cpow(ar_row[0, 1], ai_row[0, 1], tau_s)
    for c1 in range(c):
        re, im = [], []
        for d, (p_r, p_i) in enumerate(((sf_r, sf_i), (sb_r, sb_i))):
            b_r, b_i = btr[d][c1:c1 + 1, :], bti[d][c1:c1 + 1, :]
            re.append(p_r * b_r - p_i * b_i)
            im.append(p_r * b_i + p_i * b_r)
        wst_ref[0, c1 * lc:(c1 + 1) * lc, :] = jnp.concatenate(re + im, axis=1).astype(BF16)

    of_r, of_i = _cpow(ar_col[0, 0], ai_col[0, 0], tau_l + 1.0)
    ob_r, ob_i = _cpow(ar_col[0, 1], ai_col[0, 1], lc - tau_l)
    for c2 in range(c):
        re, im = [], []
        for d, (p_r, p_i) in enumerate(((of_r, of_i), (ob_r, ob_i))):
            c_r, c_i = ctr[0, d, :, c2:c2 + 1], cti[0, d, :, c2:c2 + 1]
            re.append(c_r * p_r - c_i * p_i)
            im.append(-(c_r * p_i + c_i * p_r))
        wout_ref[0, :, c2 * lc:(c2 + 1) * lc] = jnp.concatenate(re + im, axis=0).astype(BF16)

    lf_r, lf_i = _cpow(ar_row[0, 0], ai_row[0, 0], float(lc))
    lb_r, lb_i = _cpow(ar_row[0, 1], ai_row[0, 1], float(lc))
    laml_ref[0, 0:1, :] = jnp.concatenate([lf_r, lb_r], axis=1)
    laml_ref[0, 1:2, :] = jnp.concatenate([lf_i, lb_i], axis=1)


def _s5_prep(a_re, a_im, lam_re, lam_im, b_re, b_im, c_re, c_im):
    g, _, p = a_re.shape
    c, lc = SSM_GROUP, S5_CHUNK
    n = c * lc
    row = lambda x: x.reshape(g, 2, 1, p)
    col = lambda x: x.reshape(g, 2, p, 1)
    tr = lambda x: jnp.swapaxes(x, -1, -2)
    args = [row(a_re), row(a_im), col(a_re), col(a_im), row(lam_re), row(lam_im),
            tr(b_re), tr(b_im), c_re, c_im, tr(c_re), tr(c_im)]
    spec4 = lambda shp: pl.BlockSpec((1,) + shp, lambda gi: (gi, 0, 0, 0))
    spec3 = lambda shp: pl.BlockSpec((1,) + shp, lambda gi: (gi, 0, 0))
    return pl.pallas_call(
        _s5_prep_kernel,
        grid=(g,),
        in_specs=[spec4(a.shape[1:]) for a in args],
        out_specs=[spec3((n, n)), spec3((n, 4 * p)), spec3((4 * p, n)), spec3((2, 2 * p))],
        out_shape=[
            jax.ShapeDtypeStruct((g, n, n), BF16),
            jax.ShapeDtypeStruct((g, n, 4 * p), BF16),
            jax.ShapeDtypeStruct((g, 4 * p, n), BF16),
            jax.ShapeDtypeStruct((g, 2, 2 * p), F32),
        ],
        scratch_shapes=[pltpu.VMEM((c * c, 2 * lc), F32)],
        compiler_params=_cparams(1, 48),
        name="s5_weights",
    )(*args)


def _s5_kernel(u_ref, w_ref, wst_ref, wout_ref, laml_ref, dsk_ref, y_ref, xloc_s, xin_s, *, nctx, nck):
    c, lc, p, p2 = SSM_GROUP, S5_CHUNK, SSM_STATE, 2 * SSM_STATE
    ub = jnp.concatenate([u_ref[0, :, c1, :] for c1 in range(c)], axis=1).astype(BF16)
    xloc_s[...] = _dot(ub, wst_ref[0])

    nlat = nck - nctx
    order_f = list(range(nlat, nck)) + list(range(nlat))
    order_b = list(range(nck - 1, nlat - 1, -1)) + list(range(nlat - 1, -1, -1))
    m_r, m_i = laml_ref[0, 0:1, :], laml_ref[0, 1:2, :]
    is_fwd = lax.broadcasted_iota(jnp.int32, (1, p2), 1) < p
    x_r = jnp.zeros((1, p2), F32)
    x_i = jnp.zeros((1, p2), F32)
    for kf, kb in zip(order_f, order_b):
        xin_s[kf:kf + 1, 0:p] = x_r[:, 0:p]
        xin_s[kb:kb + 1, p:p2] = x_r[:, p:p2]
        xin_s[kf:kf + 1, p2:p2 + p] = x_i[:, 0:p]
        xin_s[kb:kb + 1, p2 + p:2 * p2] = x_i[:, p:p2]
        loc_r = jnp.where(is_fwd, xloc_s[kf:kf + 1, 0:p2], xloc_s[kb:kb + 1, 0:p2])
        loc_i = jnp.where(is_fwd, xloc_s[kf:kf + 1, p2:2 * p2], xloc_s[kb:kb + 1, p2:2 * p2])
        x_r, x_i = m_r * x_r - m_i * x_i + loc_r, m_r * x_i + m_i * x_r + loc_i

    y = _dot(ub, w_ref[0]) + _dot(xin_s[...].astype(BF16), wout_ref[0])
    for c2 in range(c):
        y_ref[0, :, c2, :] = y[:, c2 * lc:(c2 + 1) * lc] + dsk_ref[0, c2:c2 + 1, :] * u_ref[0, :, c2, :]


def _s5(u_t, w, wst, wout, laml, dsk, nctx):
    b, nck, wd, lc = u_t.shape
    g, c = SSM_GROUPS, SSM_GROUP
    n = c * lc
    kern = functools.partial(_s5_kernel, nctx=nctx, nck=nck)
    gspec = lambda shp: pl.BlockSpec((1,) + shp, lambda gi, bi: (gi,) + (0,) * len(shp))
    io_spec = pl.BlockSpec((1, nck, c, lc), lambda gi, bi: (bi, 0, gi, 0))
    return pl.pallas_call(
        kern,
        grid=(g, b),
        in_specs=[io_spec, gspec((n, n)), gspec((n, 4 * SSM_STATE)), gspec((4 * SSM_STATE, n)),
                  gspec((2, 2 * SSM_STATE)), gspec((c, lc))],
        out_specs=io_spec,
        out_shape=jax.ShapeDtypeStruct(u_t.shape, F32),
        scratch_shapes=[
            pltpu.VMEM((nck, 4 * SSM_STATE), F32),
            pltpu.VMEM((nck, 4 * SSM_STATE), F32),
        ],
        compiler_params=_cparams(2, 48),
        name="s5_scan",
    )(u_t, w, wst, wout, laml, dsk)


def _merge_kernel(x_ref, att_ref, of_ref, ob_ref, hg_ref, yt_ref, gate_ref, g1_ref,
                  wa_ref, wr_ref, ws_ref, wo_ref, wglu_ref, bglu_ref, hn_ref, gm_ref, o_ref):
    r = of_ref[0].astype(F32) + ob_ref[0].astype(F32)
    ms = _dot((r * r).astype(BF16), gm_ref[...])
    g = hg_ref[0].astype(F32)
    yrec = (r * lax.rsqrt(ms + RMS_EPS) * hn_ref[...]) * (g * _sigmoid(g))

    ys = jnp.concatenate([yt_ref[0, j].T for j in range(yt_ref.shape[1])], axis=0)
    z = 0.5 * ys * (1.0 + jnp.tanh(math.sqrt(2.0 / math.pi) * (ys + 0.044715 * (ys * ys * ys))))
    yssm = z * _sigmoid(_dot(z.astype(BF16), wglu_ref[...]) + bglu_ref[...])

    d = D_MODEL
    m = gate_ref[0, :, 0:d].astype(F32) * _dot(att_ref[0], wa_ref[...])
    m = m + gate_ref[0, :, d:2 * d].astype(F32) * _dot(yrec.astype(BF16), wr_ref[...])
    m = m + gate_ref[0, :, 2 * d:3 * d].astype(F32) * _dot(yssm.astype(BF16), ws_ref[...])
    y = _dot(m.astype(BF16), wo_ref[...])
    o_ref[0] = x_ref[0] + g1_ref[0] * y


def _merge(x, att, o_f, o_b, hg, y_t, t_off, gates, g1, wa, wr, ws, wo, wglu, bglu, hn, gm):
    b, t, d = x.shape
    tm = min(512, t)
    off = t_off // tm
    row = lambda bi, i: (bi, i, 0)
    row_off = lambda bi, i: (bi, i + off, 0)
    vec = lambda bi, i: (bi, 0, 0)
    return pl.pallas_call(
        _merge_kernel,
        grid=(b, t // tm),
        in_specs=[
            pl.BlockSpec((1, tm, d), row),
            pl.BlockSpec((1, tm, ATT_WIDTH), row),
            pl.BlockSpec((1, tm, HG_WIDTH), row),
            pl.BlockSpec((1, tm, HG_WIDTH), row),
            pl.BlockSpec((1, tm, HG_WIDTH), row_off),
            pl.BlockSpec((1, tm // S5_CHUNK, SSM_WIDTH, S5_CHUNK), lambda bi, i: (bi, i + off, 0, 0)),
            pl.BlockSpec((1, tm, 3 * d), row_off),
            pl.BlockSpec((1, 1, d), vec),
            _const_spec(wa.shape), _const_spec(wr.shape), _const_spec(ws.shape), _const_spec(wo.shape),
            _const_spec(wglu.shape), _const_spec(bglu.shape), _const_spec(hn.shape), _const_spec(gm.shape),
        ],
        out_specs=pl.BlockSpec((1, tm, d), row),
        out_shape=jax.ShapeDtypeStruct((b, t, d), F32),
        compiler_params=_cparams(2, 48),
        name="merge_branches",
    )(x, att, o_f, o_b, hg, y_t, gates, g1, wa, wr, ws, wo, wglu, bglu, hn, gm)


def _ffn_kernel(x_ref, sh_ref, a_ref, g_ref, wup_ref, wdn_ref, o_ref, *, nj):
    x = x_ref[0]
    ms = jnp.mean(x * x, axis=-1, keepdims=True)
    hb = ((x * lax.rsqrt(ms + RMS_EPS)) * a_ref[0] + sh_ref[0]).astype(BF16)
    f = FFN_HIDDEN
    fc = f // nj
    acc = None
    for j in range(nj):
        a = _dot(hb, wup_ref[:, j * fc:(j + 1) * fc])
        bgate = _dot(hb, wup_ref[:, f + j * fc:f + (j + 1) * fc])
        act = ((a * _sigmoid(a)) * bgate).astype(BF16)
        part = _dot(act, wdn_ref[j * fc:(j + 1) * fc, :])
        acc = part if acc is None else acc + part
    o_ref[0] = x + g_ref[0] * acc


def _ffn(x, sh, a, g, wup, wdn):
    b, t, d = x.shape
    tm = min(512, t)
    row = lambda bi, i: (bi, i, 0)
    vec = lambda bi, i: (bi, 0, 0)
    return pl.pallas_call(
        functools.partial(_ffn_kernel, nj=2),
        grid=(b, t // tm),
        in_specs=[
            pl.BlockSpec((1, tm, d), row),
            pl.BlockSpec((1, 1, d), vec), pl.BlockSpec((1, 1, d), vec), pl.BlockSpec((1, 1, d), vec),
            _const_spec(wup.shape), _const_spec(wdn.shape),
        ],
        out_specs=pl.BlockSpec((1, tm, d), row),
        out_shape=jax.ShapeDtypeStruct((b, t, d), F32),
        compiler_params=_cparams(2, 56),
        name="swiglu_ffn",
    )(x, sh, a, g, wup, wdn)


def _rope_tables(t, t_ctx):
    pos = jnp.arange(t)
    row = (pos // GRID_W).astype(F32)
    col = (pos % GRID_W).astype(F32)
    axis_dim = HEAD_DIM // 2
    inv = ROPE_THETA ** (-jnp.arange(0, axis_dim, 2, dtype=F32) / axis_dim)
    ang_r, ang_c = row[:, None] * inv, col[:, None] * inv
    cr, sr, cc, sc = jnp.cos(ang_r), jnp.sin(ang_r), jnp.cos(ang_c), jnp.sin(ang_c)
    z = jnp.zeros_like(cr)
    rep = V7X_LANES // HEAD_DIM
    c = jnp.tile(jnp.concatenate([cr, cr, cc, cc], axis=1), (1, rep))
    s1 = jnp.tile(jnp.concatenate([-sr, z, -sc, z], axis=1), (1, rep))
    s2 = jnp.tile(jnp.concatenate([z, sr, z, sc], axis=1), (1, rep))
    pad = lambda a, fill: jnp.concatenate([a, jnp.full((t_ctx, V7X_LANES), fill, F32)], axis=0)
    return pad(c, 1.0), pad(s1, 0.0), pad(s2, 0.0)


def _group_mean_matrix(width, group):
    i = np.arange(width) // group
    return np.where(i[:, None] == i[None, :], 1.0 / group, 0.0).astype(BF16)


def kernel(x, c, ctx, c_ctx, w_mod, b_mod, norm1_g, norm2_g, w_in, q_norm_g, k_norm_g, hgrn_lb, hgrn_norm_g,
           ssm_lam_re, ssm_lam_im, ssm_log_dt, ssm_b_re, ssm_b_im, ssm_c_re, ssm_c_im, ssm_d, w_glu, b_glu,
           w_br_attn, w_br_hgrn, w_br_ssm, w_out, w_ffn_up, w_ffn_down):
    bsz, t, d = x.shape
    t_ctx = ctx.shape[1]
    t_all = t + t_ctx
    depth = w_mod.shape[0]
    assert t % 512 == 0 and t_ctx % INPROJ_TILE == 0 and t % t_ctx == 0 and d == D_MODEL

    lb_soft = jax.nn.softmax(hgrn_lb.astype(F32), axis=0)
    lower_bounds = jnp.cumsum(lb_soft, axis=0) - lb_soft[0]
    rope = _rope_tables(t, t_ctx)
    gmq = _group_mean_matrix(ATT_WIDTH, HEAD_DIM)
    gmk = _group_mean_matrix(ATT_KV_WIDTH, HEAD_DIM)
    gmh = _group_mean_matrix(HG_WIDTH, HG_DK)
    hg_consts = _hgrn_constants()
    cond8 = jnp.zeros((8, d), F32).at[:bsz].set(c).at[bsz].set(c_ctx)
    mods = _modulation(cond8, w_mod, b_mod)

    x_lat, x_ctx = x, ctx
    for l in range(depth):
        with_ctx = l < depth - 1
        ml = mods[l, :bsz].reshape(bsz, ADALN_CHUNKS, 1, d)
        mc = mods[l, bsz].reshape(1, ADALN_CHUNKS, 1, d)
        sh1, sc1, g1, sh2, sc2, g2 = [ml[:, i] for i in range(ADALN_CHUNKS)]
        csh1, csc1, cg1, csh2, csc2, cg2 = [mc[:, i] for i in range(ADALN_CHUNKS)]
        n1, n2 = norm1_g[l].reshape(1, 1, d), norm2_g[l].reshape(1, 1, d)

        qg = (jnp.tile(q_norm_g[l], ATT_HEADS) * (HEAD_DIM ** -0.5 * LOG2E)).reshape(1, ATT_WIDTH)
        kg = jnp.tile(k_norm_g[l], ATT_KV_HEADS).reshape(1, ATT_KV_WIDTH)
        q, k, vt, hq, hf, hi, hg, u_t, gates = _inproj(
            x_lat, x_ctx, sh1, n1 * (1.0 + sc1), csh1, n1 * (1.0 + csc1), w_in[l].astype(BF16), qg, kg, rope, gmq, gmk)

        a_lat = _attention(q, k, vt, 0, t, 0, t_all)
        if with_ctx:
            a_ctx = _attention(q, k, vt, t, t_ctx, t, t_ctx)

        lb2 = lower_bounds[l]
        s_zero = jnp.zeros((bsz, 2, HG_WIDTH, HG_WIDTH), F32)
        of_c, ob_c, s_ctx = _hgrn(hq, hf, hi, lb2, s_zero, hg_consts, t, t_ctx)
        of_l, ob_l, _ = _hgrn(hq, hf, hi, lb2, s_ctx, hg_consts, 0, t)

        dt = jnp.exp(ssm_log_dt[l].astype(F32))[..., None]
        gd = lambda a: jnp.swapaxes(a.astype(F32), 0, 1)
        w_s5, wst, wout, laml = _s5_prep(
            gd(ssm_lam_re[l] * dt), gd(ssm_lam_im[l] * dt), gd(ssm_lam_re[l]), gd(ssm_lam_im[l]),
            gd(ssm_b_re[l]), gd(ssm_b_im[l]), gd(ssm_c_re[l]), gd(ssm_c_im[l]))
        dsk = jnp.broadcast_to(ssm_d[l].astype(F32).reshape(SSM_GROUPS, SSM_GROUP, 1), (SSM_GROUPS, SSM_GROUP, S5_CHUNK))
        y_t = _s5(u_t, w_s5, wst, wout, laml, dsk, t_ctx // S5_CHUNK)

        hn = jnp.tile(hgrn_norm_g[l], HG_HEADS).reshape(1, HG_WIDTH)
        mw = (w_br_attn[l].astype(BF16), w_br_hgrn[l].astype(BF16), w_br_ssm[l].astype(BF16), w_out[l].astype(BF16),
              w_glu[l].astype(BF16), b_glu[l].reshape(1, SSM_WIDTH), hn, gmh)
        wup, wdn = w_ffn_up[l].astype(BF16), w_ffn_down[l].astype(BF16)
        x_lat = _merge(x_lat, a_lat, of_l, ob_l, hg, y_t, 0, gates, g1, *mw)
        x_lat = _ffn(x_lat, sh2, n2 * (1.0 + sc2), g2, wup, wdn)
        if with_ctx:
            bc = lambda v: jnp.broadcast_to(v, (bsz, 1, d))
            x_ctx = _merge(x_ctx, a_ctx, of_c, ob_c, hg, y_t, t, gates, bc(cg1), *mw)
            x_ctx = _ffn(x_ctx, bc(csh2), bc(n2 * (1.0 + csc2)), bc(cg2), wup, wdn)
    return x_lat
```

```python
import functools
import math

import jax
import jax.numpy as jnp
import numpy as np
from jax import lax
from jax.experimental import pallas as pl
from jax.experimental.pallas import tpu as pltpu

F32 = jnp.float32
BF16 = jnp.bfloat16

D_MODEL = 1024
GRID_W = 64
RMS_EPS = 1e-6
ADALN_CHUNKS = 6
ATT_HEADS = 8
ATT_KV_HEADS = 2
ATT_GROUP = ATT_HEADS // ATT_KV_HEADS
HEAD_DIM = 64
ATT_WIDTH = ATT_HEADS * HEAD_DIM
ATT_KV_WIDTH = ATT_KV_HEADS * HEAD_DIM
ROPE_THETA = 10000.0
HG_HEADS = 4
HG_DK = 64
HG_WIDTH = HG_HEADS * HG_DK
SSM_WIDTH = 256
SSM_GROUP = 16
SSM_GROUPS = SSM_WIDTH // SSM_GROUP
SSM_STATE = 64
FFN_HIDDEN = 2816
N_IN = 5376
O_Q, O_K, O_V, O_HQ, O_HF, O_HI, O_HG, O_U, O_GATE = 0, 512, 640, 768, 1024, 1536, 1792, 2048, 2304

V7X_LANES = 128
V7X_VMEM_BYTES = 64 * 1024 * 1024
MIB = 1024 * 1024

INPROJ_TILE = 256
ATTN_Q_TILE = 128
ATTN_V_ROWS = 80
FFN_HIDDEN_CHUNKS = 1
HG_TILE = 128
S5_CHUNK = 128
LOG2E = math.log2(math.e)


def _dot(a, b):
    return jnp.dot(a, b, preferred_element_type=F32)


def _dot_nt(a, b):
    return lax.dot_general(a, b, (((1,), (1,)), ((), ())), preferred_element_type=F32)


def _dot_tn(a, b):
    return lax.dot_general(a, b, (((0,), (0,)), ((), ())), preferred_element_type=F32)


def _split(x):
    hi = x.astype(BF16)
    lo = (x - hi.astype(F32)).astype(BF16)
    return hi, lo


def _dot3(a, b):
    ah, al = _split(a)
    bh, bl = _split(b)
    return _dot(ah, bh) + (_dot(ah, bl) + _dot(al, bh))


def _sigmoid(x):
    return jax.nn.sigmoid(x)


def _cparams(n_axes, vmem_mib):
    return pltpu.CompilerParams(
        dimension_semantics=("arbitrary",) * n_axes,
        vmem_limit_bytes=min(vmem_mib * MIB, V7X_VMEM_BYTES - 4 * MIB),
    )


def _const_spec(shape):
    nd = len(shape)
    return pl.BlockSpec(shape, lambda *_: (0,) * nd, pipeline_mode=pl.Buffered(1))


def _mod_kernel(c_ref, w_ref, b_ref, o_ref):
    c = c_ref[...]
    s = c * _sigmoid(c)
    o_ref[0] = _dot3(s, w_ref[0]) + b_ref[0]


def _modulation(cond8, w_mod, b_mod):
    n_layers, d, n = w_mod.shape
    nb = 1536
    return pl.pallas_call(
        _mod_kernel,
        grid=(n_layers, n // nb),
        in_specs=[
            pl.BlockSpec((8, d), lambda l, j: (0, 0)),
            pl.BlockSpec((1, d, nb), lambda l, j: (l, 0, j)),
            pl.BlockSpec((1, 1, nb), lambda l, j: (l, 0, j)),
        ],
        out_specs=pl.BlockSpec((1, 8, nb), lambda l, j: (l, 0, j)),
        out_shape=jax.ShapeDtypeStruct((n_layers, 8, n), F32),
        compiler_params=_cparams(2, 40),
        name="adaln_modulation",
    )(cond8, w_mod, b_mod.reshape(n_layers, 1, n))


def _rope128(x, c, s1, s2):
    return x * c + pltpu.roll(x, V7X_LANES - 16, 1) * s1 + pltpu.roll(x, 16, 1) * s2


def _inproj_kernel(x_ref, cx_ref, sh_ref, a_ref, csh_ref, ca_ref, w_ref, qg_ref, kg_ref, c_ref, s1_ref, s2_ref,
                   gmq_ref, gmk_ref, q_ref, k_ref, vt_ref, hq_ref, hf_ref, hi_ref, hg_ref, ut_ref, gate_ref, *, n_lat):
    is_ctx = pl.program_id(1) >= n_lat
    x = jnp.where(is_ctx, cx_ref[0], x_ref[0])
    a = jnp.where(is_ctx, ca_ref[0], a_ref[0])
    sh = jnp.where(is_ctx, csh_ref[0], sh_ref[0])
    ms = jnp.mean(x * x, axis=-1, keepdims=True)
    h = (x * lax.rsqrt(ms + RMS_EPS)) * a + sh
    hb = h.astype(BF16)

    def proj(lo, hi):
        return _dot(hb, w_ref[:, lo:hi])

    c, s1, s2 = c_ref[...], s1_ref[...], s2_ref[...]

    zq = proj(O_Q, O_K)
    msq = _dot((zq * zq).astype(BF16), gmq_ref[...])
    qn = zq * lax.rsqrt(msq + RMS_EPS) * qg_ref[...]
    for j in range(ATT_WIDTH // V7X_LANES):
        sl = slice(j * V7X_LANES, (j + 1) * V7X_LANES)
        q_ref[0, :, sl] = _rope128(qn[:, sl], c, s1, s2).astype(BF16)

    zkv = proj(O_K, O_HQ)
    zk = zkv[:, 0:ATT_KV_WIDTH]
    msk = _dot((zk * zk).astype(BF16), gmk_ref[...])
    kn = _rope128(zk * lax.rsqrt(msk + RMS_EPS) * kg_ref[...], c, s1, s2)
    vt = zkv[:, ATT_KV_WIDTH:].T
    tm = x.shape[0]
    ones_row = lax.broadcasted_iota(jnp.int32, (ATTN_V_ROWS - HEAD_DIM, tm), 0) == 0
    for hd in range(ATT_KV_HEADS):
        k_ref[0, hd] = kn[:, hd * HEAD_DIM:(hd + 1) * HEAD_DIM].astype(BF16)
        vt_ref[0, hd, 0:HEAD_DIM, :] = vt[hd * HEAD_DIM:(hd + 1) * HEAD_DIM, :].astype(BF16)
        vt_ref[0, hd, HEAD_DIM:ATTN_V_ROWS, :] = jnp.where(ones_row, 1.0, 0.0).astype(BF16)

    hq_ref[0] = proj(O_HQ, O_HF).astype(BF16)
    hf_ref[0] = proj(O_HF, O_HI)
    hi_ref[0] = proj(O_HI, O_HG).astype(BF16)
    hg_ref[0] = proj(O_HG, O_U).astype(BF16)
    ut = proj(O_U, O_GATE).T
    for j in range(tm // S5_CHUNK):
        ut_ref[0, j] = ut[:, j * S5_CHUNK:(j + 1) * S5_CHUNK]
    for j in range(3):
        lo = O_GATE + j * D_MODEL
        gate_ref[0, :, j * D_MODEL:(j + 1) * D_MODEL] = _sigmoid(proj(lo, lo + D_MODEL)).astype(BF16)


def _inproj(x, cx, sh, a, csh, ca, w_in, qg, kg, rope, gmq, gmk):
    b, t, d = x.shape
    t_ctx = cx.shape[1]
    tm = INPROJ_TILE
    n_lat, n_ctx = t // tm, t_ctx // tm
    t_all = t + t_ctx
    c, s1, s2 = rope
    row = lambda bi, i: (bi, i, 0)
    vec = lambda bi, i: (bi, 0, 0)
    tab = lambda bi, i: (i, 0)

    def widths(specs):
        shapes = [jax.ShapeDtypeStruct((b, t_all, w), dt) for w, dt in specs]
        return shapes, [pl.BlockSpec((1, tm, w), row) for w, _ in specs]

    q_shape, q_spec = widths([(ATT_WIDTH, BF16)])
    h_shape, h_spec = widths([(HG_WIDTH, BF16), (2 * HG_WIDTH, F32), (HG_WIDTH, BF16), (HG_WIDTH, BF16)])
    g_shape, g_spec = widths([(3 * D_MODEL, BF16)])
    out_shape = q_shape + [
        jax.ShapeDtypeStruct((b, ATT_KV_HEADS, t_all, HEAD_DIM), BF16),
        jax.ShapeDtypeStruct((b, ATT_KV_HEADS, ATTN_V_ROWS, t_all), BF16),
    ] + h_shape + [jax.ShapeDtypeStruct((b, t_all // S5_CHUNK, SSM_WIDTH, S5_CHUNK), F32)] + g_shape
    out_specs = q_spec + [
        pl.BlockSpec((1, ATT_KV_HEADS, tm, HEAD_DIM), lambda bi, i: (bi, 0, i, 0)),
        pl.BlockSpec((1, ATT_KV_HEADS, ATTN_V_ROWS, tm), lambda bi, i: (bi, 0, 0, i)),
    ] + h_spec + [pl.BlockSpec((1, tm // S5_CHUNK, SSM_WIDTH, S5_CHUNK), lambda bi, i: (bi, i, 0, 0))] + g_spec
    return pl.pallas_call(
        functools.partial(_inproj_kernel, n_lat=n_lat),
        grid=(b, n_lat + n_ctx),
        in_specs=[
            pl.BlockSpec((1, tm, d), lambda bi, i: (bi, jnp.minimum(i, n_lat - 1), 0)),
            pl.BlockSpec((1, tm, d), lambda bi, i: (bi, jnp.maximum(i - n_lat, 0), 0)),
            pl.BlockSpec((1, 1, d), vec),
            pl.BlockSpec((1, 1, d), vec),
            pl.BlockSpec((1, 1, d), lambda bi, i: (0, 0, 0)),
            pl.BlockSpec((1, 1, d), lambda bi, i: (0, 0, 0)),
            _const_spec((d, N_IN)),
            _const_spec((1, ATT_WIDTH)),
            _const_spec((1, ATT_KV_WIDTH)),
            pl.BlockSpec((tm, V7X_LANES), tab),
            pl.BlockSpec((tm, V7X_LANES), tab),
            pl.BlockSpec((tm, V7X_LANES), tab),
            _const_spec((ATT_WIDTH, ATT_WIDTH)),
            _const_spec((ATT_KV_WIDTH, ATT_KV_WIDTH)),
        ],
        out_specs=out_specs,
        out_shape=out_shape,
        compiler_params=_cparams(2, 56),
        name="in_projection",
    )(x, cx, sh, a, csh, ca, w_in, qg, kg, c, s1, s2, gmq, gmk)


def _attn_kernel(q_ref, k_ref, vt_ref, o_ref, qt_ref, m_ref, acc_ref, sa_ref, sb_ref, mxa_ref, mxb_ref, *, tk, nkb):
    tq = q_ref.shape[1]
    qt = q_ref[0].astype(F32).T
    qt_ref[...] = jnp.concatenate(
        [qt[HEAD_DIM * g:HEAD_DIM * (g + 1), :] for g in range(ATT_GROUP)], axis=1).astype(BF16)
    m_ref[...] = jnp.full(m_ref.shape, -jnp.inf, F32)
    acc_ref[...] = jnp.zeros(acc_ref.shape, F32)

    def scores(kb, s_ref, mx_ref):
        off = pl.multiple_of(kb * tk, tk)
        s = _dot(k_ref[0, 0, pl.ds(off, tk), :], qt_ref[...])
        s_ref[...] = s
        mx_ref[...] = jnp.max(s, axis=0, keepdims=True)

    def consume(kb, s_ref, mx_ref):
        off = pl.multiple_of(kb * tk, tk)
        m_prev = m_ref[...]
        m_new = jnp.maximum(m_prev, mx_ref[...])
        alpha = jnp.exp2(m_prev - m_new)
        p = jnp.exp2(s_ref[...] - m_new).astype(BF16)
        acc_ref[...] = alpha * acc_ref[...] + _dot(vt_ref[0, 0, :, pl.ds(off, tk)], p)
        m_ref[...] = m_new

    bufs = ((sa_ref, mxa_ref), (sb_ref, mxb_ref))
    scores(0, *bufs[0])

    def body(i, carry):
        base = ATTN_BLOCKS_PER_ITER * i
        for j in range(ATTN_BLOCKS_PER_ITER):
            scores(base + j + 1, *bufs[(j + 1) % 2])
            consume(base + j, *bufs[j % 2])
        return carry

    n_it = (nkb - 1) // ATTN_BLOCKS_PER_ITER
    lax.fori_loop(0, n_it, body, 0)
    for r in range(n_it * ATTN_BLOCKS_PER_ITER, nkb):
        if r + 1 < nkb:
            scores(r + 1, *bufs[(r + 1) % 2])
        consume(r, *bufs[r % 2])

    acc = acc_ref[...]
    out_t = acc[0:HEAD_DIM, :] / acc[HEAD_DIM:HEAD_DIM + 1, :]
    out_t = jnp.concatenate([out_t[:, g * tq:(g + 1) * tq] for g in range(ATT_GROUP)], axis=0)
    o_ref[0] = out_t.T.astype(BF16)


ATTN_KEY_BLOCKS = (640, 512, 256, 128)
ATTN_BLOCKS_PER_ITER = 12


def _attention(q, k, vt_ext, q_start, q_len, k_start, k_len):
    b = q.shape[0]
    tq = ATTN_Q_TILE
    tk = next(c for c in ATTN_KEY_BLOCKS if k_len % c == 0)
    assert q_start % tq == 0 and q_len % tq == 0 and k_start % k_len == 0
    q_off, k_blk = q_start // tq, k_start // k_len
    kern = functools.partial(_attn_kernel, tk=tk, nkb=k_len // tk)
    gw = ATT_GROUP * HEAD_DIM
    m = ATT_GROUP * tq
    return pl.pallas_call(
        kern,
        grid=(b, ATT_KV_HEADS, q_len // tq),
        in_specs=[
            pl.BlockSpec((1, tq, gw), lambda bi, h, i: (bi, i + q_off, h)),
            pl.BlockSpec((1, 1, k_len, HEAD_DIM), lambda bi, h, i: (bi, h, k_blk, 0)),
            pl.BlockSpec((1, 1, ATTN_V_ROWS, k_len), lambda bi, h, i: (bi, h, 0, k_blk)),
        ],
        out_specs=pl.BlockSpec((1, tq, gw), lambda bi, h, i: (bi, i, h)),
        out_shape=jax.ShapeDtypeStruct((b, q_len, ATT_WIDTH), BF16),
        scratch_shapes=[
            pltpu.VMEM((HEAD_DIM, m), BF16),
            pltpu.VMEM((1, m), F32),
            pltpu.VMEM((ATTN_V_ROWS, m), F32),
            pltpu.VMEM((tk, m), F32),
            pltpu.VMEM((tk, m), F32),
            pltpu.VMEM((1, m), F32),
            pltpu.VMEM((1, m), F32),
        ],
        compiler_params=_cparams(3, 48),
        name="gqa_attention",
    )(q, k, vt_ext)


HG_LEVELS = 7
HG_SEL_LEVELS = 1


def _hgrn_kernel(qf_ref, ff_ref, vf_ref, qb_ref, fb_ref, vb_ref, lb_ref, s0_ref,
                 tri_ref, sel_ref, sm_ref, hm_ref, hmt_ref, gsum_ref, bd_ref,
                 of_ref, ob_ref, sfin_ref, stf_ref, stb_ref, cf_s, cb_s):
    @pl.when(pl.program_id(1) == 0)
    def _():
        stf_ref[...] = s0_ref[0, 0]
        stb_ref[...] = s0_ref[0, 1]

    tt, w = HG_TILE, HG_WIDTH
    dirs = (0, 1)
    q_refs, f_refs, v_refs = (qf_ref, qb_ref), (ff_ref, fb_ref), (vf_ref, vb_ref)
    o_refs, st_refs, c_refs = (of_ref, ob_ref), (stf_ref, stb_ref), (cf_s, cb_s)
    hm = [hm_ref[h] for h in range(HG_HEADS)]
    hmt = [hmt_ref[h] for h in range(HG_HEADS)]

    qs, kin, vb, c, tot, ref_small, o, scores = [], [], [], [], [], [], [], [None, None]
    for d in dirs:
        q = q_refs[d][0].astype(F32)
        fpre = f_refs[d][0]
        lb = lb_ref[d:d + 1, :]
        qs.append(q * _sigmoid(q))
        kin.append((1.0 - lb) * _sigmoid(-fpre))
        vb.append(v_refs[d][0])
        hi, lo = _split(jnp.log(lb + (1.0 - lb) * _sigmoid(fpre)))
        cd = (_dot(tri_ref[d], hi) + _dot(tri_ref[d], lo)) * LOG2E
        c_refs[d][...] = cd
        c.append(cd)
    for d in dirs:
        last = 0 if d else tt - 1
        tot.append(c_refs[d][last:last + 1, :])
        chi, clo = _split(c[d])
        ref_small.append(_dot(sel_ref[d], chi) + _dot(sel_ref[d], clo))
        o.append(_dot((qs[d] * kin[d]).astype(BF16), gsum_ref[...]) * vb[d].astype(F32))

    for lvl in range(HG_LEVELS):
        m = 1 << lvl
        for d in dirs:
            if lvl < HG_SEL_LEVELS:
                cref = ref_small[d][lvl * tt:(lvl + 1) * tt, :]
            else:
                rows = []
                for blk in range(tt // (2 * m)):
                    r = blk * 2 * m + (m if d else m - 1)
                    rows.append(jnp.broadcast_to(c_refs[d][r:r + 1, :], (2 * m, w)))
                cref = rows[0] if len(rows) == 1 else jnp.concatenate(rows, axis=0)
            wgt = jnp.exp2(-jnp.abs(c[d] - cref))
            ql = (qs[d] * wgt).astype(BF16)
            kl_t = (kin[d] * wgt).T.astype(BF16)
            kstack_t = jnp.concatenate([kl_t * hmt[h] for h in range(HG_HEADS)], axis=1)
            sc = _dot(ql, kstack_t).astype(BF16) * sm_ref[d, lvl]
            scores[d] = sc if scores[d] is None else scores[d] + sc

    for d in dirs:
        vstack = jnp.concatenate([vb[d] * hm[h] for h in range(HG_HEADS)], axis=0)
        o[d] = o[d] + _dot(scores[d], vstack)
    for d in dirs:
        st = st_refs[d][...]
        o[d] = o[d] + _dot_nt((qs[d] * jnp.exp2(c[d])).astype(BF16), st.astype(BF16))
        kv = _dot_tn(vb[d], (kin[d] * jnp.exp2(tot[d] - c[d])).astype(BF16))
        st_new = jnp.exp2(tot[d]) * st + kv * bd_ref[...]
        st_refs[d][...] = st_new
        sfin_ref[0, d] = st_new
        o_refs[d][0] = o[d].astype(BF16)


def _hgrn_constants():
    tt, w = HG_TILE, HG_WIDTH
    t = np.arange(tt)
    tri = np.stack([t[None, :] <= t[:, None], t[None, :] >= t[:, None]]).astype(BF16)
    sel = []
    for reverse in (False, True):
        per = []
        for lvl in range(HG_SEL_LEVELS):
            m = 1 << lvl
            r = (t // (2 * m)) * (2 * m) + (m if reverse else m - 1)
            per.append(t[None, :] == r[:, None])
        sel.append(np.concatenate(per, axis=0))
    sel = np.stack(sel).astype(BF16)
    col = np.arange(HG_HEADS * tt) % tt
    sm = []
    for reverse in (False, True):
        per = []
        for lvl in range(HG_LEVELS):
            t_up, s_up = ((t >> lvl) & 1) == 1, ((col >> lvl) & 1) == 1
            same = (t[:, None] >> (lvl + 1)) == (col[None, :] >> (lvl + 1))
            halves = (~t_up[:, None] & s_up[None, :]) if reverse else (t_up[:, None] & ~s_up[None, :])
            per.append(same & halves)
        sm.append(np.stack(per))
    sm = np.stack(sm).astype(BF16)
    lane_head = np.arange(w) // HG_DK
    hm = np.stack([np.broadcast_to((lane_head == h)[None, :], (tt, w)) for h in range(HG_HEADS)]).astype(BF16)
    same_head = lane_head[:, None] == lane_head[None, :]
    hmt = np.ascontiguousarray(np.swapaxes(hm, 1, 2))
    return tri, sel, sm, hm, hmt, same_head.astype(BF16), same_head.astype(F32)


def _hgrn(hq, hf, hv, lb2, s0, consts, start, t):
    b, _, w = hq.shape
    tt = HG_TILE
    nt, off = t // tt, start // tt
    fwd = lambda bi, i: (bi, off + i, 0)
    bwd = lambda bi, i: (bi, off + nt - 1 - i, 0)
    bwd_f = lambda bi, i: (bi, off + nt - 1 - i, 1)
    fwd_o = lambda bi, i: (bi, i, 0)
    bwd_o = lambda bi, i: (bi, nt - 1 - i, 0)
    st_spec = pl.BlockSpec((1, 2, w, w), lambda bi, i: (bi, 0, 0, 0))
    return pl.pallas_call(
        _hgrn_kernel,
        grid=(b, nt),
        in_specs=[
            pl.BlockSpec((1, tt, w), fwd), pl.BlockSpec((1, tt, w), fwd), pl.BlockSpec((1, tt, w), fwd),
            pl.BlockSpec((1, tt, w), bwd), pl.BlockSpec((1, tt, w), bwd_f), pl.BlockSpec((1, tt, w), bwd),
            pl.BlockSpec((2, w), lambda bi, i: (0, 0)),
            st_spec,
        ] + [_const_spec(a.shape) for a in consts],
        out_specs=[pl.BlockSpec((1, tt, w), fwd_o), pl.BlockSpec((1, tt, w), bwd_o), st_spec],
        out_shape=[
            jax.ShapeDtypeStruct((b, t, w), BF16),
            jax.ShapeDtypeStruct((b, t, w), BF16),
            jax.ShapeDtypeStruct((b, 2, w, w), F32),
        ],
        scratch_shapes=[
            pltpu.VMEM((w, w), F32),
            pltpu.VMEM((w, w), F32),
            pltpu.VMEM((tt, w), F32),
            pltpu.VMEM((tt, w), F32),
        ],
        compiler_params=_cparams(2, 32),
        name="hgrn2_scan",
    )(hq, hf, hv, hq, hf, hv, lb2, s0, *consts)


def _cpow(a_re, a_im, tau):
    mag = jnp.exp(tau * a_re)
    ang = tau * a_im
    return mag * jnp.cos(ang), mag * jnp.sin(ang)


def _s5_prep_kernel(ar_row, ai_row, ar_col, ai_col, lr_row, li_row, bt_r, bt_i, cr, ci, ctr, cti,
                    w_ref, wst_ref, wout_ref, laml_ref, kall_ref):
    lc, p, c = S5_CHUNK, SSM_STATE, SSM_GROUP
    tau_l = lax.broadcasted_iota(jnp.int32, (p, lc), 1).astype(F32)
    tau_s = lax.broadcasted_iota(jnp.int32, (lc, p), 0).astype(F32)

    btr, bti = [], []
    for d in range(2):
        e_r, e_i = _cpow(ar_row[0, d], ai_row[0, d], 1.0)
        l_r, l_i = lr_row[0, d], li_row[0, d]
        den = l_r * l_r + l_i * l_i
        f_r = ((e_r - 1.0) * l_r + e_i * l_i) / den
        f_i = (e_i * l_r - (e_r - 1.0) * l_i) / den
        btr.append(f_r * bt_r[0, d] - f_i * bt_i[0, d])
        bti.append(f_r * bt_i[0, d] + f_i * bt_r[0, d])

    def cb(d):
        re, im = [], []
        for c1 in range(c):
            b_r, b_i = btr[d][c1:c1 + 1, :], bti[d][c1:c1 + 1, :]
            re.append(b_r * cr[0, d] - b_i * ci[0, d])
            im.append(b_r * ci[0, d] + b_i * cr[0, d])
        return jnp.concatenate(re, axis=0), jnp.concatenate(im, axis=0)

    cbf_r, cbf_i = cb(0)
    pf_r, pf_i = _cpow(ar_col[0, 0], ai_col[0, 0], tau_l)
    kf = _dot3(cbf_r, pf_r) - _dot3(cbf_i, pf_i)
    cbb_r, cbb_i = cb(1)
    pb_r, pb_i = _cpow(ar_col[0, 1], ai_col[0, 1], lc - tau_l)
    kb = _dot3(cbb_r, pb_r) - _dot3(cbb_i, pb_i)
    lane = lax.broadcasted_iota(jnp.int32, (c * c, lc), 1)
    kf = kf + jnp.where(lane == 0, jnp.sum(cbb_r, axis=1, keepdims=True), 0.0)
    kall_ref[...] = jnp.concatenate([kf, kb], axis=1)

    def toeplitz_rows(c1, carry):
        for c2 in range(c):
            row = kall_ref[pl.ds(c1 * c + c2, 1), :]
            blk = pltpu.roll(jnp.broadcast_to(row, (lc, 2 * lc)), 0, 1, stride=1, stride_axis=0)
            w_ref[0, pl.ds(pl.multiple_of(c1 * lc, lc), lc), c2 * lc:(c2 + 1) * lc] = blk[:, :lc].astype(BF16)
        return carry

    lax.fori_loop(0, c, toeplitz_rows, 0)

    sf_r, sf_i = _cpow(ar_row[0, 0], ai_row[0, 0], (lc - 1) - tau_s)
    sb_r, sb_i = _cpow(ar_row[0, 1], ai_row[0, 1], tau_s)
    for c1 in range(c):
        re, im = [], []
        for d, (p_r, p_i) in enumerate(((sf_r, sf_i), (sb_r, sb_i))):
            b_r, b_i = btr[d][c1:c1 + 1, :], bti[d][c1:c1 + 1, :]
            re.append(p_r * b_r - p_i * b_i)
            im.append(p_r * b_i + p_i * b_r)
        wst_ref[0, c1 * lc:(c1 + 1) * lc, :] = jnp.concatenate(re + im, axis=1).astype(BF16)

    of_r, of_i = _cpow(ar_col[0, 0], ai_col[0, 0], tau_l + 1.0)
    ob_r, ob_i = _cpow(ar_col[0, 1], ai_col[0, 1], lc - tau_l)
    for c2 in range(c):
        re, im = [], []
        for d, (p_r, p_i) in enumerate(((of_r, of_i), (ob_r, ob_i))):
            c_r, c_i = ctr[0, d, :, c2:c2 + 1], cti[0, d, :, c2:c2 + 1]
            re.append(c_r * p_r - c_i * p_i)
            im.append(-(c_r * p_i + c_i * p_r))
        wout_ref[0, :, c2 * lc:(c2 + 1) * lc] = jnp.concatenate(re + im, axis=0).astype(BF16)

    lf_r, lf_i = _cpow(ar_row[0, 0], ai_row[0, 0], float(lc))
    lb_r, lb_i = _cpow(ar_row[0, 1], ai_row[0, 1], float(lc))
    laml_ref[0, 0:1, :] = jnp.concatenate([lf_r, lb_r], axis=1)
    laml_ref[0, 1:2, :] = jnp.concatenate([lf_i, lb_i], axis=1)


def _s5_prep(a_re, a_im, lam_re, lam_im, b_re, b_im, c_re, c_im):
    g, _, p = a_re.shape
    c, lc = SSM_GROUP, S5_CHUNK
    n = c * lc
    row = lambda x: x.reshape(g, 2, 1, p)
    col = lambda x: x.reshape(g, 2, p, 1)
    tr = lambda x: jnp.swapaxes(x, -1, -2)
    args = [row(a_re), row(a_im), col(a_re), col(a_im), row(lam_re), row(lam_im),
            tr(b_re), tr(b_im), c_re, c_im, tr(c_re), tr(c_im)]
    spec4 = lambda shp: pl.BlockSpec((1,) + shp, lambda gi: (gi, 0, 0, 0))
    spec3 = lambda shp: pl.BlockSpec((1,) + shp, lambda gi: (gi, 0, 0))
    return pl.pallas_call(
        _s5_prep_kernel,
        grid=(g,),
        in_specs=[spec4(a.shape[1:]) for a in args],
        out_specs=[spec3((n, n)), spec3((n, 4 * p)), spec3((4 * p, n)), spec3((2, 2 * p))],
        out_shape=[
            jax.ShapeDtypeStruct((g, n, n), BF16),
            jax.ShapeDtypeStruct((g, n, 4 * p), BF16),
            jax.ShapeDtypeStruct((g, 4 * p, n), BF16),
            jax.ShapeDtypeStruct((g, 2, 2 * p), F32),
        ],
        scratch_shapes=[pltpu.VMEM((c * c, 2 * lc), F32)],
        compiler_params=_cparams(1, 48),
        name="s5_weights",
    )(*args)


def _s5_kernel(u_ref, w_ref, wst_ref, wout_ref, laml_ref, dsk_ref, y_ref, xloc_s, xin_s, *, nctx, nck):
    c, lc, p, p2 = SSM_GROUP, S5_CHUNK, SSM_STATE, 2 * SSM_STATE
    ub = jnp.concatenate([u_ref[0, :, c1, :] for c1 in range(c)], axis=1).astype(BF16)
    xloc_s[...] = _dot(ub, wst_ref[0])

    nlat = nck - nctx
    order_f = list(range(nlat, nck)) + list(range(nlat))
    order_b = list(range(nck - 1, nlat - 1, -1)) + list(range(nlat - 1, -1, -1))
    m_r, m_i = laml_ref[0, 0:1, :], laml_ref[0, 1:2, :]
    is_fwd = lax.broadcasted_iota(jnp.int32, (1, p2), 1) < p
    x_r = jnp.zeros((1, p2), F32)
    x_i = jnp.zeros((1, p2), F32)
    for kf, kb in zip(order_f, order_b):
        xin_s[kf:kf + 1, 0:p] = x_r[:, 0:p]
        xin_s[kb:kb + 1, p:p2] = x_r[:, p:p2]
        xin_s[kf:kf + 1, p2:p2 + p] = x_i[:, 0:p]
        xin_s[kb:kb + 1, p2 + p:2 * p2] = x_i[:, p:p2]
        loc_r = jnp.where(is_fwd, xloc_s[kf:kf + 1, 0:p2], xloc_s[kb:kb + 1, 0:p2])
        loc_i = jnp.where(is_fwd, xloc_s[kf:kf + 1, p2:2 * p2], xloc_s[kb:kb + 1, p2:2 * p2])
        x_r, x_i = m_r * x_r - m_i * x_i + loc_r, m_r * x_i + m_i * x_r + loc_i

    y = _dot(ub, w_ref[0]) + _dot(xin_s[...].astype(BF16), wout_ref[0])
    for c2 in range(c):
        y_ref[0, :, c2, :] = y[:, c2 * lc:(c2 + 1) * lc] + dsk_ref[0, c2:c2 + 1, :] * u_ref[0, :, c2, :]


def _s5(u_t, w, wst, wout, laml, dsk, nctx):
    b, nck, wd, lc = u_t.shape
    g, c = SSM_GROUPS, SSM_GROUP
    n = c * lc
    kern = functools.partial(_s5_kernel, nctx=nctx, nck=nck)
    gspec = lambda shp: pl.BlockSpec((1,) + shp, lambda gi, bi: (gi,) + (0,) * len(shp))
    io_spec = pl.BlockSpec((1, nck, c, lc), lambda gi, bi: (bi, 0, gi, 0))
    return pl.pallas_call(
        kern,
        grid=(g, b),
        in_specs=[io_spec, gspec((n, n)), gspec((n, 4 * SSM_STATE)), gspec((4 * SSM_STATE, n)),
                  gspec((2, 2 * SSM_STATE)), gspec((c, lc))],
        out_specs=io_spec,
        out_shape=jax.ShapeDtypeStruct(u_t.shape, F32),
        scratch_shapes=[
            pltpu.VMEM((nck, 4 * SSM_STATE), F32),
            pltpu.VMEM((nck, 4 * SSM_STATE), F32),
        ],
        compiler_params=_cparams(2, 48),
        name="s5_scan",
    )(u_t, w, wst, wout, laml, dsk)


def _merge_kernel(x_ref, att_ref, of_ref, ob_ref, hg_ref, yt_ref, gate_ref, g1_ref,
                  wa_ref, wr_ref, ws_ref, wo_ref, wglu_ref, bglu_ref, hn_ref, gm_ref, o_ref):
    r = of_ref[0].astype(F32) + ob_ref[0].astype(F32)
    ms = _dot((r * r).astype(BF16), gm_ref[...])
    g = hg_ref[0].astype(F32)
    yrec = (r * lax.rsqrt(ms + RMS_EPS) * hn_ref[...]) * (g * _sigmoid(g))

    ys = jnp.concatenate([yt_ref[0, j].T for j in range(yt_ref.shape[1])], axis=0)
    z = 0.5 * ys * (1.0 + jnp.tanh(math.sqrt(2.0 / math.pi) * (ys + 0.044715 * (ys * ys * ys))))
    yssm = z * _sigmoid(_dot(z.astype(BF16), wglu_ref[...]) + bglu_ref[...])

    d = D_MODEL
    m = gate_ref[0, :, 0:d].astype(F32) * _dot(att_ref[0], wa_ref[...])
    m = m + gate_ref[0, :, d:2 * d].astype(F32) * _dot(yrec.astype(BF16), wr_ref[...])
    m = m + gate_ref[0, :, 2 * d:3 * d].astype(F32) * _dot(yssm.astype(BF16), ws_ref[...])
    y = _dot(m.astype(BF16), wo_ref[...])
    o_ref[0] = x_ref[0] + g1_ref[0] * y


def _merge(x, att, o_f, o_b, hg, y_t, t_off, gates, g1, wa, wr, ws, wo, wglu, bglu, hn, gm):
    b, t, d = x.shape
    tm = min(512, t)
    off = t_off // tm
    row = lambda bi, i: (bi, i, 0)
    row_off = lambda bi, i: (bi, i + off, 0)
    vec = lambda bi, i: (bi, 0, 0)
    return pl.pallas_call(
        _merge_kernel,
        grid=(b, t // tm),
        in_specs=[
            pl.BlockSpec((1, tm, d), row),
            pl.BlockSpec((1, tm, ATT_WIDTH), row),
            pl.BlockSpec((1, tm, HG_WIDTH), row),
            pl.BlockSpec((1, tm, HG_WIDTH), row),
            pl.BlockSpec((1, tm, HG_WIDTH), row_off),
            pl.BlockSpec((1, tm // S5_CHUNK, SSM_WIDTH, S5_CHUNK), lambda bi, i: (bi, i + off, 0, 0)),
            pl.BlockSpec((1, tm, 3 * d), row_off),
            pl.BlockSpec((1, 1, d), vec),
            _const_spec(wa.shape), _const_spec(wr.shape), _const_spec(ws.shape), _const_spec(wo.shape),
            _const_spec(wglu.shape), _const_spec(bglu.shape), _const_spec(hn.shape), _const_spec(gm.shape),
        ],
        out_specs=pl.BlockSpec((1, tm, d), row),
        out_shape=jax.ShapeDtypeStruct((b, t, d), F32),
        compiler_params=_cparams(2, 48),
        name="merge_branches",
    )(x, att, o_f, o_b, hg, y_t, gates, g1, wa, wr, ws, wo, wglu, bglu, hn, gm)


def _ffn_kernel(x_ref, sh_ref, a_ref, g_ref, wup_ref, wdn_ref, o_ref, *, nj):
    x = x_ref[0]
    ms = jnp.mean(x * x, axis=-1, keepdims=True)
    hb = ((x * lax.rsqrt(ms + RMS_EPS)) * a_ref[0] + sh_ref[0]).astype(BF16)
    f = FFN_HIDDEN
    fc = f // nj
    acc = None
    for j in range(nj):
        a = _dot(hb, wup_ref[:, j * fc:(j + 1) * fc])
        bgate = _dot(hb, wup_ref[:, f + j * fc:f + (j + 1) * fc])
        act = ((a * _sigmoid(a)) * bgate).astype(BF16)
        part = _dot(act, wdn_ref[j * fc:(j + 1) * fc, :])
        acc = part if acc is None else acc + part
    o_ref[0] = x + g_ref[0] * acc


def _ffn(x, sh, a, g, wup, wdn):
    b, t, d = x.shape
    tm = min(512, t)
    row = lambda bi, i: (bi, i, 0)
    vec = lambda bi, i: (bi, 0, 0)
    return pl.pallas_call(
        functools.partial(_ffn_kernel, nj=FFN_HIDDEN_CHUNKS),
        grid=(b, t // tm),
        in_specs=[
            pl.BlockSpec((1, tm, d), row),
            pl.BlockSpec((1, 1, d), vec), pl.BlockSpec((1, 1, d), vec), pl.BlockSpec((1, 1, d), vec),
            _const_spec(wup.shape), _const_spec(wdn.shape),
        ],
        out_specs=pl.BlockSpec((1, tm, d), row),
        out_shape=jax.ShapeDtypeStruct((b, t, d), F32),
        compiler_params=_cparams(2, 56),
        name="swiglu_ffn",
    )(x, sh, a, g, wup, wdn)


def _rope_tables(t, t_ctx):
    pos = jnp.arange(t)
    row = (pos // GRID_W).astype(F32)
    col = (pos % GRID_W).astype(F32)
    axis_dim = HEAD_DIM // 2
    inv = ROPE_THETA ** (-jnp.arange(0, axis_dim, 2, dtype=F32) / axis_dim)
    ang_r, ang_c = row[:, None] * inv, col[:, None] * inv
    cr, sr, cc, sc = jnp.cos(ang_r), jnp.sin(ang_r), jnp.cos(ang_c), jnp.sin(ang_c)
    z = jnp.zeros_like(cr)
    rep = V7X_LANES // HEAD_DIM
    c = jnp.tile(jnp.concatenate([cr, cr, cc, cc], axis=1), (1, rep))
    s1 = jnp.tile(jnp.concatenate([-sr, z, -sc, z], axis=1), (1, rep))
    s2 = jnp.tile(jnp.concatenate([z, sr, z, sc], axis=1), (1, rep))
    pad = lambda a, fill: jnp.concatenate([a, jnp.full((t_ctx, V7X_LANES), fill, F32)], axis=0)
    return pad(c, 1.0), pad(s1, 0.0), pad(s2, 0.0)


def _group_mean_matrix(width, group):
    i = np.arange(width) // group
    return np.where(i[:, None] == i[None, :], 1.0 / group, 0.0).astype(BF16)


def kernel(x, c, ctx, c_ctx, w_mod, b_mod, norm1_g, norm2_g, w_in, q_norm_g, k_norm_g, hgrn_lb, hgrn_norm_g,
           ssm_lam_re, ssm_lam_im, ssm_log_dt, ssm_b_re, ssm_b_im, ssm_c_re, ssm_c_im, ssm_d, w_glu, b_glu,
           w_br_attn, w_br_hgrn, w_br_ssm, w_out, w_ffn_up, w_ffn_down):
    bsz, t, d = x.shape
    t_ctx = ctx.shape[1]
    t_all = t + t_ctx
    depth = w_mod.shape[0]
    assert t % 512 == 0 and t_ctx % INPROJ_TILE == 0 and t % t_ctx == 0 and d == D_MODEL

    lb_soft = jax.nn.softmax(hgrn_lb.astype(F32), axis=0)
    lower_bounds = jnp.cumsum(lb_soft, axis=0) - lb_soft[0]
    rope = _rope_tables(t, t_ctx)
    gmq = _group_mean_matrix(ATT_WIDTH, HEAD_DIM)
    gmk = _group_mean_matrix(ATT_KV_WIDTH, HEAD_DIM)
    gmh = _group_mean_matrix(HG_WIDTH, HG_DK)
    hg_consts = _hgrn_constants()
    cond8 = jnp.zeros((8, d), F32).at[:bsz].set(c).at[bsz].set(c_ctx)
    mods = _modulation(cond8, w_mod, b_mod)

    x_lat, x_ctx = x, ctx
    for l in range(depth):
        with_ctx = l < depth - 1
        ml = mods[l, :bsz].reshape(bsz, ADALN_CHUNKS, 1, d)
        mc = mods[l, bsz].reshape(1, ADALN_CHUNKS, 1, d)
        sh1, sc1, g1, sh2, sc2, g2 = [ml[:, i] for i in range(ADALN_CHUNKS)]
        csh1, csc1, cg1, csh2, csc2, cg2 = [mc[:, i] for i in range(ADALN_CHUNKS)]
        n1, n2 = norm1_g[l].reshape(1, 1, d), norm2_g[l].reshape(1, 1, d)

        qg = (jnp.tile(q_norm_g[l], ATT_HEADS) * (HEAD_DIM ** -0.5 * LOG2E)).reshape(1, ATT_WIDTH)
        kg = jnp.tile(k_norm_g[l], ATT_KV_HEADS).reshape(1, ATT_KV_WIDTH)
        q, k, vt, hq, hf, hi, hg, u_t, gates = _inproj(
            x_lat, x_ctx, sh1, n1 * (1.0 + sc1), csh1, n1 * (1.0 + csc1), w_in[l].astype(BF16), qg, kg, rope, gmq, gmk)

        a_lat = _attention(q, k, vt, 0, t, 0, t_all)
        if with_ctx:
            a_ctx = _attention(q, k, vt, t, t_ctx, t, t_ctx)

        lb2 = lower_bounds[l]
        s_zero = jnp.zeros((bsz, 2, HG_WIDTH, HG_WIDTH), F32)
        of_c, ob_c, s_ctx = _hgrn(hq, hf, hi, lb2, s_zero, hg_consts, t, t_ctx)
        of_l, ob_l, _ = _hgrn(hq, hf, hi, lb2, s_ctx, hg_consts, 0, t)

        dt = jnp.exp(ssm_log_dt[l].astype(F32))[..., None]
        gd = lambda a: jnp.swapaxes(a.astype(F32), 0, 1)
        w_s5, wst, wout, laml = _s5_prep(
            gd(ssm_lam_re[l] * dt), gd(ssm_lam_im[l] * dt), gd(ssm_lam_re[l]), gd(ssm_lam_im[l]),
            gd(ssm_b_re[l]), gd(ssm_b_im[l]), gd(ssm_c_re[l]), gd(ssm_c_im[l]))
        dsk = jnp.broadcast_to(ssm_d[l].astype(F32).reshape(SSM_GROUPS, SSM_GROUP, 1), (SSM_GROUPS, SSM_GROUP, S5_CHUNK))
        y_t = _s5(u_t, w_s5, wst, wout, laml, dsk, t_ctx // S5_CHUNK)

        hn = jnp.tile(hgrn_norm_g[l], HG_HEADS).reshape(1, HG_WIDTH)
        mw = (w_br_attn[l].astype(BF16), w_br_hgrn[l].astype(BF16), w_br_ssm[l].astype(BF16), w_out[l].astype(BF16),
              w_glu[l].astype(BF16), b_glu[l].reshape(1, SSM_WIDTH), hn, gmh)
        wup, wdn = w_ffn_up[l].astype(BF16), w_ffn_down[l].astype(BF16)
        x_lat = _merge(x_lat, a_lat, of_l, ob_l, hg, y_t, 0, gates, g1, *mw)
        x_lat = _ffn(x_lat, sh2, n2 * (1.0 + sc2), g2, wup, wdn)
        if with_ctx:
            bc = lambda v: jnp.broadcast_to(v, (bsz, 1, d))
            x_ctx = _merge(x_ctx, a_ctx, of_c, ob_c, hg, y_t, t, gates, bc(cg1), *mw)
            x_ctx = _ffn(x_ctx, bc(csh2), bc(n2 * (1.0 + csc2)), bc(cg2), wup, wdn)
    return x_lat
```

```python
import functools
import math

import jax
import jax.numpy as jnp
import numpy as np
from jax import lax
from jax.experimental import pallas as pl
from jax.experimental.pallas import tpu as pltpu

F32 = jnp.float32
BF16 = jnp.bfloat16

D_MODEL = 1024
GRID_W = 64
RMS_EPS = 1e-6
ADALN_CHUNKS = 6
ATT_HEADS = 8
ATT_KV_HEADS = 2
ATT_GROUP = ATT_HEADS // ATT_KV_HEADS
HEAD_DIM = 64
ATT_WIDTH = ATT_HEADS * HEAD_DIM
ATT_KV_WIDTH = ATT_KV_HEADS * HEAD_DIM
ROPE_THETA = 10000.0
HG_HEADS = 4
HG_DK = 64
HG_WIDTH = HG_HEADS * HG_DK
SSM_WIDTH = 256
SSM_GROUP = 16
SSM_GROUPS = SSM_WIDTH // SSM_GROUP
SSM_STATE = 64
FFN_HIDDEN = 2816
N_IN = 5376
O_Q, O_K, O_V, O_HQ, O_HF, O_HI, O_HG, O_U, O_GATE = 0, 512, 640, 768, 1024, 1536, 1792, 2048, 2304

V7X_LANES = 128
V7X_VMEM_BYTES = 64 * 1024 * 1024
MIB = 1024 * 1024

INPROJ_TILE = 256
ATTN_Q_TILE = 128
ATTN_V_ROWS = 80
FFN_HIDDEN_CHUNKS = 1
HG_TILE = 128
S5_CHUNK = 128
LOG2E = math.log2(math.e)


def _dot(a, b):
    return jnp.dot(a, b, preferred_element_type=F32)


def _dot_nt(a, b):
    return lax.dot_general(a, b, (((1,), (1,)), ((), ())), preferred_element_type=F32)


def _dot_tn(a, b):
    return lax.dot_general(a, b, (((0,), (0,)), ((), ())), preferred_element_type=F32)


def _split(x):
    hi = x.astype(BF16)
    lo = (x - hi.astype(F32)).astype(BF16)
    return hi, lo


def _dot3(a, b):
    ah, al = _split(a)
    bh, bl = _split(b)
    return _dot(ah, bh) + (_dot(ah, bl) + _dot(al, bh))


def _sigmoid(x):
    return jax.nn.sigmoid(x)


def _cparams(n_axes, vmem_mib):
    return pltpu.CompilerParams(
        dimension_semantics=("arbitrary",) * n_axes,
        vmem_limit_bytes=min(vmem_mib * MIB, V7X_VMEM_BYTES - 4 * MIB),
    )


def _const_spec(shape):
    nd = len(shape)
    return pl.BlockSpec(shape, lambda *_: (0,) * nd, pipeline_mode=pl.Buffered(1))


def _mod_kernel(c_ref, w_ref, b_ref, o_ref):
    c = c_ref[...]
    s = c * _sigmoid(c)
    o_ref[0] = _dot3(s, w_ref[0]) + b_ref[0]


def _modulation(cond8, w_mod, b_mod):
    n_layers, d, n = w_mod.shape
    nb = 1536
    return pl.pallas_call(
        _mod_kernel,
        grid=(n_layers, n // nb),
        in_specs=[
            pl.BlockSpec((8, d), lambda l, j: (0, 0)),
            pl.BlockSpec((1, d, nb), lambda l, j: (l, 0, j)),
            pl.BlockSpec((1, 1, nb), lambda l, j: (l, 0, j)),
        ],
        out_specs=pl.BlockSpec((1, 8, nb), lambda l, j: (l, 0, j)),
        out_shape=jax.ShapeDtypeStruct((n_layers, 8, n), F32),
        compiler_params=_cparams(2, 40),
        name="adaln_modulation",
    )(cond8, w_mod, b_mod.reshape(n_layers, 1, n))


def _rope128(x, c, s1, s2):
    return x * c + pltpu.roll(x, V7X_LANES - 16, 1) * s1 + pltpu.roll(x, 16, 1) * s2


def _inproj_kernel(x_ref, cx_ref, sh_ref, a_ref, csh_ref, ca_ref, w_ref, qg_ref, kg_ref, c_ref, s1_ref, s2_ref,
                   gmq_ref, gmk_ref, q_ref, k_ref, vt_ref, hq_ref, hf_ref, hi_ref, hg_ref, ut_ref, gate_ref, *, n_lat):
    is_ctx = pl.program_id(1) >= n_lat
    x = jnp.where(is_ctx, cx_ref[0], x_ref[0])
    a = jnp.where(is_ctx, ca_ref[0], a_ref[0])
    sh = jnp.where(is_ctx, csh_ref[0], sh_ref[0])
    ms = jnp.mean(x * x, axis=-1, keepdims=True)
    h = (x * lax.rsqrt(ms + RMS_EPS)) * a + sh
    hb = h.astype(BF16)

    def proj(lo, hi):
        return _dot(hb, w_ref[:, lo:hi])

    c, s1, s2 = c_ref[...], s1_ref[...], s2_ref[...]

    zq = proj(O_Q, O_K)
    msq = _dot((zq * zq).astype(BF16), gmq_ref[...])
    qn = zq * lax.rsqrt(msq + RMS_EPS) * qg_ref[...]
    for j in range(ATT_WIDTH // V7X_LANES):
        sl = slice(j * V7X_LANES, (j + 1) * V7X_LANES)
        q_ref[0, :, sl] = _rope128(qn[:, sl], c, s1, s2).astype(BF16)

    zkv = proj(O_K, O_HQ)
    zk = zkv[:, 0:ATT_KV_WIDTH]
    msk = _dot((zk * zk).astype(BF16), gmk_ref[...])
    kn = _rope128(zk * lax.rsqrt(msk + RMS_EPS) * kg_ref[...], c, s1, s2)
    vt = zkv[:, ATT_KV_WIDTH:].T
    tm = x.shape[0]
    ones_row = lax.broadcasted_iota(jnp.int32, (ATTN_V_ROWS - HEAD_DIM, tm), 0) == 0
    for hd in range(ATT_KV_HEADS):
        k_ref[0, hd] = kn[:, hd * HEAD_DIM:(hd + 1) * HEAD_DIM].astype(BF16)
        vt_ref[0, hd, 0:HEAD_DIM, :] = vt[hd * HEAD_DIM:(hd + 1) * HEAD_DIM, :].astype(BF16)
        vt_ref[0, hd, HEAD_DIM:ATTN_V_ROWS, :] = jnp.where(ones_row, 1.0, 0.0).astype(BF16)

    hq_ref[0] = proj(O_HQ, O_HF).astype(BF16)
    hf_ref[0] = proj(O_HF, O_HI)
    hi_ref[0] = proj(O_HI, O_HG).astype(BF16)
    hg_ref[0] = proj(O_HG, O_U).astype(BF16)
    ut = proj(O_U, O_GATE).T
    for j in range(tm // S5_CHUNK):
        ut_ref[0, j] = ut[:, j * S5_CHUNK:(j + 1) * S5_CHUNK]
    for j in range(3):
        lo = O_GATE + j * D_MODEL
        gate_ref[0, :, j * D_MODEL:(j + 1) * D_MODEL] = _sigmoid(proj(lo, lo + D_MODEL)).astype(BF16)


def _inproj(x, cx, sh, a, csh, ca, w_in, qg, kg, rope, gmq, gmk):
    b, t, d = x.shape
    t_ctx = cx.shape[1]
    tm = INPROJ_TILE
    n_lat, n_ctx = t // tm, t_ctx // tm
    t_all = t + t_ctx
    c, s1, s2 = rope
    row = lambda bi, i: (bi, i, 0)
    vec = lambda bi, i: (bi, 0, 0)
    tab = lambda bi, i: (i, 0)

    def widths(specs):
        shapes = [jax.ShapeDtypeStruct((b, t_all, w), dt) for w, dt in specs]
        return shapes, [pl.BlockSpec((1, tm, w), row) for w, _ in specs]

    q_shape, q_spec = widths([(ATT_WIDTH, BF16)])
    h_shape, h_spec = widths([(HG_WIDTH, BF16), (2 * HG_WIDTH, F32), (HG_WIDTH, BF16), (HG_WIDTH, BF16)])
    g_shape, g_spec = widths([(3 * D_MODEL, BF16)])
    out_shape = q_shape + [
        jax.ShapeDtypeStruct((b, ATT_KV_HEADS, t_all, HEAD_DIM), BF16),
        jax.ShapeDtypeStruct((b, ATT_KV_HEADS, ATTN_V_ROWS, t_all), BF16),
    ] + h_shape + [jax.ShapeDtypeStruct((b, t_all // S5_CHUNK, SSM_WIDTH, S5_CHUNK), F32)] + g_shape
    out_specs = q_spec + [
        pl.BlockSpec((1, ATT_KV_HEADS, tm, HEAD_DIM), lambda bi, i: (bi, 0, i, 0)),
        pl.BlockSpec((1, ATT_KV_HEADS, ATTN_V_ROWS, tm), lambda bi, i: (bi, 0, 0, i)),
    ] + h_spec + [pl.BlockSpec((1, tm // S5_CHUNK, SSM_WIDTH, S5_CHUNK), lambda bi, i: (bi, i, 0, 0))] + g_spec
    return pl.pallas_call(
        functools.partial(_inproj_kernel, n_lat=n_lat),
        grid=(b, n_lat + n_ctx),
        in_specs=[
            pl.BlockSpec((1, tm, d), lambda bi, i: (bi, jnp.minimum(i, n_lat - 1), 0)),
            pl.BlockSpec((1, tm, d), lambda bi, i: (bi, jnp.maximum(i - n_lat, 0), 0)),
            pl.BlockSpec((1, 1, d), vec),
            pl.BlockSpec((1, 1, d), vec),
            pl.BlockSpec((1, 1, d), lambda bi, i: (0, 0, 0)),
            pl.BlockSpec((1, 1, d), lambda bi, i: (0, 0, 0)),
            _const_spec((d, N_IN)),
            _const_spec((1, ATT_WIDTH)),
            _const_spec((1, ATT_KV_WIDTH)),
            pl.BlockSpec((tm, V7X_LANES), tab),
            pl.BlockSpec((tm, V7X_LANES), tab),
            pl.BlockSpec((tm, V7X_LANES), tab),
            _const_spec((ATT_WIDTH, ATT_WIDTH)),
            _const_spec((ATT_KV_WIDTH, ATT_KV_WIDTH)),
        ],
        out_specs=out_specs,
        out_shape=out_shape,
        compiler_params=_cparams(2, 56),
        name="in_projection",
    )(x, cx, sh, a, csh, ca, w_in, qg, kg, c, s1, s2, gmq, gmk)


def _attn_kernel(q_ref, k_ref, vt_ref, o_ref, qt_ref, m_ref, acc_ref, sa_ref, sb_ref, mxa_ref, mxb_ref, *, tk, nkb):
    tq = q_ref.shape[1]
    qt = q_ref[0].astype(F32).T
    qt_ref[...] = jnp.concatenate(
        [qt[HEAD_DIM * g:HEAD_DIM * (g + 1), :] for g in range(ATT_GROUP)], axis=1).astype(BF16)
    m_ref[...] = jnp.full(m_ref.shape, -jnp.inf, F32)
    acc_ref[...] = jnp.zeros(acc_ref.shape, F32)

    def scores(kb, s_ref, mx_ref):
        off = pl.multiple_of(kb * tk, tk)
        s = _dot(k_ref[0, 0, pl.ds(off, tk), :], qt_ref[...])
        s_ref[...] = s
        mx_ref[...] = jnp.max(s, axis=0, keepdims=True)

    def consume(kb, s_ref, mx_ref):
        off = pl.multiple_of(kb * tk, tk)
        m_prev = m_ref[...]
        m_new = jnp.maximum(m_prev, mx_ref[...])
        alpha = jnp.exp2(m_prev - m_new)
        p = jnp.exp2(s_ref[...] - m_new).astype(BF16)
        acc_ref[...] = alpha * acc_ref[...] + _dot(vt_ref[0, 0, :, pl.ds(off, tk)], p)
        m_ref[...] = m_new

    bufs = ((sa_ref, mxa_ref), (sb_ref, mxb_ref))
    scores(0, *bufs[0])

    def body(i, carry):
        base = ATTN_BLOCKS_PER_ITER * i
        for j in range(ATTN_BLOCKS_PER_ITER):
            scores(base + j + 1, *bufs[(j + 1) % 2])
            consume(base + j, *bufs[j % 2])
        return carry

    n_it = (nkb - 1) // ATTN_BLOCKS_PER_ITER
    lax.fori_loop(0, n_it, body, 0)
    for r in range(n_it * ATTN_BLOCKS_PER_ITER, nkb):
        if r + 1 < nkb:
            scores(r + 1, *bufs[(r + 1) % 2])
        consume(r, *bufs[r % 2])

    acc = acc_ref[...]
    out_t = acc[0:HEAD_DIM, :] / acc[HEAD_DIM:HEAD_DIM + 1, :]
    out_t = jnp.concatenate([out_t[:, g * tq:(g + 1) * tq] for g in range(ATT_GROUP)], axis=0)
    o_ref[0] = out_t.T.astype(BF16)


ATTN_KEY_BLOCKS = (640, 512, 256, 128)
ATTN_BLOCKS_PER_ITER = 12


def _attention(q, k, vt_ext, q_start, q_len, k_start, k_len):
    b = q.shape[0]
    tq = ATTN_Q_TILE
    tk = next(c for c in ATTN_KEY_BLOCKS if k_len % c == 0)
    assert q_start % tq == 0 and q_len % tq == 0 and k_start % k_len == 0
    q_off, k_blk = q_start // tq, k_start // k_len
    kern = functools.partial(_attn_kernel, tk=tk, nkb=k_len // tk)
    gw = ATT_GROUP * HEAD_DIM
    m = ATT_GROUP * tq
    return pl.pallas_call(
        kern,
        grid=(b, ATT_KV_HEADS, q_len // tq),
        in_specs=[
            pl.BlockSpec((1, tq, gw), lambda bi, h, i: (bi, i + q_off, h)),
            pl.BlockSpec((1, 1, k_len, HEAD_DIM), lambda bi, h, i: (bi, h, k_blk, 0)),
            pl.BlockSpec((1, 1, ATTN_V_ROWS, k_len), lambda bi, h, i: (bi, h, 0, k_blk)),
        ],
        out_specs=pl.BlockSpec((1, tq, gw), lambda bi, h, i: (bi, i, h)),
        out_shape=jax.ShapeDtypeStruct((b, q_len, ATT_WIDTH), BF16),
        scratch_shapes=[
            pltpu.VMEM((HEAD_DIM, m), BF16),
            pltpu.VMEM((1, m), F32),
            pltpu.VMEM((ATTN_V_ROWS, m), F32),
            pltpu.VMEM((tk, m), F32),
            pltpu.VMEM((tk, m), F32),
            pltpu.VMEM((1, m), F32),
            pltpu.VMEM((1, m), F32),
        ],
        compiler_params=_cparams(3, 48),
        name="gqa_attention",
    )(q, k, vt_ext)


HG_LEVELS = 7
HG_SEL_LEVELS = 1


def _hgrn_kernel(qf_ref, ff_ref, vf_ref, qb_ref, fb_ref, vb_ref, lb_ref, s0_ref,
                 tri_ref, sel_ref, sm_ref, hm_ref, hmt_ref, gsum_ref, bd_ref,
                 of_ref, ob_ref, sfin_ref, stf_ref, stb_ref, cf_s, cb_s):
    @pl.when(pl.program_id(1) == 0)
    def _():
        stf_ref[...] = s0_ref[0, 0]
        stb_ref[...] = s0_ref[0, 1]

    tt, w = HG_TILE, HG_WIDTH
    dirs = (0, 1)
    q_refs, f_refs, v_refs = (qf_ref, qb_ref), (ff_ref, fb_ref), (vf_ref, vb_ref)
    o_refs, st_refs, c_refs = (of_ref, ob_ref), (stf_ref, stb_ref), (cf_s, cb_s)
    hm = [hm_ref[h] for h in range(HG_HEADS)]
    hmt = [hmt_ref[h] for h in range(HG_HEADS)]

    qs, kin, vb, c, tot, ref_small, o, scores = [], [], [], [], [], [], [], [None, None]
    for d in dirs:
        q = q_refs[d][0].astype(F32)
        fpre = f_refs[d][0]
        lb = lb_ref[d:d + 1, :]
        qs.append(q * _sigmoid(q))
        kin.append((1.0 - lb) * _sigmoid(-fpre))
        vb.append(v_refs[d][0])
        hi, lo = _split(jnp.log(lb + (1.0 - lb) * _sigmoid(fpre)))
        cd = (_dot(tri_ref[d], hi) + _dot(tri_ref[d], lo)) * LOG2E
        c_refs[d][...] = cd
        c.append(cd)
    for d in dirs:
        last = 0 if d else tt - 1
        tot.append(c_refs[d][last:last + 1, :])
        chi, clo = _split(c[d])
        ref_small.append(_dot(sel_ref[d], chi) + _dot(sel_ref[d], clo))
        o.append(_dot((qs[d] * kin[d]).astype(BF16), gsum_ref[...]) * vb[d].astype(F32))

    for lvl in range(HG_LEVELS):
        m = 1 << lvl
        for d in dirs:
            if lvl < HG_SEL_LEVELS:
                cref = ref_small[d][lvl * tt:(lvl + 1) * tt, :]
            else:
                rows = []
                for blk in range(tt // (2 * m)):
                    r = blk * 2 * m + (m if d else m - 1)
                    rows.append(jnp.broadcast_to(c_refs[d][r:r + 1, :], (2 * m, w)))
                cref = rows[0] if len(rows) == 1 else jnp.concatenate(rows, axis=0)
            wgt = jnp.exp2(-jnp.abs(c[d] - cref))
            ql = (qs[d] * wgt).astype(BF16)
            kl_t = (kin[d] * wgt).T.astype(BF16)
            kstack_t = jnp.concatenate([kl_t * hmt[h] for h in range(HG_HEADS)], axis=1)
            sc = _dot(ql, kstack_t).astype(BF16) * sm_ref[d, lvl]
            scores[d] = sc if scores[d] is None else scores[d] + sc

    for d in dirs:
        vstack = jnp.concatenate([vb[d] * hm[h] for h in range(HG_HEADS)], axis=0)
        o[d] = o[d] + _dot(scores[d], vstack)
    for d in dirs:
        st = st_refs[d][...]
        o[d] = o[d] + _dot_nt((qs[d] * jnp.exp2(c[d])).astype(BF16), st.astype(BF16))
        kv = _dot_tn(vb[d], (kin[d] * jnp.exp2(tot[d] - c[d])).astype(BF16))
        st_new = jnp.exp2(tot[d]) * st + kv * bd_ref[...]
        st_refs[d][...] = st_new
        sfin_ref[0, d] = st_new
        o_refs[d][0] = o[d].astype(BF16)


def _hgrn_constants():
    tt, w = HG_TILE, HG_WIDTH
    t = np.arange(tt)
    tri = np.stack([t[None, :] <= t[:, None], t[None, :] >= t[:, None]]).astype(BF16)
    sel = []
    for reverse in (False, True):
        per = []
        for lvl in range(HG_SEL_LEVELS):
            m = 1 << lvl
            r = (t // (2 * m)) * (2 * m) + (m if reverse else m - 1)
            per.append(t[None, :] == r[:, None])
        sel.append(np.concatenate(per, axis=0))
    sel = np.stack(sel).astype(BF16)
    col = np.arange(HG_HEADS * tt) % tt
    sm = []
    for reverse in (False, True):
        per = []
        for lvl in range(HG_LEVELS):
            t_up, s_up = ((t >> lvl) & 1) == 1, ((col >> lvl) & 1) == 1
            same = (t[:, None] >> (lvl + 1)) == (col[None, :] >> (lvl + 1))
            halves = (~t_up[:, None] & s_up[None, :]) if reverse else (t_up[:, None] & ~s_up[None, :])
            per.append(same & halves)
        sm.append(np.stack(per))
    sm = np.stack(sm).astype(BF16)
    lane_head = np.arange(w) // HG_DK
    hm = np.stack([np.broadcast_to((lane_head == h)[None, :], (tt, w)) for h in range(HG_HEADS)]).astype(BF16)
    same_head = lane_head[:, None] == lane_head[None, :]
    hmt = np.ascontiguousarray(np.swapaxes(hm, 1, 2))
    return tri, sel, sm, hm, hmt, same_head.astype(BF16), same_head.astype(F32)


def _hgrn(hq, hf, hv, lb2, s0, consts, start, t):
    b, _, w = hq.shape
    tt = HG_TILE
    nt, off = t // tt, start // tt
    fwd = lambda bi, i: (bi, off + i, 0)
    bwd = lambda bi, i: (bi, off + nt - 1 - i, 0)
    bwd_f = lambda bi, i: (bi, off + nt - 1 - i, 1)
    fwd_o = lambda bi, i: (bi, i, 0)
    bwd_o = lambda bi, i: (bi, nt - 1 - i, 0)
    st_spec = pl.BlockSpec((1, 2, w, w), lambda bi, i: (bi, 0, 0, 0))
    return pl.pallas_call(
        _hgrn_kernel,
        grid=(b, nt),
        in_specs=[
            pl.BlockSpec((1, tt, w), fwd), pl.BlockSpec((1, tt, w), fwd), pl.BlockSpec((1, tt, w), fwd),
            pl.BlockSpec((1, tt, w), bwd), pl.BlockSpec((1, tt, w), bwd_f), pl.BlockSpec((1, tt, w), bwd),
            pl.BlockSpec((2, w), lambda bi, i: (0, 0)),
            st_spec,
        ] + [_const_spec(a.shape) for a in consts],
        out_specs=[pl.BlockSpec((1, tt, w), fwd_o), pl.BlockSpec((1, tt, w), bwd_o), st_spec],
        out_shape=[
            jax.ShapeDtypeStruct((b, t, w), BF16),
            jax.ShapeDtypeStruct((b, t, w), BF16),
            jax.ShapeDtypeStruct((b, 2, w, w), F32),
        ],
        scratch_shapes=[
            pltpu.VMEM((w, w), F32),
            pltpu.VMEM((w, w), F32),
            pltpu.VMEM((tt, w), F32),
            pltpu.VMEM((tt, w), F32),
        ],
        compiler_params=_cparams(2, 32),
        name="hgrn2_scan",
    )(hq, hf, hv, hq, hf, hv, lb2, s0, *consts)


def _cpow(a_re, a_im, tau):
    mag = jnp.exp(tau * a_re)
    ang = tau * a_im
    return mag * jnp.cos(ang), mag * jnp.sin(ang)


def _s5_build_weights(ar_row, ai_row, ar_col, ai_col, lr_row, li_row, bt_r, bt_i, cr, ci, ctr, cti,
                      w_ref, wst_ref, wout_ref, laml_ref, kall_ref):
    lc, p, c = S5_CHUNK, SSM_STATE, SSM_GROUP
    tau_l = lax.broadcasted_iota(jnp.int32, (p, lc), 1).astype(F32)
    tau_s = lax.broadcasted_iota(jnp.int32, (lc, p), 0).astype(F32)

    btr, bti = [], []
    for d in range(2):
        e_r, e_i = _cpow(ar_row[0, d], ai_row[0, d], 1.0)
        l_r, l_i = lr_row[0, d], li_row[0, d]
        den = l_r * l_r + l_i * l_i
        f_r = ((e_r - 1.0) * l_r + e_i * l_i) / den
        f_i = (e_i * l_r - (e_r - 1.0) * l_i) / den
        btr.append(f_r * bt_r[0, d] - f_i * bt_i[0, d])
        bti.append(f_r * bt_i[0, d] + f_i * bt_r[0, d])

    def cb(d):
        re, im = [], []
        for c1 in range(c):
            b_r, b_i = btr[d][c1:c1 + 1, :], bti[d][c1:c1 + 1, :]
            re.append(b_r * cr[0, d] - b_i * ci[0, d])
            im.append(b_r * ci[0, d] + b_i * cr[0, d])
        return jnp.concatenate(re, axis=0), jnp.concatenate(im, axis=0)

    cbf_r, cbf_i = cb(0)
    pf_r, pf_i = _cpow(ar_col[0, 0], ai_col[0, 0], tau_l)
    kf = _dot3(cbf_r, pf_r) - _dot3(cbf_i, pf_i)
    cbb_r, cbb_i = cb(1)
    pb_r, pb_i = _cpow(ar_col[0, 1], ai_col[0, 1], lc - tau_l)
    kb = _dot3(cbb_r, pb_r) - _dot3(cbb_i, pb_i)
    lane = lax.broadcasted_iota(jnp.int32, (c * c, lc), 1)
    kf = kf + jnp.where(lane == 0, jnp.sum(cbb_r, axis=1, keepdims=True), 0.0)
    kall_ref[...] = jnp.concatenate([kf, kb], axis=1)

    def toeplitz_rows(c1, carry):
        for c2 in range(c):
            row = kall_ref[pl.ds(c1 * c + c2, 1), :]
            blk = pltpu.roll(jnp.broadcast_to(row, (lc, 2 * lc)), 0, 1, stride=1, stride_axis=0)
            w_ref[pl.ds(pl.multiple_of(c1 * lc, lc), lc), c2 * lc:(c2 + 1) * lc] = blk[:, :lc].astype(BF16)
        return carry

    lax.fori_loop(0, c, toeplitz_rows, 0)

    sf_r, sf_i = _cpow(ar_row[0, 0], ai_row[0, 0], (lc - 1) - tau_s)
    sb_r, sb_i = _cpow(ar_row[0, 1], ai_row[0, 1], tau_s)
    for c1 in range(c):
        re, im = [], []
        for d, (p_r, p_i) in enumerate(((sf_r, sf_i), (sb_r, sb_i))):
            b_r, b_i = btr[d][c1:c1 + 1, :], bti[d][c1:c1 + 1, :]
            re.append(p_r * b_r - p_i * b_i)
            im.append(p_r * b_i + p_i * b_r)
        wst_ref[c1 * lc:(c1 + 1) * lc, :] = jnp.concatenate(re + im, axis=1).astype(BF16)

    of_r, of_i = _cpow(ar_col[0, 0], ai_col[0, 0], tau_l + 1.0)
    ob_r, ob_i = _cpow(ar_col[0, 1], ai_col[0, 1], lc - tau_l)
    for c2 in range(c):
        re, im = [], []
        for d, (p_r, p_i) in enumerate(((of_r, of_i), (ob_r, ob_i))):
            c_r, c_i = ctr[0, d, :, c2:c2 + 1], cti[0, d, :, c2:c2 + 1]
            re.append(c_r * p_r - c_i * p_i)
            im.append(-(c_r * p_i + c_i * p_r))
        wout_ref[:, c2 * lc:(c2 + 1) * lc] = jnp.concatenate(re + im, axis=0).astype(BF16)

    lf_r, lf_i = _cpow(ar_row[0, 0], ai_row[0, 0], float(lc))
    lb_r, lb_i = _cpow(ar_row[0, 1], ai_row[0, 1], float(lc))
    laml_ref[0:1, :] = jnp.concatenate([lf_r, lb_r], axis=1)
    laml_ref[1:2, :] = jnp.concatenate([lf_i, lb_i], axis=1)


def _s5_kernel(u_ref, ar_row, ai_row, ar_col, ai_col, lr_row, li_row, bt_r, bt_i, cr, ci, ctr, cti, dsk_ref, y_ref,
               w_ref, wst_ref, wout_ref, laml_ref, kall_ref, xloc_s, xin_s, *, nctx, nck):
    c, lc, p, p2 = SSM_GROUP, S5_CHUNK, SSM_STATE, 2 * SSM_STATE

    @pl.when(pl.program_id(1) == 0)
    def _():
        _s5_build_weights(ar_row, ai_row, ar_col, ai_col, lr_row, li_row, bt_r, bt_i, cr, ci, ctr, cti,
                          w_ref, wst_ref, wout_ref, laml_ref, kall_ref)

    ub = jnp.concatenate([u_ref[0, :, c1, :] for c1 in range(c)], axis=1).astype(BF16)
    xloc_s[...] = _dot(ub, wst_ref[...])

    nlat = nck - nctx
    order_f = list(range(nlat, nck)) + list(range(nlat))
    order_b = list(range(nck - 1, nlat - 1, -1)) + list(range(nlat - 1, -1, -1))
    m_r, m_i = laml_ref[0:1, :], laml_ref[1:2, :]
    is_fwd = lax.broadcasted_iota(jnp.int32, (1, p2), 1) < p
    x_r = jnp.zeros((1, p2), F32)
    x_i = jnp.zeros((1, p2), F32)
    for kf, kb in zip(order_f, order_b):
        xin_s[kf:kf + 1, 0:p] = x_r[:, 0:p]
        xin_s[kb:kb + 1, p:p2] = x_r[:, p:p2]
        xin_s[kf:kf + 1, p2:p2 + p] = x_i[:, 0:p]
        xin_s[kb:kb + 1, p2 + p:2 * p2] = x_i[:, p:p2]
        loc_r = jnp.where(is_fwd, xloc_s[kf:kf + 1, 0:p2], xloc_s[kb:kb + 1, 0:p2])
        loc_i = jnp.where(is_fwd, xloc_s[kf:kf + 1, p2:2 * p2], xloc_s[kb:kb + 1, p2:2 * p2])
        x_r, x_i = m_r * x_r - m_i * x_i + loc_r, m_r * x_i + m_i * x_r + loc_i

    y = _dot(ub, w_ref[...]) + _dot(xin_s[...].astype(BF16), wout_ref[...])
    for c2 in range(c):
        y_ref[0, :, c2, :] = y[:, c2 * lc:(c2 + 1) * lc] + dsk_ref[0, c2:c2 + 1, :] * u_ref[0, :, c2, :]


def _s5(u_t, a_re, a_im, lam_re, lam_im, b_re, b_im, c_re, c_im, dsk, nctx):
    b, nck, wd, lc = u_t.shape
    g, c, p = SSM_GROUPS, SSM_GROUP, SSM_STATE
    n = c * lc
    row = lambda x: x.reshape(g, 2, 1, p)
    col = lambda x: x.reshape(g, 2, p, 1)
    tr = lambda x: jnp.swapaxes(x, -1, -2)
    params = [row(a_re), row(a_im), col(a_re), col(a_im), row(lam_re), row(lam_im),
              tr(b_re), tr(b_im), c_re, c_im, tr(c_re), tr(c_im)]
    kern = functools.partial(_s5_kernel, nctx=nctx, nck=nck)
    gspec = lambda shp: pl.BlockSpec((1,) + shp, lambda gi, bi: (gi,) + (0,) * len(shp))
    io_spec = pl.BlockSpec((1, nck, c, lc), lambda gi, bi: (bi, 0, gi, 0))
    return pl.pallas_call(
        kern,
        grid=(g, b),
        in_specs=[io_spec] + [gspec(a.shape[1:]) for a in params] + [gspec((c, lc))],
        out_specs=io_spec,
        out_shape=jax.ShapeDtypeStruct(u_t.shape, F32),
        scratch_shapes=[
            pltpu.VMEM((n, n), BF16),
            pltpu.VMEM((n, 4 * p), BF16),
            pltpu.VMEM((4 * p, n), BF16),
            pltpu.VMEM((2, 2 * p), F32),
            pltpu.VMEM((c * c, 2 * lc), F32),
            pltpu.VMEM((nck, 4 * p), F32),
            pltpu.VMEM((nck, 4 * p), F32),
        ],
        compiler_params=_cparams(2, 48),
        name="s5_scan",
    )(u_t, *params, dsk)


def _merge_kernel(x_ref, att_ref, of_ref, ob_ref, hg_ref, yt_ref, gate_ref, g1_ref,
                  wa_ref, wr_ref, ws_ref, wo_ref, wglu_ref, bglu_ref, hn_ref, gm_ref, o_ref):
    r = of_ref[0].astype(F32) + ob_ref[0].astype(F32)
    ms = _dot((r * r).astype(BF16), gm_ref[...])
    g = hg_ref[0].astype(F32)
    yrec = (r * lax.rsqrt(ms + RMS_EPS) * hn_ref[...]) * (g * _sigmoid(g))

    ys = jnp.concatenate([yt_ref[0, j].T for j in range(yt_ref.shape[1])], axis=0)
    z = 0.5 * ys * (1.0 + jnp.tanh(math.sqrt(2.0 / math.pi) * (ys + 0.044715 * (ys * ys * ys))))
    yssm = z * _sigmoid(_dot(z.astype(BF16), wglu_ref[...]) + bglu_ref[...])

    d = D_MODEL
    m = gate_ref[0, :, 0:d].astype(F32) * _dot(att_ref[0], wa_ref[...])
    m = m + gate_ref[0, :, d:2 * d].astype(F32) * _dot(yrec.astype(BF16), wr_ref[...])
    m = m + gate_ref[0, :, 2 * d:3 * d].astype(F32) * _dot(yssm.astype(BF16), ws_ref[...])
    y = _dot(m.astype(BF16), wo_ref[...])
    o_ref[0] = x_ref[0] + g1_ref[0] * y


def _merge(x, att, o_f, o_b, hg, y_t, t_off, gates, g1, wa, wr, ws, wo, wglu, bglu, hn, gm):
    b, t, d = x.shape
    tm = min(512, t)
    off = t_off // tm
    row = lambda bi, i: (bi, i, 0)
    row_off = lambda bi, i: (bi, i + off, 0)
    vec = lambda bi, i: (bi, 0, 0)
    return pl.pallas_call(
        _merge_kernel,
        grid=(b, t // tm),
        in_specs=[
            pl.BlockSpec((1, tm, d), row),
            pl.BlockSpec((1, tm, ATT_WIDTH), row),
            pl.BlockSpec((1, tm, HG_WIDTH), row),
            pl.BlockSpec((1, tm, HG_WIDTH), row),
            pl.BlockSpec((1, tm, HG_WIDTH), row_off),
            pl.BlockSpec((1, tm // S5_CHUNK, SSM_WIDTH, S5_CHUNK), lambda bi, i: (bi, i + off, 0, 0)),
            pl.BlockSpec((1, tm, 3 * d), row_off),
            pl.BlockSpec((1, 1, d), vec),
            _const_spec(wa.shape), _const_spec(wr.shape), _const_spec(ws.shape), _const_spec(wo.shape),
            _const_spec(wglu.shape), _const_spec(bglu.shape), _const_spec(hn.shape), _const_spec(gm.shape),
        ],
        out_specs=pl.BlockSpec((1, tm, d), row),
        out_shape=jax.ShapeDtypeStruct((b, t, d), F32),
        compiler_params=_cparams(2, 48),
        name="merge_branches",
    )(x, att, o_f, o_b, hg, y_t, gates, g1, wa, wr, ws, wo, wglu, bglu, hn, gm)


def _ffn_kernel(x_ref, sh_ref, a_ref, g_ref, wup_ref, wdn_ref, o_ref, *, nj):
    x = x_ref[0]
    ms = jnp.mean(x * x, axis=-1, keepdims=True)
    hb = ((x * lax.rsqrt(ms + RMS_EPS)) * a_ref[0] + sh_ref[0]).astype(BF16)
    f = FFN_HIDDEN
    fc = f // nj
    acc = None
    for j in range(nj):
        a = _dot(hb, wup_ref[:, j * fc:(j + 1) * fc])
        bgate = _dot(hb, wup_ref[:, f + j * fc:f + (j + 1) * fc])
        act = ((a * _sigmoid(a)) * bgate).astype(BF16)
        part = _dot(act, wdn_ref[j * fc:(j + 1) * fc, :])
        acc = part if acc is None else acc + part
    o_ref[0] = x + g_ref[0] * acc


def _ffn(x, sh, a, g, wup, wdn):
    b, t, d = x.shape
    tm = min(512, t)
    row = lambda bi, i: (bi, i, 0)
    vec = lambda bi, i: (bi, 0, 0)
    return pl.pallas_call(
        functools.partial(_ffn_kernel, nj=FFN_HIDDEN_CHUNKS),
        grid=(b, t // tm),
        in_specs=[
            pl.BlockSpec((1, tm, d), row),
            pl.BlockSpec((1, 1, d), vec), pl.BlockSpec((1, 1, d), vec), pl.BlockSpec((1, 1, d), vec),
            _const_spec(wup.shape), _const_spec(wdn.shape),
        ],
        out_specs=pl.BlockSpec((1, tm, d), row),
        out_shape=jax.ShapeDtypeStruct((b, t, d), F32),
        compiler_params=_cparams(2, 56),
        name="swiglu_ffn",
    )(x, sh, a, g, wup, wdn)


def _rope_tables(t, t_ctx):
    pos = jnp.arange(t)
    row = (pos // GRID_W).astype(F32)
    col = (pos % GRID_W).astype(F32)
    axis_dim = HEAD_DIM // 2
    inv = ROPE_THETA ** (-jnp.arange(0, axis_dim, 2, dtype=F32) / axis_dim)
    ang_r, ang_c = row[:, None] * inv, col[:, None] * inv
    cr, sr, cc, sc = jnp.cos(ang_r), jnp.sin(ang_r), jnp.cos(ang_c), jnp.sin(ang_c)
    z = jnp.zeros_like(cr)
    rep = V7X_LANES // HEAD_DIM
    c = jnp.tile(jnp.concatenate([cr, cr, cc, cc], axis=1), (1, rep))
    s1 = jnp.tile(jnp.concatenate([-sr, z, -sc, z], axis=1), (1, rep))
    s2 = jnp.tile(jnp.concatenate([z, sr, z, sc], axis=1), (1, rep))
    pad = lambda a, fill: jnp.concatenate([a, jnp.full((t_ctx, V7X_LANES), fill, F32)], axis=0)
    return pad(c, 1.0), pad(s1, 0.0), pad(s2, 0.0)


def _group_mean_matrix(width, group):
    i = np.arange(width) // group
    return np.where(i[:, None] == i[None, :], 1.0 / group, 0.0).astype(BF16)


def kernel(x, c, ctx, c_ctx, w_mod, b_mod, norm1_g, norm2_g, w_in, q_norm_g, k_norm_g, hgrn_lb, hgrn_norm_g,
           ssm_lam_re, ssm_lam_im, ssm_log_dt, ssm_b_re, ssm_b_im, ssm_c_re, ssm_c_im, ssm_d, w_glu, b_glu,
           w_br_attn, w_br_hgrn, w_br_ssm, w_out, w_ffn_up, w_ffn_down):
    bsz, t, d = x.shape
    t_ctx = ctx.shape[1]
    t_all = t + t_ctx
    depth = w_mod.shape[0]
    assert t % 512 == 0 and t_ctx % INPROJ_TILE == 0 and t % t_ctx == 0 and d == D_MODEL

    lb_soft = jax.nn.softmax(hgrn_lb.astype(F32), axis=0)
    lower_bounds = jnp.cumsum(lb_soft, axis=0) - lb_soft[0]
    rope = _rope_tables(t, t_ctx)
    gmq = _group_mean_matrix(ATT_WIDTH, HEAD_DIM)
    gmk = _group_mean_matrix(ATT_KV_WIDTH, HEAD_DIM)
    gmh = _group_mean_matrix(HG_WIDTH, HG_DK)
    hg_consts = _hgrn_constants()
    cond8 = jnp.zeros((8, d), F32).at[:bsz].set(c).at[bsz].set(c_ctx)
    mods = _modulation(cond8, w_mod, b_mod)

    x_lat, x_ctx = x, ctx
    for l in range(depth):
        with_ctx = l < depth - 1
        ml = mods[l, :bsz].reshape(bsz, ADALN_CHUNKS, 1, d)
        mc = mods[l, bsz].reshape(1, ADALN_CHUNKS, 1, d)
        sh1, sc1, g1, sh2, sc2, g2 = [ml[:, i] for i in range(ADALN_CHUNKS)]
        csh1, csc1, cg1, csh2, csc2, cg2 = [mc[:, i] for i in range(ADALN_CHUNKS)]
        n1, n2 = norm1_g[l].reshape(1, 1, d), norm2_g[l].reshape(1, 1, d)

        qg = (jnp.tile(q_norm_g[l], ATT_HEADS) * (HEAD_DIM ** -0.5 * LOG2E)).reshape(1, ATT_WIDTH)
        kg = jnp.tile(k_norm_g[l], ATT_KV_HEADS).reshape(1, ATT_KV_WIDTH)
        q, k, vt, hq, hf, hi, hg, u_t, gates = _inproj(
            x_lat, x_ctx, sh1, n1 * (1.0 + sc1), csh1, n1 * (1.0 + csc1), w_in[l].astype(BF16), qg, kg, rope, gmq, gmk)

        a_lat = _attention(q, k, vt, 0, t, 0, t_all)
        if with_ctx:
            a_ctx = _attention(q, k, vt, t, t_ctx, t, t_ctx)

        lb2 = lower_bounds[l]
        s_zero = jnp.zeros((bsz, 2, HG_WIDTH, HG_WIDTH), F32)
        of_c, ob_c, s_ctx = _hgrn(hq, hf, hi, lb2, s_zero, hg_consts, t, t_ctx)
        of_l, ob_l, _ = _hgrn(hq, hf, hi, lb2, s_ctx, hg_consts, 0, t)

        dt = jnp.exp(ssm_log_dt[l].astype(F32))[..., None]
        gd = lambda a: jnp.swapaxes(a.astype(F32), 0, 1)
        s5_params = (gd(ssm_lam_re[l] * dt), gd(ssm_lam_im[l] * dt), gd(ssm_lam_re[l]), gd(ssm_lam_im[l]),
                     gd(ssm_b_re[l]), gd(ssm_b_im[l]), gd(ssm_c_re[l]), gd(ssm_c_im[l]))
        dsk = jnp.broadcast_to(ssm_d[l].astype(F32).reshape(SSM_GROUPS, SSM_GROUP, 1), (SSM_GROUPS, SSM_GROUP, S5_CHUNK))
        y_t = _s5(u_t, *s5_params, dsk, t_ctx // S5_CHUNK)

        hn = jnp.tile(hgrn_norm_g[l], HG_HEADS).reshape(1, HG_WIDTH)
        mw = (w_br_attn[l].astype(BF16), w_br_hgrn[l].astype(BF16), w_br_ssm[l].astype(BF16), w_out[l].astype(BF16),
              w_glu[l].astype(BF16), b_glu[l].reshape(1, SSM_WIDTH), hn, gmh)
        wup, wdn = w_ffn_up[l].astype(BF16), w_ffn_down[l].astype(BF16)
        x_lat = _merge(x_lat, a_lat, of_l, ob_l, hg, y_t, 0, gates, g1, *mw)
        x_lat = _ffn(x_lat, sh2, n2 * (1.0 + sc2), g2, wup, wdn)
        if with_ctx:
            bc = lambda v: jnp.broadcast_to(v, (bsz, 1, d))
            x_ctx = _merge(x_ctx, a_ctx, of_c, ob_c, hg, y_t, t, gates, bc(cg1), *mw)
            x_ctx = _ffn(x_ctx, bc(csh2), bc(n2 * (1.0 + csc2)), bc(cg2), wup, wdn)
    return x_lat
```

```python
import functools
import math

import jax
import jax.numpy as jnp
import numpy as np
from jax import lax
from jax.experimental import pallas as pl
from jax.experimental.pallas import tpu as pltpu

F32 = jnp.float32
BF16 = jnp.bfloat16

D_MODEL = 1024
GRID_W = 64
RMS_EPS = 1e-6
ADALN_CHUNKS = 6
ATT_HEADS = 8
ATT_KV_HEADS = 2
ATT_GROUP = ATT_HEADS // ATT_KV_HEADS
HEAD_DIM = 64
ATT_WIDTH = ATT_HEADS * HEAD_DIM
ATT_KV_WIDTH = ATT_KV_HEADS * HEAD_DIM
ROPE_THETA = 10000.0
HG_HEADS = 4
HG_DK = 64
HG_WIDTH = HG_HEADS * HG_DK
SSM_WIDTH = 256
SSM_GROUP = 16
SSM_GROUPS = SSM_WIDTH // SSM_GROUP
SSM_STATE = 64
FFN_HIDDEN = 2816
N_IN = 5376
O_Q, O_K, O_V, O_HQ, O_HF, O_HI, O_HG, O_U, O_GATE = 0, 512, 640, 768, 1024, 1536, 1792, 2048, 2304

V7X_LANES = 128
V7X_VMEM_BYTES = 64 * 1024 * 1024
MIB = 1024 * 1024

INPROJ_TILE = 256
ATTN_Q_TILE = 512
ATTN_Q_SUBTILE = 128
ATTN_V_ROWS = 80
FFN_HIDDEN_CHUNKS = 1
HG_TILE = 128
S5_CHUNK = 128
LOG2E = math.log2(math.e)


def _dot(a, b):
    return jnp.dot(a, b, preferred_element_type=F32)


def _dot_nt(a, b):
    return lax.dot_general(a, b, (((1,), (1,)), ((), ())), preferred_element_type=F32)


def _dot_tn(a, b):
    return lax.dot_general(a, b, (((0,), (0,)), ((), ())), preferred_element_type=F32)


def _split(x):
    hi = x.astype(BF16)
    lo = (x - hi.astype(F32)).astype(BF16)
    return hi, lo


def _dot3(a, b):
    ah, al = _split(a)
    bh, bl = _split(b)
    return _dot(ah, bh) + (_dot(ah, bl) + _dot(al, bh))


def _sigmoid(x):
    return jax.nn.sigmoid(x)


def _cparams(n_axes, vmem_mib):
    return pltpu.CompilerParams(
        dimension_semantics=("arbitrary",) * n_axes,
        vmem_limit_bytes=min(vmem_mib * MIB, V7X_VMEM_BYTES - 4 * MIB),
    )


def _const_spec(shape):
    nd = len(shape)
    return pl.BlockSpec(shape, lambda *_: (0,) * nd, pipeline_mode=pl.Buffered(1))


def _mod_kernel(c_ref, w_ref, b_ref, o_ref):
    c = c_ref[...]
    s = c * _sigmoid(c)
    o_ref[0] = _dot3(s, w_ref[0]) + b_ref[0]


def _modulation(cond8, w_mod, b_mod):
    n_layers, d, n = w_mod.shape
    nb = 1536
    return pl.pallas_call(
        _mod_kernel,
        grid=(n_layers, n // nb),
        in_specs=[
            pl.BlockSpec((8, d), lambda l, j: (0, 0)),
            pl.BlockSpec((1, d, nb), lambda l, j: (l, 0, j)),
            pl.BlockSpec((1, 1, nb), lambda l, j: (l, 0, j)),
        ],
        out_specs=pl.BlockSpec((1, 8, nb), lambda l, j: (l, 0, j)),
        out_shape=jax.ShapeDtypeStruct((n_layers, 8, n), F32),
        compiler_params=_cparams(2, 40),
        name="adaln_modulation",
    )(cond8, w_mod, b_mod.reshape(n_layers, 1, n))


def _rope128(x, c, s1, s2):
    return x * c + pltpu.roll(x, V7X_LANES - 16, 1) * s1 + pltpu.roll(x, 16, 1) * s2


def _inproj_kernel(x_ref, cx_ref, sh_ref, a_ref, csh_ref, ca_ref, w_ref, qg_ref, kg_ref, c_ref, s1_ref, s2_ref,
                   gmq_ref, gmk_ref, q_ref, k_ref, vt_ref, hq_ref, hf_ref, hi_ref, hg_ref, ut_ref, gate_ref, *, n_lat):
    is_ctx = pl.program_id(1) >= n_lat
    x = jnp.where(is_ctx, cx_ref[0], x_ref[0])
    a = jnp.where(is_ctx, ca_ref[0], a_ref[0])
    sh = jnp.where(is_ctx, csh_ref[0], sh_ref[0])
    ms = jnp.mean(x * x, axis=-1, keepdims=True)
    h = (x * lax.rsqrt(ms + RMS_EPS)) * a + sh
    hb = h.astype(BF16)

    def proj(lo, hi):
        return _dot(hb, w_ref[:, lo:hi])

    c, s1, s2 = c_ref[...], s1_ref[...], s2_ref[...]

    zq = proj(O_Q, O_K)
    msq = _dot((zq * zq).astype(BF16), gmq_ref[...])
    qn = zq * lax.rsqrt(msq + RMS_EPS) * qg_ref[...]
    for j in range(ATT_WIDTH // V7X_LANES):
        sl = slice(j * V7X_LANES, (j + 1) * V7X_LANES)
        q_ref[0, :, sl] = _rope128(qn[:, sl], c, s1, s2).astype(BF16)

    zkv = proj(O_K, O_HQ)
    zk = zkv[:, 0:ATT_KV_WIDTH]
    msk = _dot((zk * zk).astype(BF16), gmk_ref[...])
    kn = _rope128(zk * lax.rsqrt(msk + RMS_EPS) * kg_ref[...], c, s1, s2)
    vt = zkv[:, ATT_KV_WIDTH:].T
    tm = x.shape[0]
    ones_row = lax.broadcasted_iota(jnp.int32, (ATTN_V_ROWS - HEAD_DIM, tm), 0) == 0
    for hd in range(ATT_KV_HEADS):
        k_ref[0, hd] = kn[:, hd * HEAD_DIM:(hd + 1) * HEAD_DIM].astype(BF16)
        vt_ref[0, hd, 0:HEAD_DIM, :] = vt[hd * HEAD_DIM:(hd + 1) * HEAD_DIM, :].astype(BF16)
        vt_ref[0, hd, HEAD_DIM:ATTN_V_ROWS, :] = jnp.where(ones_row, 1.0, 0.0).astype(BF16)

    hq_ref[0] = proj(O_HQ, O_HF).astype(BF16)
    hf_ref[0] = proj(O_HF, O_HI)
    hi_ref[0] = proj(O_HI, O_HG).astype(BF16)
    hg_ref[0] = proj(O_HG, O_U).astype(BF16)
    ut = proj(O_U, O_GATE).T
    for j in range(tm // S5_CHUNK):
        ut_ref[0, j] = ut[:, j * S5_CHUNK:(j + 1) * S5_CHUNK]
    for j in range(3):
        lo = O_GATE + j * D_MODEL
        gate_ref[0, :, j * D_MODEL:(j + 1) * D_MODEL] = _sigmoid(proj(lo, lo + D_MODEL)).astype(BF16)


def _inproj(x, cx, sh, a, csh, ca, w_in, qg, kg, rope, gmq, gmk):
    b, t, d = x.shape
    t_ctx = cx.shape[1]
    tm = INPROJ_TILE
    n_lat, n_ctx = t // tm, t_ctx // tm
    t_all = t + t_ctx
    c, s1, s2 = rope
    row = lambda bi, i: (bi, i, 0)
    vec = lambda bi, i: (bi, 0, 0)
    tab = lambda bi, i: (i, 0)

    def widths(specs):
        shapes = [jax.ShapeDtypeStruct((b, t_all, w), dt) for w, dt in specs]
        return shapes, [pl.BlockSpec((1, tm, w), row) for w, _ in specs]

    q_shape, q_spec = widths([(ATT_WIDTH, BF16)])
    h_shape, h_spec = widths([(HG_WIDTH, BF16), (2 * HG_WIDTH, F32), (HG_WIDTH, BF16), (HG_WIDTH, BF16)])
    g_shape, g_spec = widths([(3 * D_MODEL, BF16)])
    out_shape = q_shape + [
        jax.ShapeDtypeStruct((b, ATT_KV_HEADS, t_all, HEAD_DIM), BF16),
        jax.ShapeDtypeStruct((b, ATT_KV_HEADS, ATTN_V_ROWS, t_all), BF16),
    ] + h_shape + [jax.ShapeDtypeStruct((b, t_all // S5_CHUNK, SSM_WIDTH, S5_CHUNK), F32)] + g_shape
    out_specs = q_spec + [
        pl.BlockSpec((1, ATT_KV_HEADS, tm, HEAD_DIM), lambda bi, i: (bi, 0, i, 0)),
        pl.BlockSpec((1, ATT_KV_HEADS, ATTN_V_ROWS, tm), lambda bi, i: (bi, 0, 0, i)),
    ] + h_spec + [pl.BlockSpec((1, tm // S5_CHUNK, SSM_WIDTH, S5_CHUNK), lambda bi, i: (bi, i, 0, 0))] + g_spec
    return pl.pallas_call(
        functools.partial(_inproj_kernel, n_lat=n_lat),
        grid=(b, n_lat + n_ctx),
        in_specs=[
            pl.BlockSpec((1, tm, d), lambda bi, i: (bi, jnp.minimum(i, n_lat - 1), 0)),
            pl.BlockSpec((1, tm, d), lambda bi, i: (bi, jnp.maximum(i - n_lat, 0), 0)),
            pl.BlockSpec((1, 1, d), vec),
            pl.BlockSpec((1, 1, d), vec),
            pl.BlockSpec((1, 1, d), lambda bi, i: (0, 0, 0)),
            pl.BlockSpec((1, 1, d), lambda bi, i: (0, 0, 0)),
            _const_spec((d, N_IN)),
            _const_spec((1, ATT_WIDTH)),
            _const_spec((1, ATT_KV_WIDTH)),
            pl.BlockSpec((tm, V7X_LANES), tab),
            pl.BlockSpec((tm, V7X_LANES), tab),
            pl.BlockSpec((tm, V7X_LANES), tab),
            _const_spec((ATT_WIDTH, ATT_WIDTH)),
            _const_spec((ATT_KV_WIDTH, ATT_KV_WIDTH)),
        ],
        out_specs=out_specs,
        out_shape=out_shape,
        compiler_params=_cparams(2, 56),
        name="in_projection",
    )(x, cx, sh, a, csh, ca, w_in, qg, kg, c, s1, s2, gmq, gmk)


def _attn_kernel(q_ref, k_ref, vt_ref, o_ref, *scratch, tk, nkb):
    n_sub = q_ref.shape[1] // ATTN_Q_SUBTILE
    per = len(scratch) // n_sub
    for i in range(n_sub):
        _attn_subtile(q_ref, k_ref, vt_ref, o_ref, i, *scratch[i * per:(i + 1) * per], tk=tk, nkb=nkb)


def _attn_subtile(q_ref, k_ref, vt_ref, o_ref, i, qt_ref, m_ref, acc_ref, sa_ref, sb_ref, mxa_ref, mxb_ref, *, tk, nkb):
    tq = ATTN_Q_SUBTILE
    rows = slice(i * tq, (i + 1) * tq)
    qt = q_ref[0, rows, :].astype(F32).T
    qt_ref[...] = jnp.concatenate(
        [qt[HEAD_DIM * g:HEAD_DIM * (g + 1), :] for g in range(ATT_GROUP)], axis=1).astype(BF16)
    m_ref[...] = jnp.full(m_ref.shape, -jnp.inf, F32)
    acc_ref[...] = jnp.zeros(acc_ref.shape, F32)

    def scores(kb, s_ref, mx_ref):
        off = pl.multiple_of(kb * tk, tk)
        s = _dot(k_ref[0, 0, pl.ds(off, tk), :], qt_ref[...])
        s_ref[...] = s
        mx_ref[...] = jnp.max(s, axis=0, keepdims=True)

    def consume(kb, s_ref, mx_ref):
        off = pl.multiple_of(kb * tk, tk)
        m_prev = m_ref[...]
        m_new = jnp.maximum(m_prev, mx_ref[...])
        alpha = jnp.exp2(m_prev - m_new)
        p = jnp.exp2(s_ref[...] - m_new).astype(BF16)
        acc_ref[...] = alpha * acc_ref[...] + _dot(vt_ref[0, 0, :, pl.ds(off, tk)], p)
        m_ref[...] = m_new

    bufs = ((sa_ref, mxa_ref), (sb_ref, mxb_ref))
    scores(0, *bufs[0])

    def body(it, carry):
        base = ATTN_BLOCKS_PER_ITER * it
        for j in range(ATTN_BLOCKS_PER_ITER):
            scores(base + j + 1, *bufs[(j + 1) % 2])
            consume(base + j, *bufs[j % 2])
        return carry

    n_it = (nkb - 1) // ATTN_BLOCKS_PER_ITER
    lax.fori_loop(0, n_it, body, 0)
    for r in range(n_it * ATTN_BLOCKS_PER_ITER, nkb):
        if r + 1 < nkb:
            scores(r + 1, *bufs[(r + 1) % 2])
        consume(r, *bufs[r % 2])

    acc = acc_ref[...]
    out_t = acc[0:HEAD_DIM, :] / acc[HEAD_DIM:HEAD_DIM + 1, :]
    out_t = jnp.concatenate([out_t[:, g * tq:(g + 1) * tq] for g in range(ATT_GROUP)], axis=0)
    o_ref[0, rows, :] = out_t.T.astype(BF16)


ATTN_KEY_BLOCKS = (640, 512, 256, 128)
ATTN_BLOCKS_PER_ITER = 12


def _attention(q, k, vt_ext, q_start, q_len, k_start, k_len):
    b = q.shape[0]
    tq = min(ATTN_Q_TILE, q_len)
    tk = next(c for c in ATTN_KEY_BLOCKS if k_len % c == 0)
    assert q_start % tq == 0 and q_len % tq == 0 and k_start % k_len == 0 and tq % ATTN_Q_SUBTILE == 0
    q_off, k_blk = q_start // tq, k_start // k_len
    kern = functools.partial(_attn_kernel, tk=tk, nkb=k_len // tk)
    gw = ATT_GROUP * HEAD_DIM
    m = ATT_GROUP * ATTN_Q_SUBTILE
    return pl.pallas_call(
        kern,
        grid=(b, ATT_KV_HEADS, q_len // tq),
        in_specs=[
            pl.BlockSpec((1, tq, gw), lambda bi, h, i: (bi, i + q_off, h)),
            pl.BlockSpec((1, 1, k_len, HEAD_DIM), lambda bi, h, i: (bi, h, k_blk, 0)),
            pl.BlockSpec((1, 1, ATTN_V_ROWS, k_len), lambda bi, h, i: (bi, h, 0, k_blk)),
        ],
        out_specs=pl.BlockSpec((1, tq, gw), lambda bi, h, i: (bi, i, h)),
        out_shape=jax.ShapeDtypeStruct((b, q_len, ATT_WIDTH), BF16),
        scratch_shapes=[
            pltpu.VMEM((HEAD_DIM, m), BF16),
            pltpu.VMEM((1, m), F32),
            pltpu.VMEM((ATTN_V_ROWS, m), F32),
            pltpu.VMEM((tk, m), F32),
            pltpu.VMEM((tk, m), F32),
            pltpu.VMEM((1, m), F32),
            pltpu.VMEM((1, m), F32),
        ] * (tq // ATTN_Q_SUBTILE),
        compiler_params=_cparams(3, 48),
        name="gqa_attention",
    )(q, k, vt_ext)


HG_LEVELS = 7
HG_SEL_LEVELS = 1


def _hgrn_kernel(qf_ref, ff_ref, vf_ref, qb_ref, fb_ref, vb_ref, lb_ref, s0_ref,
                 tri_ref, sel_ref, sm_ref, hm_ref, hmt_ref, gsum_ref, bd_ref,
                 of_ref, ob_ref, sfin_ref, stf_ref, stb_ref, cf_s, cb_s):
    @pl.when(pl.program_id(1) == 0)
    def _():
        stf_ref[...] = s0_ref[0, 0]
        stb_ref[...] = s0_ref[0, 1]

    tt, w = HG_TILE, HG_WIDTH
    dirs = (0, 1)
    q_refs, f_refs, v_refs = (qf_ref, qb_ref), (ff_ref, fb_ref), (vf_ref, vb_ref)
    o_refs, st_refs, c_refs = (of_ref, ob_ref), (stf_ref, stb_ref), (cf_s, cb_s)
    hm = [hm_ref[h] for h in range(HG_HEADS)]
    hmt = [hmt_ref[h] for h in range(HG_HEADS)]

    qs, kin, vb, c, tot, ref_small, o, scores = [], [], [], [], [], [], [], [None, None]
    for d in dirs:
        q = q_refs[d][0].astype(F32)
        fpre = f_refs[d][0]
        lb = lb_ref[d:d + 1, :]
        qs.append(q * _sigmoid(q))
        kin.append((1.0 - lb) * _sigmoid(-fpre))
        vb.append(v_refs[d][0])
        hi, lo = _split(jnp.log(lb + (1.0 - lb) * _sigmoid(fpre)))
        cd = (_dot(tri_ref[d], hi) + _dot(tri_ref[d], lo)) * LOG2E
        c_refs[d][...] = cd
        c.append(cd)
    for d in dirs:
        last = 0 if d else tt - 1
        tot.append(c_refs[d][last:last + 1, :])
        chi, clo = _split(c[d])
        ref_small.append(_dot(sel_ref[d], chi) + _dot(sel_ref[d], clo))
        o.append(_dot((qs[d] * kin[d]).astype(BF16), gsum_ref[...]) * vb[d].astype(F32))

    for lvl in range(HG_LEVELS):
        m = 1 << lvl
        for d in dirs:
            if lvl < HG_SEL_LEVELS:
                cref = ref_small[d][lvl * tt:(lvl + 1) * tt, :]
            else:
                rows = []
                for blk in range(tt // (2 * m)):
                    r = blk * 2 * m + (m if d else m - 1)
                    rows.append(jnp.broadcast_to(c_refs[d][r:r + 1, :], (2 * m, w)))
                cref = rows[0] if len(rows) == 1 else jnp.concatenate(rows, axis=0)
            wgt = jnp.exp2(-jnp.abs(c[d] - cref))
            ql = (qs[d] * wgt).astype(BF16)
            kl_t = (kin[d] * wgt).T.astype(BF16)
            kstack_t = jnp.concatenate([kl_t * hmt[h] for h in range(HG_HEADS)], axis=1)
            sc = _dot(ql, kstack_t).astype(BF16) * sm_ref[d, lvl]
            scores[d] = sc if scores[d] is None else scores[d] + sc

    for d in dirs:
        vstack = jnp.concatenate([vb[d] * hm[h] for h in range(HG_HEADS)], axis=0)
        o[d] = o[d] + _dot(scores[d], vstack)
    for d in dirs:
        st = st_refs[d][...]
        o[d] = o[d] + _dot_nt((qs[d] * jnp.exp2(c[d])).astype(BF16), st.astype(BF16))
        kv = _dot_tn(vb[d], (kin[d] * jnp.exp2(tot[d] - c[d])).astype(BF16))
        st_new = jnp.exp2(tot[d]) * st + kv * bd_ref[...]
        st_refs[d][...] = st_new
        sfin_ref[0, d] = st_new
        o_refs[d][0] = o[d].astype(BF16)


def _hgrn_constants():
    tt, w = HG_TILE, HG_WIDTH
    t = np.arange(tt)
    tri = np.stack([t[None, :] <= t[:, None], t[None, :] >= t[:, None]]).astype(BF16)
    sel = []
    for reverse in (False, True):
        per = []
        for lvl in range(HG_SEL_LEVELS):
            m = 1 << lvl
            r = (t // (2 * m)) * (2 * m) + (m if reverse else m - 1)
            per.append(t[None, :] == r[:, None])
        sel.append(np.concatenate(per, axis=0))
    sel = np.stack(sel).astype(BF16)
    col = np.arange(HG_HEADS * tt) % tt
    sm = []
    for reverse in (False, True):
        per = []
        for lvl in range(HG_LEVELS):
            t_up, s_up = ((t >> lvl) & 1) == 1, ((col >> lvl) & 1) == 1
            same = (t[:, None] >> (lvl + 1)) == (col[None, :] >> (lvl + 1))
            halves = (~t_up[:, None] & s_up[None, :]) if reverse else (t_up[:, None] & ~s_up[None, :])
            per.append(same & halves)
        sm.append(np.stack(per))
    sm = np.stack(sm).astype(BF16)
    lane_head = np.arange(w) // HG_DK
    hm = np.stack([np.broadcast_to((lane_head == h)[None, :], (tt, w)) for h in range(HG_HEADS)]).astype(BF16)
    same_head = lane_head[:, None] == lane_head[None, :]
    hmt = np.ascontiguousarray(np.swapaxes(hm, 1, 2))
    return tri, sel, sm, hm, hmt, same_head.astype(BF16), same_head.astype(F32)


def _hgrn(hq, hf, hv, lb2, s0, consts, start, t):
    b, _, w = hq.shape
    tt = HG_TILE
    nt, off = t // tt, start // tt
    fwd = lambda bi, i: (bi, off + i, 0)
    bwd = lambda bi, i: (bi, off + nt - 1 - i, 0)
    bwd_f = lambda bi, i: (bi, off + nt - 1 - i, 1)
    fwd_o = lambda bi, i: (bi, i, 0)
    bwd_o = lambda bi, i: (bi, nt - 1 - i, 0)
    st_spec = pl.BlockSpec((1, 2, w, w), lambda bi, i: (bi, 0, 0, 0))
    return pl.pallas_call(
        _hgrn_kernel,
        grid=(b, nt),
        in_specs=[
            pl.BlockSpec((1, tt, w), fwd), pl.BlockSpec((1, tt, w), fwd), pl.BlockSpec((1, tt, w), fwd),
            pl.BlockSpec((1, tt, w), bwd), pl.BlockSpec((1, tt, w), bwd_f), pl.BlockSpec((1, tt, w), bwd),
            pl.BlockSpec((2, w), lambda bi, i: (0, 0)),
            st_spec,
        ] + [_const_spec(a.shape) for a in consts],
        out_specs=[pl.BlockSpec((1, tt, w), fwd_o), pl.BlockSpec((1, tt, w), bwd_o), st_spec],
        out_shape=[
            jax.ShapeDtypeStruct((b, t, w), BF16),
            jax.ShapeDtypeStruct((b, t, w), BF16),
            jax.ShapeDtypeStruct((b, 2, w, w), F32),
        ],
        scratch_shapes=[
            pltpu.VMEM((w, w), F32),
            pltpu.VMEM((w, w), F32),
            pltpu.VMEM((tt, w), F32),
            pltpu.VMEM((tt, w), F32),
        ],
        compiler_params=_cparams(2, 32),
        name="hgrn2_scan",
    )(hq, hf, hv, hq, hf, hv, lb2, s0, *consts)


def _cpow(a_re, a_im, tau):
    mag = jnp.exp(tau * a_re)
    ang = tau * a_im
    return mag * jnp.cos(ang), mag * jnp.sin(ang)


def _s5_build_weights(ar_row, ai_row, ar_col, ai_col, lr_row, li_row, bt_r, bt_i, cr, ci, ctr, cti,
                      w_ref, wst_ref, wout_ref, laml_ref, kall_ref):
    lc, p, c = S5_CHUNK, SSM_STATE, SSM_GROUP
    tau_l = lax.broadcasted_iota(jnp.int32, (p, lc), 1).astype(F32)
    tau_s = lax.broadcasted_iota(jnp.int32, (lc, p), 0).astype(F32)

    btr, bti = [], []
    for d in range(2):
        e_r, e_i = _cpow(ar_row[0, d], ai_row[0, d], 1.0)
        l_r, l_i = lr_row[0, d], li_row[0, d]
        den = l_r * l_r + l_i * l_i
        f_r = ((e_r - 1.0) * l_r + e_i * l_i) / den
        f_i = (e_i * l_r - (e_r - 1.0) * l_i) / den
        btr.append(f_r * bt_r[0, d] - f_i * bt_i[0, d])
        bti.append(f_r * bt_i[0, d] + f_i * bt_r[0, d])

    def cb(d):
        re, im = [], []
        for c1 in range(c):
            b_r, b_i = btr[d][c1:c1 + 1, :], bti[d][c1:c1 + 1, :]
            re.append(b_r * cr[0, d] - b_i * ci[0, d])
            im.append(b_r * ci[0, d] + b_i * cr[0, d])
        return jnp.concatenate(re, axis=0), jnp.concatenate(im, axis=0)

    cbf_r, cbf_i = cb(0)
    pf_r, pf_i = _cpow(ar_col[0, 0], ai_col[0, 0], tau_l)
    kf = _dot3(cbf_r, pf_r) - _dot3(cbf_i, pf_i)
    cbb_r, cbb_i = cb(1)
    pb_r, pb_i = _cpow(ar_col[0, 1], ai_col[0, 1], lc - tau_l)
    kb = _dot3(cbb_r, pb_r) - _dot3(cbb_i, pb_i)
    lane = lax.broadcasted_iota(jnp.int32, (c * c, lc), 1)
    kf = kf + jnp.where(lane == 0, jnp.sum(cbb_r, axis=1, keepdims=True), 0.0)
    kall_ref[...] = jnp.concatenate([kf, kb], axis=1)

    def toeplitz_rows(c1, carry):
        for c2 in range(c):
            row = kall_ref[pl.ds(c1 * c + c2, 1), :]
            blk = pltpu.roll(jnp.broadcast_to(row, (lc, 2 * lc)), 0, 1, stride=1, stride_axis=0)
            w_ref[pl.ds(pl.multiple_of(c1 * lc, lc), lc), c2 * lc:(c2 + 1) * lc] = blk[:, :lc].astype(BF16)
        return carry

    lax.fori_loop(0, c, toeplitz_rows, 0)

    sf_r, sf_i = _cpow(ar_row[0, 0], ai_row[0, 0], (lc - 1) - tau_s)
    sb_r, sb_i = _cpow(ar_row[0, 1], ai_row[0, 1], tau_s)
    for c1 in range(c):
        re, im = [], []
        for d, (p_r, p_i) in enumerate(((sf_r, sf_i), (sb_r, sb_i))):
            b_r, b_i = btr[d][c1:c1 + 1, :], bti[d][c1:c1 + 1, :]
            re.append(p_r * b_r - p_i * b_i)
            im.append(p_r * b_i + p_i * b_r)
        wst_ref[c1 * lc:(c1 + 1) * lc, :] = jnp.concatenate(re + im, axis=1).astype(BF16)

    of_r, of_i = _cpow(ar_col[0, 0], ai_col[0, 0], tau_l + 1.0)
    ob_r, ob_i = _cpow(ar_col[0, 1], ai_col[0, 1], lc - tau_l)
    for c2 in range(c):
        re, im = [], []
        for d, (p_r, p_i) in enumerate(((of_r, of_i), (ob_r, ob_i))):
            c_r, c_i = ctr[0, d, :, c2:c2 + 1], cti[0, d, :, c2:c2 + 1]
            re.append(c_r * p_r - c_i * p_i)
            im.append(-(c_r * p_i + c_i * p_r))
        wout_ref[:, c2 * lc:(c2 + 1) * lc] = jnp.concatenate(re + im, axis=0).astype(BF16)

    lf_r, lf_i = _cpow(ar_row[0, 0], ai_row[0, 0], float(lc))
    lb_r, lb_i = _cpow(ar_row[0, 1], ai_row[0, 1], float(lc))
    laml_ref[0:1, :] = jnp.concatenate([lf_r, lb_r], axis=1)
    laml_ref[1:2, :] = jnp.concatenate([lf_i, lb_i], axis=1)


def _s5_kernel(u_ref, ar_row, ai_row, ar_col, ai_col, lr_row, li_row, bt_r, bt_i, cr, ci, ctr, cti, dsk_ref, y_ref,
               w_ref, wst_ref, wout_ref, laml_ref, kall_ref, xloc_s, xin_s, *, nctx, nck):
    c, lc, p, p2 = SSM_GROUP, S5_CHUNK, SSM_STATE, 2 * SSM_STATE

    @pl.when(pl.program_id(1) == 0)
    def _():
        _s5_build_weights(ar_row, ai_row, ar_col, ai_col, lr_row, li_row, bt_r, bt_i, cr, ci, ctr, cti,
                          w_ref, wst_ref, wout_ref, laml_ref, kall_ref)

    ub = jnp.concatenate([u_ref[0, :, c1, :] for c1 in range(c)], axis=1).astype(BF16)
    xloc_s[...] = _dot(ub, wst_ref[...])

    nlat = nck - nctx
    order_f = list(range(nlat, nck)) + list(range(nlat))
    order_b = list(range(nck - 1, nlat - 1, -1)) + list(range(nlat - 1, -1, -1))
    m_r, m_i = laml_ref[0:1, :], laml_ref[1:2, :]
    is_fwd = lax.broadcasted_iota(jnp.int32, (1, p2), 1) < p
    x_r = jnp.zeros((1, p2), F32)
    x_i = jnp.zeros((1, p2), F32)
    for kf, kb in zip(order_f, order_b):
        xin_s[kf:kf + 1, 0:p] = x_r[:, 0:p]
        xin_s[kb:kb + 1, p:p2] = x_r[:, p:p2]
        xin_s[kf:kf + 1, p2:p2 + p] = x_i[:, 0:p]
        xin_s[kb:kb + 1, p2 + p:2 * p2] = x_i[:, p:p2]
        loc_r = jnp.where(is_fwd, xloc_s[kf:kf + 1, 0:p2], xloc_s[kb:kb + 1, 0:p2])
        loc_i = jnp.where(is_fwd, xloc_s[kf:kf + 1, p2:2 * p2], xloc_s[kb:kb + 1, p2:2 * p2])
        x_r, x_i = m_r * x_r - m_i * x_i + loc_r, m_r * x_i + m_i * x_r + loc_i

    y = _dot(ub, w_ref[...]) + _dot(xin_s[...].astype(BF16), wout_ref[...])
    for c2 in range(c):
        y_ref[0, :, c2, :] = y[:, c2 * lc:(c2 + 1) * lc] + dsk_ref[0, c2:c2 + 1, :] * u_ref[0, :, c2, :]


def _s5(u_t, a_re, a_im, lam_re, lam_im, b_re, b_im, c_re, c_im, dsk, nctx):
    b, nck, wd, lc = u_t.shape
    g, c, p = SSM_GROUPS, SSM_GROUP, SSM_STATE
    n = c * lc
    row = lambda x: x.reshape(g, 2, 1, p)
    col = lambda x: x.reshape(g, 2, p, 1)
    tr = lambda x: jnp.swapaxes(x, -1, -2)
    params = [row(a_re), row(a_im), col(a_re), col(a_im), row(lam_re), row(lam_im),
              tr(b_re), tr(b_im), c_re, c_im, tr(c_re), tr(c_im)]
    kern = functools.partial(_s5_kernel, nctx=nctx, nck=nck)
    gspec = lambda shp: pl.BlockSpec((1,) + shp, lambda gi, bi: (gi,) + (0,) * len(shp))
    io_spec = pl.BlockSpec((1, nck, c, lc), lambda gi, bi: (bi, 0, gi, 0))
    return pl.pallas_call(
        kern,
        grid=(g, b),
        in_specs=[io_spec] + [gspec(a.shape[1:]) for a in params] + [gspec((c, lc))],
        out_specs=io_spec,
        out_shape=jax.ShapeDtypeStruct(u_t.shape, F32),
        scratch_shapes=[
            pltpu.VMEM((n, n), BF16),
            pltpu.VMEM((n, 4 * p), BF16),
            pltpu.VMEM((4 * p, n), BF16),
            pltpu.VMEM((2, 2 * p), F32),
            pltpu.VMEM((c * c, 2 * lc), F32),
            pltpu.VMEM((nck, 4 * p), F32),
            pltpu.VMEM((nck, 4 * p), F32),
        ],
        compiler_params=_cparams(2, 48),
        name="s5_scan",
    )(u_t, *params, dsk)


def _merge_kernel(x_ref, att_ref, of_ref, ob_ref, hg_ref, yt_ref, gate_ref, g1_ref,
                  wa_ref, wr_ref, ws_ref, wo_ref, wglu_ref, bglu_ref, hn_ref, gm_ref, o_ref):
    r = of_ref[0].astype(F32) + ob_ref[0].astype(F32)
    ms = _dot((r * r).astype(BF16), gm_ref[...])
    g = hg_ref[0].astype(F32)
    yrec = (r * lax.rsqrt(ms + RMS_EPS) * hn_ref[...]) * (g * _sigmoid(g))

    ys = jnp.concatenate([yt_ref[0, j].T for j in range(yt_ref.shape[1])], axis=0)
    z = 0.5 * ys * (1.0 + jnp.tanh(math.sqrt(2.0 / math.pi) * (ys + 0.044715 * (ys * ys * ys))))
    yssm = z * _sigmoid(_dot(z.astype(BF16), wglu_ref[...]) + bglu_ref[...])

    d = D_MODEL
    m = gate_ref[0, :, 0:d].astype(F32) * _dot(att_ref[0], wa_ref[...])
    m = m + gate_ref[0, :, d:2 * d].astype(F32) * _dot(yrec.astype(BF16), wr_ref[...])
    m = m + gate_ref[0, :, 2 * d:3 * d].astype(F32) * _dot(yssm.astype(BF16), ws_ref[...])
    y = _dot(m.astype(BF16), wo_ref[...])
    o_ref[0] = x_ref[0] + g1_ref[0] * y


def _merge(x, att, o_f, o_b, hg, y_t, t_off, gates, g1, wa, wr, ws, wo, wglu, bglu, hn, gm):
    b, t, d = x.shape
    tm = min(512, t)
    off = t_off // tm
    row = lambda bi, i: (bi, i, 0)
    row_off = lambda bi, i: (bi, i + off, 0)
    vec = lambda bi, i: (bi, 0, 0)
    return pl.pallas_call(
        _merge_kernel,
        grid=(b, t // tm),
        in_specs=[
            pl.BlockSpec((1, tm, d), row),
            pl.BlockSpec((1, tm, ATT_WIDTH), row),
            pl.BlockSpec((1, tm, HG_WIDTH), row),
            pl.BlockSpec((1, tm, HG_WIDTH), row),
            pl.BlockSpec((1, tm, HG_WIDTH), row_off),
            pl.BlockSpec((1, tm // S5_CHUNK, SSM_WIDTH, S5_CHUNK), lambda bi, i: (bi, i + off, 0, 0)),
            pl.BlockSpec((1, tm, 3 * d), row_off),
            pl.BlockSpec((1, 1, d), vec),
            _const_spec(wa.shape), _const_spec(wr.shape), _const_spec(ws.shape), _const_spec(wo.shape),
            _const_spec(wglu.shape), _const_spec(bglu.shape), _const_spec(hn.shape), _const_spec(gm.shape),
        ],
        out_specs=pl.BlockSpec((1, tm, d), row),
        out_shape=jax.ShapeDtypeStruct((b, t, d), F32),
        compiler_params=_cparams(2, 48),
        name="merge_branches",
    )(x, att, o_f, o_b, hg, y_t, gates, g1, wa, wr, ws, wo, wglu, bglu, hn, gm)


def _ffn_kernel(x_ref, sh_ref, a_ref, g_ref, wup_ref, wdn_ref, o_ref, *, nj):
    x = x_ref[0]
    ms = jnp.mean(x * x, axis=-1, keepdims=True)
    hb = ((x * lax.rsqrt(ms + RMS_EPS)) * a_ref[0] + sh_ref[0]).astype(BF16)
    f = FFN_HIDDEN
    fc = f // nj
    acc = None
    for j in range(nj):
        a = _dot(hb, wup_ref[:, j * fc:(j + 1) * fc])
        bgate = _dot(hb, wup_ref[:, f + j * fc:f + (j + 1) * fc])
        act = ((a * _sigmoid(a)) * bgate).astype(BF16)
        part = _dot(act, wdn_ref[j * fc:(j + 1) * fc, :])
        acc = part if acc is None else acc + part
    o_ref[0] = x + g_ref[0] * acc


def _ffn(x, sh, a, g, wup, wdn):
    b, t, d = x.shape
    tm = min(512, t)
    row = lambda bi, i: (bi, i, 0)
    vec = lambda bi, i: (bi, 0, 0)
    return pl.pallas_call(
        functools.partial(_ffn_kernel, nj=FFN_HIDDEN_CHUNKS),
        grid=(b, t // tm),
        in_specs=[
            pl.BlockSpec((1, tm, d), row),
            pl.BlockSpec((1, 1, d), vec), pl.BlockSpec((1, 1, d), vec), pl.BlockSpec((1, 1, d), vec),
            _const_spec(wup.shape), _const_spec(wdn.shape),
        ],
        out_specs=pl.BlockSpec((1, tm, d), row),
        out_shape=jax.ShapeDtypeStruct((b, t, d), F32),
        compiler_params=_cparams(2, 56),
        name="swiglu_ffn",
    )(x, sh, a, g, wup, wdn)


def _rope_tables(t, t_ctx):
    pos = jnp.arange(t)
    row = (pos // GRID_W).astype(F32)
    col = (pos % GRID_W).astype(F32)
    axis_dim = HEAD_DIM // 2
    inv = ROPE_THETA ** (-jnp.arange(0, axis_dim, 2, dtype=F32) / axis_dim)
    ang_r, ang_c = row[:, None] * inv, col[:, None] * inv
    cr, sr, cc, sc = jnp.cos(ang_r), jnp.sin(ang_r), jnp.cos(ang_c), jnp.sin(ang_c)
    z = jnp.zeros_like(cr)
    rep = V7X_LANES // HEAD_DIM
    c = jnp.tile(jnp.concatenate([cr, cr, cc, cc], axis=1), (1, rep))
    s1 = jnp.tile(jnp.concatenate([-sr, z, -sc, z], axis=1), (1, rep))
    s2 = jnp.tile(jnp.concatenate([z, sr, z, sc], axis=1), (1, rep))
    pad = lambda a, fill: jnp.concatenate([a, jnp.full((t_ctx, V7X_LANES), fill, F32)], axis=0)
    return pad(c, 1.0), pad(s1, 0.0), pad(s2, 0.0)


def _group_mean_matrix(width, group):
    i = np.arange(width) // group
    return np.where(i[:, None] == i[None, :], 1.0 / group, 0.0).astype(BF16)


def kernel(x, c, ctx, c_ctx, w_mod, b_mod, norm1_g, norm2_g, w_in, q_norm_g, k_norm_g, hgrn_lb, hgrn_norm_g,
           ssm_lam_re, ssm_lam_im, ssm_log_dt, ssm_b_re, ssm_b_im, ssm_c_re, ssm_c_im, ssm_d, w_glu, b_glu,
           w_br_attn, w_br_hgrn, w_br_ssm, w_out, w_ffn_up, w_ffn_down):
    bsz, t, d = x.shape
    t_ctx = ctx.shape[1]
    t_all = t + t_ctx
    depth = w_mod.shape[0]
    assert t % 512 == 0 and t_ctx % INPROJ_TILE == 0 and t % t_ctx == 0 and d == D_MODEL

    lb_soft = jax.nn.softmax(hgrn_lb.astype(F32), axis=0)
    lower_bounds = jnp.cumsum(lb_soft, axis=0) - lb_soft[0]
    rope = _rope_tables(t, t_ctx)
    gmq = _group_mean_matrix(ATT_WIDTH, HEAD_DIM)
    gmk = _group_mean_matrix(ATT_KV_WIDTH, HEAD_DIM)
    gmh = _group_mean_matrix(HG_WIDTH, HG_DK)
    hg_consts = _hgrn_constants()
    cond8 = jnp.zeros((8, d), F32).at[:bsz].set(c).at[bsz].set(c_ctx)
    mods = _modulation(cond8, w_mod, b_mod)

    x_lat, x_ctx = x, ctx
    for l in range(depth):
        with_ctx = l < depth - 1
        ml = mods[l, :bsz].reshape(bsz, ADALN_CHUNKS, 1, d)
        mc = mods[l, bsz].reshape(1, ADALN_CHUNKS, 1, d)
        sh1, sc1, g1, sh2, sc2, g2 = [ml[:, i] for i in range(ADALN_CHUNKS)]
        csh1, csc1, cg1, csh2, csc2, cg2 = [mc[:, i] for i in range(ADALN_CHUNKS)]
        n1, n2 = norm1_g[l].reshape(1, 1, d), norm2_g[l].reshape(1, 1, d)

        qg = (jnp.tile(q_norm_g[l], ATT_HEADS) * (HEAD_DIM ** -0.5 * LOG2E)).reshape(1, ATT_WIDTH)
        kg = jnp.tile(k_norm_g[l], ATT_KV_HEADS).reshape(1, ATT_KV_WIDTH)
        q, k, vt, hq, hf, hi, hg, u_t, gates = _inproj(
            x_lat, x_ctx, sh1, n1 * (1.0 + sc1), csh1, n1 * (1.0 + csc1), w_in[l].astype(BF16), qg, kg, rope, gmq, gmk)

        a_lat = _attention(q, k, vt, 0, t, 0, t_all)
        if with_ctx:
            a_ctx = _attention(q, k, vt, t, t_ctx, t, t_ctx)

        lb2 = lower_bounds[l]
        s_zero = jnp.zeros((bsz, 2, HG_WIDTH, HG_WIDTH), F32)
        of_c, ob_c, s_ctx = _hgrn(hq, hf, hi, lb2, s_zero, hg_consts, t, t_ctx)
        of_l, ob_l, _ = _hgrn(hq, hf, hi, lb2, s_ctx, hg_consts, 0, t)

        dt = jnp.exp(ssm_log_dt[l].astype(F32))[..., None]
        gd = lambda a: jnp.swapaxes(a.astype(F32), 0, 1)
        s5_params = (gd(ssm_lam_re[l] * dt), gd(ssm_lam_im[l] * dt), gd(ssm_lam_re[l]), gd(ssm_lam_im[l]),
                     gd(ssm_b_re[l]), gd(ssm_b_im[l]), gd(ssm_c_re[l]), gd(ssm_c_im[l]))
        dsk = jnp.broadcast_to(ssm_d[l].astype(F32).reshape(SSM_GROUPS, SSM_GROUP, 1), (SSM_GROUPS, SSM_GROUP, S5_CHUNK))
        y_t = _s5(u_t, *s5_params, dsk, t_ctx // S5_CHUNK)

        hn = jnp.tile(hgrn_norm_g[l], HG_HEADS).reshape(1, HG_WIDTH)
        mw = (w_br_attn[l].astype(BF16), w_br_hgrn[l].astype(BF16), w_br_ssm[l].astype(BF16), w_out[l].astype(BF16),
              w_glu[l].astype(BF16), b_glu[l].reshape(1, SSM_WIDTH), hn, gmh)
        wup, wdn = w_ffn_up[l].astype(BF16), w_ffn_down[l].astype(BF16)
        x_lat = _merge(x_lat, a_lat, of_l, ob_l, hg, y_t, 0, gates, g1, *mw)
        x_lat = _ffn(x_lat, sh2, n2 * (1.0 + sc2), g2, wup, wdn)
        if with_ctx:
            bc = lambda v: jnp.broadcast_to(v, (bsz, 1, d))
            x_ctx = _merge(x_ctx, a_ctx, of_c, ob_c, hg, y_t, t, gates, bc(cg1), *mw)
            x_ctx = _ffn(x_ctx, bc(csh2), bc(n2 * (1.0 + csc2)), bc(cg2), wup, wdn)
    return x_lat
```
